```python
import jax, jax.numpy as jnp
from jax import lax
import numpy as np

D_MODEL = 2048
BATCH = 4
SEQ = 4096
DEPTH = 1

CHUNK = 64
Q_BLOCK = 128
MIX_WIDTH = D_MODEL

RWKV_HEAD_DIM = 64
RWKV_WIDTH = MIX_WIDTH // 2
RWKV_HEADS = RWKV_WIDTH // RWKV_HEAD_DIM
DECAY_LORA = 96
ICLR_LORA = 96
GATE_LORA = 256
GN_EPS = 64e-5

QK_NOPE_DIM = 128
QK_ROPE_DIM = 64
V_HEAD_DIM = 128
MLA_WIDTH = MIX_WIDTH - RWKV_WIDTH
MLA_HEADS = MLA_WIDTH // V_HEAD_DIM
Q_LORA_RANK = 512
KV_LORA_RANK = 256
ROPE_THETA = 10000.0

D_FF = 5632
CONV_WIDTH = 3

NORM_EPS = 1e-6
N_MOD = 6

RWKV_COLS = 3 * RWKV_WIDTH + DECAY_LORA + ICLR_LORA + GATE_LORA
MLA_COLS = Q_LORA_RANK + KV_LORA_RANK + QK_ROPE_DIM
IN_COLS = RWKV_COLS + MLA_COLS

kernel_name = "hybrid_rwkv7_mla_convffn_block"


def rms_norm(x, g):
    xf = x.astype(jnp.float32)
    y = xf * lax.rsqrt(jnp.mean(xf * xf, axis=-1, keepdims=True) + NORM_EPS)
    return (y * g.astype(jnp.float32)).astype(x.dtype)


def shift_prev(u):
    return jnp.pad(u, ((0, 0), (1, 0), (0, 0)))[:, :-1]


def apply_rope(u, cos, sin):
    u1, u2 = jnp.split(u, 2, axis=-1)
    return jnp.concatenate([u1 * cos - u2 * sin, u1 * sin + u2 * cos], axis=-1)


def rwkv7_mixer(p, mu_shift, w0, w_decay_up, a0, w_iclr_up, w_gate_up, k_k, k_a, r_k, ln_x_g, ln_x_b):
    B, S, _ = p.shape
    H, N, W = RWKV_HEADS, RWKV_HEAD_DIM, RWKV_WIDTH
    f32 = jnp.float32
    p = p + (shift_prev(p) - p) * mu_shift
    r, k, v, wd, ad, gd = jnp.split(
        p, [W, 2 * W, 3 * W, 3 * W + DECAY_LORA, 3 * W + DECAY_LORA + ICLR_LORA], axis=-1)
    w_log = -jax.nn.softplus(-(w0 + jnp.tanh(wd) @ w_decay_up)) - 0.5
    decay = jnp.exp(-jnp.exp(w_log.astype(f32)))
    a = jax.nn.sigmoid(a0 + ad @ w_iclr_up)
    g = jax.nn.sigmoid(gd) @ w_gate_up
    kk = (k * k_k).reshape(B, S, H, N).astype(f32)
    kk = kk / jnp.maximum(jnp.sqrt(jnp.sum(kk * kk, axis=-1, keepdims=True)), 1e-12)
    k = k * (1.0 + (a - 1.0) * k_a)

    def heads(t):
        return t.reshape(B, S, H, N).astype(f32)

    r_h, k_h, v_h, w_h, a_h = heads(r), heads(k), heads(v), heads(decay), heads(a)

    def step(state, inp):
        r_t, w_t, k_t, v_t, kk_t, a_t = inp
        sa = jnp.einsum('bhvk,bhk->bhv', state, -kk_t)
        state = (state * w_t[:, :, None, :]
                 + sa[..., None] * (kk_t * a_t)[:, :, None, :]
                 + v_t[..., None] * k_t[:, :, None, :])
        y_t = jnp.einsum('bhvk,bhk->bhv', state, r_t)
        return state, y_t

    xs = tuple(jnp.moveaxis(t, 1, 0) for t in (r_h, w_h, k_h, v_h, kk, a_h))
    _, y = lax.scan(step, jnp.zeros((B, H, N, N), f32), xs)
    y = jnp.moveaxis(y, 0, 1)
    mean = jnp.mean(y, axis=-1, keepdims=True)
    var = jnp.mean(jnp.square(y - mean), axis=-1, keepdims=True)
    y = ((y - mean) * lax.rsqrt(var + GN_EPS)).reshape(B, S, W) * ln_x_g + ln_x_b
    bonus = jnp.sum(r_h * k_h * r_k, axis=-1, keepdims=True) * v_h
    y = (y + bonus.reshape(B, S, W)) * g
    return y.astype(p.dtype)


def mla_mixer(p, positions, q_norm_g, w_q_up, kv_norm_g, w_kv_up):
    B, S, _ = p.shape
    H, DN, DR, DV = MLA_HEADS, QK_NOPE_DIM, QK_ROPE_DIM, V_HEAD_DIM
    f32 = jnp.float32
    q_lat, kv_lat, k_rope = jnp.split(p, [Q_LORA_RANK, Q_LORA_RANK + KV_LORA_RANK], axis=-1)
    q = (rms_norm(q_lat, q_norm_g) @ w_q_up).reshape(B, S, H, DN + DR)
    q_nope, q_rope = q[..., :DN], q[..., DN:]
    kv = (rms_norm(kv_lat, kv_norm_g) @ w_kv_up).reshape(B, S, H, DN + DV)
    k_nope, v = kv[..., :DN], kv[..., DN:]
    inv_freq = ROPE_THETA ** (-jnp.arange(0, DR, 2, dtype=f32) / DR)
    ang = positions.astype(f32)[..., None] * inv_freq
    cos, sin = jnp.cos(ang), jnp.sin(ang)
    q_rope = apply_rope(q_rope.astype(f32), cos[:, :, None], sin[:, :, None]).astype(q.dtype)
    k_rope = apply_rope(k_rope.astype(f32), cos, sin).astype(q.dtype)
    scale = (DN + DR) ** -0.5
    n_blk = S // Q_BLOCK
    k_chunk = jnp.arange(S) // CHUNK

    def q_block(args):
        i, qn, qr = args
        s = (jnp.einsum('bqhd,bkhd->bhqk', qn, k_nope)
             + jnp.einsum('bqhr,bkr->bhqk', qr, k_rope)).astype(f32) * scale
        q_chunk = (i * Q_BLOCK + jnp.arange(Q_BLOCK)) // CHUNK
        mask = q_chunk[:, None] >= k_chunk[None, :]
        s = jnp.where(mask[None, None], s, -jnp.inf)
        prob = jax.nn.softmax(s, axis=-1).astype(v.dtype)
        return jnp.einsum('bhqk,bkhd->bqhd', prob, v)

    qn_b = q_nope.reshape(B, n_blk, Q_BLOCK, H, DN).transpose(1, 0, 2, 3, 4)
    qr_b = q_rope.reshape(B, n_blk, Q_BLOCK, H, DR).transpose(1, 0, 2, 3, 4)
    o = lax.map(q_block, (jnp.arange(n_blk), qn_b, qr_b))
    return o.transpose(1, 0, 2, 3, 4).reshape(B, S, H * DV)


def conv_ffn(h, w_ffn_up, conv_w, conv_b, w_ffn_down):
    S = h.shape[1]
    u = h @ w_ffn_up
    up = jnp.pad(u, ((0, 0), (CONV_WIDTH - 1, 0), (0, 0)))
    uc = conv_b + conv_w[CONV_WIDTH - 1] * u
    for j in range(CONV_WIDTH - 1):
        uc = uc + conv_w[j] * up[:, j:j + S]
    gate, val = jnp.split(uc, 2, axis=-1)
    return (jax.nn.gelu(gate, approximate=True) * val) @ w_ffn_down


def setup_inputs(seed: int = 0) -> dict:
    key = jax.random.key(seed)
    ks = iter(jax.random.split(key, 40))
    L, D = DEPTH, D_MODEL
    f32 = jnp.float32

    def nrm(shape, scale):
        return jax.random.normal(next(ks), shape, f32) * scale

    def gain(shape):
        return 1.0 + nrm(shape, 0.02)

    x = nrm((BATCH, SEQ, D), 1.0)
    c = nrm((BATCH, D), 1.0)
    start = jax.random.randint(next(ks), (BATCH, 1), 0, 8192, dtype=jnp.int32)
    positions = (start + jnp.arange(SEQ, dtype=jnp.int32)[None, :]).astype(jnp.int32)
    return {
        "x": x,
        "c": c,
        "positions": positions,
        "w_mod": nrm((L, D, N_MOD * D), D ** -0.5),
        "b_mod": nrm((L, N_MOD * D), 0.01),
        "g_pre_mix": gain((L, D)),
        "w_in": nrm((L, D, IN_COLS), D ** -0.5),
        "mu_shift": jax.random.uniform(next(ks), (L, RWKV_COLS), f32),
        "w0": jax.random.uniform(next(ks), (L, RWKV_WIDTH), f32, -6.0, 1.0),
        "w_decay_up": nrm((L, DECAY_LORA, RWKV_WIDTH), 0.1 * DECAY_LORA ** -0.5),
        "a0": nrm((L, RWKV_WIDTH), 0.1),
        "w_iclr_up": nrm((L, ICLR_LORA, RWKV_WIDTH), 0.5 * ICLR_LORA ** -0.5),
        "w_gate_up": nrm((L, GATE_LORA, RWKV_WIDTH), GATE_LORA ** -0.5),
        "k_k": 0.85 + nrm((L, RWKV_WIDTH), 0.02),
        "k_a": gain((L, RWKV_WIDTH)),
        "r_k": nrm((L, RWKV_HEADS, RWKV_HEAD_DIM), 0.1),
        "ln_x_g": gain((L, RWKV_WIDTH)),
        "ln_x_b": nrm((L, RWKV_WIDTH), 0.01),
        "q_norm_g": gain((L, Q_LORA_RANK)),
        "w_q_up": nrm((L, Q_LORA_RANK, MLA_HEADS * (QK_NOPE_DIM + QK_ROPE_DIM)), Q_LORA_RANK ** -0.5),
        "kv_norm_g": gain((L, KV_LORA_RANK)),
        "w_kv_up": nrm((L, KV_LORA_RANK, MLA_HEADS * (QK_NOPE_DIM + V_HEAD_DIM)), KV_LORA_RANK ** -0.5),
        "w_out": nrm((L, MIX_WIDTH, D), MIX_WIDTH ** -0.5),
        "g_post_mix": gain((L, D)),
        "g_pre_ffn": gain((L, D)),
        "w_ffn_up": nrm((L, D, 2 * D_FF), D ** -0.5),
        "conv_w": nrm((L, CONV_WIDTH, 2 * D_FF), CONV_WIDTH ** -0.5),
        "conv_b": nrm((L, 2 * D_FF), 0.01),
        "w_ffn_down": nrm((L, D_FF, D), D_FF ** -0.5),
        "g_post_ffn": gain((L, D)),
    }


def reference(x, c, positions, w_mod, b_mod, g_pre_mix, w_in, mu_shift, w0, w_decay_up, a0,
              w_iclr_up, w_gate_up, k_k, k_a, r_k, ln_x_g, ln_x_b, q_norm_g, w_q_up, kv_norm_g,
              w_kv_up, w_out, g_post_mix, g_pre_ffn, w_ffn_up, conv_w, conv_b, w_ffn_down, g_post_ffn):
    silu_c = jax.nn.silu(c)
    for l in range(DEPTH):
        mod = (silu_c @ w_mod[l] + b_mod[l])[:, None, :]
        shift_m, scale_m, gate_m, shift_f, scale_f, gate_f = jnp.split(mod, N_MOD, axis=-1)

        h = rms_norm(x, g_pre_mix[l]) * (1.0 + scale_m) + shift_m
        p = h @ w_in[l]
        y_rwkv = rwkv7_mixer(p[..., :RWKV_COLS], mu_shift[l], w0[l], w_decay_up[l], a0[l],
                             w_iclr_up[l], w_gate_up[l], k_k[l], k_a[l], r_k[l],
                             ln_x_g[l], ln_x_b[l])
        y_mla = mla_mixer(p[..., RWKV_COLS:], positions, q_norm_g[l], w_q_up[l],
                          kv_norm_g[l], w_kv_up[l])
        o = jnp.concatenate([y_rwkv, y_mla], axis=-1) @ w_out[l]
        x = x + gate_m * rms_norm(o, g_post_mix[l])

        h = rms_norm(x, g_pre_ffn[l]) * (1.0 + scale_f) + shift_f
        o = conv_ffn(h, w_ffn_up[l], conv_w[l], conv_b[l], w_ffn_down[l])
        x = x + gate_f * rms_norm(o, g_post_ffn[l])
    return x
```

```python
import functools

import numpy as np
import jax
import jax.numpy as jnp
from jax import lax
from jax.experimental import pallas as pl
from jax.experimental.pallas import tpu as pltpu

F32 = jnp.float32
BF16 = jnp.bfloat16

D_MODEL = 2048
CHUNK = 64
RWKV_HEAD_DIM = 64
RWKV_WIDTH = 1024
RWKV_HEADS = 16
DECAY_LORA = 96
ICLR_LORA = 96
GATE_LORA = 256
GN_EPS = 64e-5
QK_NOPE_DIM = 128
QK_ROPE_DIM = 64
V_HEAD_DIM = 128
MLA_WIDTH = 1024
MLA_HEADS = 8
Q_LORA_RANK = 512
KV_LORA_RANK = 256
ROPE_THETA = 10000.0
D_FF = 5632
NORM_EPS = 1e-6
N_MOD = 6

LANES = 128
HALO = 16
GROUP = 256
HEADS_PER_GROUP = GROUP // RWKV_HEAD_DIM
N_GROUPS = RWKV_WIDTH // GROUP
QK_PAD = 256
IN_COLS_PAD = 4608
VMEM_LIMIT = 56 * 1024 * 1024


def _bdot(a, b):
    return jnp.dot(a.astype(BF16), b.astype(BF16), preferred_element_type=F32)


def _split2(x):
    hi = x.astype(BF16)
    lo = (x - hi.astype(F32)).astype(BF16)
    return hi, lo


def _sigmoid(x):
    return 1.0 / (1.0 + jnp.exp(-x))


def _rms(xv, g):
    ms = jnp.mean(xv * xv, axis=-1, keepdims=True)
    return xv * lax.rsqrt(ms + NORM_EPS) * g


def _mod_kernel(c_ref, w_ref, b_ref, o_ref):
    c = c_ref[...]
    s = c * _sigmoid(c)
    o_ref[...] = _bdot(s, w_ref[...]) + b_ref[...]


def _mod_call(c_pad, w_mod, b_mod, tn=1024):
    rows, d = c_pad.shape
    n = w_mod.shape[1]
    return pl.pallas_call(
        _mod_kernel,
        grid=(n // tn,),
        in_specs=[pl.BlockSpec((rows, d), lambda j: (0, 0)),
                  pl.BlockSpec((d, tn), lambda j: (0, j)),
                  pl.BlockSpec((1, tn), lambda j: (0, j))],
        out_specs=pl.BlockSpec((rows, tn), lambda j: (0, j)),
        out_shape=jax.ShapeDtypeStruct((rows, n), F32),
        compiler_params=pltpu.CompilerParams(dimension_semantics=("arbitrary",),
                                             vmem_limit_bytes=VMEM_LIMIT),
        name="mod",
    )(c_pad, w_mod, b_mod)


def _inproj_kernel(x_ref, xh_ref, mod_ref, g_ref, w_ref, mu_ref, o_ref, *, tiles_per_seq):
    i = pl.program_id(1)
    first = (i % tiles_per_seq) == 0
    shift = mod_ref[0, 0:1, :]
    scale = mod_ref[0, 1:2, :]
    g = g_ref[...]
    h = _rms(x_ref[...], g) * (1.0 + scale) + shift
    hh = _rms(xh_ref[...], g) * (1.0 + scale) + shift
    hh = jnp.where(first, 0.0, hh)
    hcat = jnp.concatenate([hh, h], axis=0).astype(BF16)
    p = jnp.dot(hcat, w_ref[...], preferred_element_type=F32)
    prev = pltpu.roll(p, 1, axis=0)
    out = p + (prev - p) * mu_ref[...]
    o_ref[...] = out[HALO:, :]


def _inproj_call(x2, mod3, g_pre, w_in_p, mu_p, seq, tm, tn):
    t, d = x2.shape
    n = w_in_p.shape[1]
    tps = seq // tm
    hb = tm // HALO
    return pl.pallas_call(
        functools.partial(_inproj_kernel, tiles_per_seq=tps),
        grid=(n // tn, t // tm),
        in_specs=[pl.BlockSpec((tm, d), lambda j, i: (i, 0)),
                  pl.BlockSpec((HALO, d), lambda j, i: (jnp.maximum(i * hb - 1, 0), 0)),
                  pl.BlockSpec((1, N_MOD, d), lambda j, i: (i // tps, 0, 0)),
                  pl.BlockSpec((1, d), lambda j, i: (0, 0)),
                  pl.BlockSpec((d, tn), lambda j, i: (0, j)),
                  pl.BlockSpec((1, tn), lambda j, i: (0, j))],
        out_specs=pl.BlockSpec((tm, tn), lambda j, i: (i, j)),
        out_shape=jax.ShapeDtypeStruct((t, n), F32),
        compiler_params=pltpu.CompilerParams(dimension_semantics=("arbitrary", "arbitrary"),
                                             vmem_limit_bytes=VMEM_LIMIT),
        name="inproj",
    )(x2, x2, mod3, g_pre, w_in_p, mu_p)


def _rwkv_masks():
    i = np.arange(GROUP)[:, None]
    j = np.arange(GROUP)[None, :]
    n = RWKV_HEAD_DIM
    bd = (i // n) == (j // n)
    sl = bd & ((j % n) < (i % n))
    mk = np.stack([
        bd,
        sl & ((i // 8) == (j // 8)),
        bd & ((i // 16) == (j // 16)) & ((i // 8) > (j // 8)),
        bd & ((i // 32) == (j // 32)) & ((i // 16) > (j // 16)),
        bd & ((i // 32) > (j // 32)),
        i == j,
    ]).astype(np.float32)
    t = np.arange(CHUNK)[:, None]
    mc = np.stack([(j % n) < t, (j % n) <= t]).astype(np.float32)
    tri = (np.arange(CHUNK)[None, :] <= t).astype(np.float32)
    return mk, mc, tri


def _rwkv_kernel(r_ref, k_ref, v_ref, lo_ref, vec_ref, wd_ref, wa_ref, wg_ref,
                 mk_ref, mc_ref, tri_ref, e_ref, o_ref, st_ref):
    @pl.when(pl.program_id(1) == 0)
    def _reset_state():
        st_ref[...] = jnp.zeros_like(st_ref)

    r = r_ref[0]
    k = k_ref[0]
    v = v_ref[0]
    lo = lo_ref[0]
    w0 = vec_ref[0:1, :]
    a0 = vec_ref[1:2, :]
    k_k = vec_ref[2:3, :]
    k_a = vec_ref[3:4, :]
    r_k = vec_ref[4:5, :]
    ln_g = vec_ref[5:6, :]
    ln_b = vec_ref[6:7, :]

    z = -(w0 + _bdot(jnp.tanh(lo[:, 0:LANES]), wd_ref[...]))
    softplus = jnp.maximum(z, 0.0) + jnp.log(1.0 + jnp.exp(-jnp.abs(z)))
    lw = -jnp.exp(-softplus - 0.5)
    a = _sigmoid(a0 + _bdot(lo[:, LANES:2 * LANES], wa_ref[...]))
    gate = _bdot(_sigmoid(lo[:, 2 * LANES:4 * LANES]), wg_ref[...])
    kk = k * k_k
    kp = k * (1.0 + (a - 1.0) * k_a)

    tri = tri_ref[...]
    l1 = lw.astype(BF16)
    rem = lw - l1.astype(F32)
    l2 = rem.astype(BF16)
    l3 = (rem - l2.astype(F32)).astype(BF16)
    cum = (jnp.dot(tri, l1, preferred_element_type=F32)
           + jnp.dot(tri, l2, preferred_element_type=F32)
           + jnp.dot(tri, l3, preferred_element_type=F32))
    cum_last = cum[CHUNK - 1:CHUNK, :]
    e_pos = jnp.exp(cum)
    e_pos_x = jnp.exp(cum - lw)
    e_neg = jnp.exp(-cum)
    e_end = jnp.exp(cum_last - cum)
    p_end = jnp.exp(cum_last)

    m_bd = mk_ref[0]
    m_b8 = mk_ref[1]
    m_l16 = mk_ref[2]
    m_l32 = mk_ref[3]
    m_l64 = mk_ref[4]
    eye = mk_ref[5]
    c_strict = mc_ref[0]
    c_incl = mc_ref[1]
    ones_bd = e_ref[...]

    def head_sum(x):
        hi, lo_ = _split2(x)
        return (jnp.dot(hi, ones_bd, preferred_element_type=F32)
                + jnp.dot(lo_, ones_bd, preferred_element_type=F32))

    def tile4(x):
        return jnp.concatenate([x] * HEADS_PER_GROUP, axis=0) * m_bd

    def fold4(x):
        return (x[0:CHUNK] + x[CHUNK:2 * CHUNK] + x[2 * CHUNK:3 * CHUNK] + x[3 * CHUNK:4 * CHUNK])

    def dot3(x, y):
        xh, xl = _split2(x)
        yh, yl = _split2(y)
        return (jnp.dot(xh, yh, preferred_element_type=F32)
                + jnp.dot(xh, yl, preferred_element_type=F32)
                + jnp.dot(xl, yh, preferred_element_type=F32))

    for gi in range(N_GROUPS):
        sl = slice(gi * GROUP, (gi + 1) * GROUP)
        r_g, v_g, kp_g, a_g = r[:, sl], v[:, sl], kp[:, sl], a[:, sl]
        kk_g = kk[:, sl]
        norm = jnp.sqrt(head_sum(kk_g * kk_g))
        kkn = kk_g / jnp.maximum(norm, 1e-12)
        ka = kkn * a_g
        a_bar = -kkn * e_pos_x[:, sl]
        r_bar = r_g * e_pos[:, sl]
        b_til = ka * e_neg[:, sl]
        k_til = kp_g * e_neg[:, sl]
        b_hat = ka * e_end[:, sl]
        k_hat = kp_g * e_end[:, sl]

        lhs = jnp.concatenate([a_bar, r_bar], axis=0)
        rhs = jnp.concatenate([tile4(b_til), tile4(k_til)], axis=0)
        scores = lax.dot_general(lhs.astype(BF16), rhs.astype(BF16),
                                 (((1,), (1,)), ((), ())), preferred_element_type=F32)
        a_ab = scores[0:CHUNK, 0:GROUP] * c_strict
        a_ak = scores[0:CHUNK, GROUP:] * c_strict
        a_rb = scores[CHUNK:, 0:GROUP] * c_incl
        a_rk = scores[CHUNK:, GROUP:] * c_incl

        a_full = jnp.concatenate([a_ab] * HEADS_PER_GROUP, axis=0) * m_bd
        n1 = a_full * m_b8
        n2 = _bdot(n1, n1)
        n4 = _bdot(n2, n2)
        inv = eye + n1
        inv = inv + _bdot(inv, n2)
        inv = inv + _bdot(inv, n4)
        for m_off in (m_l16, m_l32, m_l64):
            inv = inv + _bdot(_bdot(inv, a_full * m_off), inv)

        v_bd = tile4(v_g)
        akv = _bdot(a_ak, v_bd)
        wu = _bdot(inv, jnp.concatenate([tile4(a_bar), tile4(akv)], axis=1))
        w_cat = fold4(wu[:, 0:GROUP])
        u_cat = fold4(wu[:, GROUP:])
        ry = _bdot(a_rb, wu)
        r_hat = r_bar + ry[:, 0:GROUP]
        y_hat = ry[:, GROUP:] + _bdot(a_rk, v_bd)

        state = st_ref[gi]
        y = dot3(r_hat, state) + y_hat

        lhs_t = jnp.concatenate([b_hat, k_hat], axis=0)
        rhs_t = jnp.concatenate(
            [jnp.concatenate([w_cat, u_cat], axis=1),
             jnp.concatenate([jnp.zeros_like(v_g), v_g], axis=1)], axis=0)
        mc = lax.dot_general(lhs_t.astype(BF16), rhs_t.astype(BF16),
                             (((0,), (0,)), ((), ())), preferred_element_type=F32)
        m_t = eye * p_end[:, sl] + mc[:, 0:GROUP] * m_bd
        c_t = mc[:, GROUP:] * m_bd
        st_ref[gi] = dot3(m_t, state) + c_t

        inv_n = 1.0 / RWKV_HEAD_DIM
        mean = head_sum(y) * inv_n
        dlt = y - mean
        var = head_sum(dlt * dlt) * inv_n
        yn = dlt * lax.rsqrt(var + GN_EPS) * ln_g[:, sl] + ln_b[:, sl]
        bonus = head_sum(r_g * kp_g * r_k[:, sl]) * v_g
        o_ref[0, :, sl] = ((yn + bonus) * gate[:, sl]).astype(o_ref.dtype)


def _rwkv_call(p3, vecs, wd, wa, wg):
    b, s, _ = p3.shape
    w = RWKV_WIDTH
    mk, mc, tri = _rwkv_masks()
    mk = jnp.asarray(mk)
    mc = jnp.asarray(mc)
    tri = jnp.asarray(tri, dtype=BF16)
    ones_bd = mk[0].astype(BF16)
    const2 = lambda bi, ci: (0, 0)
    const3 = lambda bi, ci: (0, 0, 0)
    return pl.pallas_call(
        _rwkv_kernel,
        grid=(b, s // CHUNK),
        in_specs=[pl.BlockSpec((1, CHUNK, w), lambda bi, ci: (bi, ci, 0)),
                  pl.BlockSpec((1, CHUNK, w), lambda bi, ci: (bi, ci, 1)),
                  pl.BlockSpec((1, CHUNK, w), lambda bi, ci: (bi, ci, 2)),
                  pl.BlockSpec((1, CHUNK, 4 * LANES), lambda bi, ci: (bi, ci, 3 * w // (4 * LANES))),
                  pl.BlockSpec(vecs.shape, const2),
                  pl.BlockSpec(wd.shape, const2),
                  pl.BlockSpec(wa.shape, const2),
                  pl.BlockSpec(wg.shape, const2),
                  pl.BlockSpec(mk.shape, const3),
                  pl.BlockSpec(mc.shape, const3),
                  pl.BlockSpec(tri.shape, const2),
                  pl.BlockSpec(ones_bd.shape, const2)],
        out_specs=pl.BlockSpec((1, CHUNK, w), lambda bi, ci: (bi, ci, 0)),
        out_shape=jax.ShapeDtypeStruct((b, s, w), BF16),
        scratch_shapes=[pltpu.VMEM((N_GROUPS, GROUP, GROUP), F32)],
        compiler_params=pltpu.CompilerParams(dimension_semantics=("arbitrary", "arbitrary"),
                                             vmem_limit_bytes=VMEM_LIMIT),
        name="rwkv",
    )(p3, p3, p3, p3, vecs, wd, wa, wg, mk, mc, tri, ones_bd)


def _mlaproj_kernel(pq_ref, pkv_ref, pos_ref, invf_ref, gq_ref, gkv_ref, wq_ref, wkv_ref,
                    q_ref, k_ref, v_ref):
    q = _bdot(_rms(pq_ref[...], gq_ref[...]), wq_ref[...])
    pkv = pkv_ref[...]
    kv = _bdot(_rms(pkv[:, 0:KV_LORA_RANK], gkv_ref[...]), wkv_ref[...])
    ang = pos_ref[...].astype(F32) * invf_ref[...]
    lane = lax.broadcasted_iota(jnp.int32, ang.shape, 1)
    half = QK_ROPE_DIM // 2
    cos_f = jnp.where(lane < QK_ROPE_DIM, jnp.cos(ang), 0.0)
    sin = jnp.sin(ang)
    sin_f = jnp.where(lane < half, -sin, jnp.where(lane < QK_ROPE_DIM, sin, 0.0))
    k_rot = pkv[:, KV_LORA_RANK:KV_LORA_RANK + LANES] * cos_f + pkv[:, KV_LORA_RANK + LANES:] * sin_f
    scale = (QK_NOPE_DIM + QK_ROPE_DIM) ** -0.5
    swap0 = MLA_HEADS * QK_PAD
    for h in range(MLA_HEADS):
        q_nope = q[:, h * QK_PAD:h * QK_PAD + LANES]
        q_rot = (q[:, h * QK_PAD + LANES:(h + 1) * QK_PAD] * cos_f
                 + q[:, swap0 + h * LANES:swap0 + (h + 1) * LANES] * sin_f)
        q_ref[0, h] = (jnp.concatenate([q_nope, q_rot], axis=1) * scale).astype(q_ref.dtype)
        k_ref[0, h] = jnp.concatenate([kv[:, h * LANES:(h + 1) * LANES], k_rot], axis=1).astype(k_ref.dtype)
        v_ref[0, h] = kv[:, (MLA_HEADS + h) * LANES:(MLA_HEADS + h + 1) * LANES].astype(v_ref.dtype)


def _mlaproj_call(p2, pos2, invf, gq, gkv, wq, wkv, batch, seq, tm):
    t = p2.shape[0]
    tps = seq // tm
    blk = 4 * LANES
    q_blk = (IN_COLS_PAD - 2 * blk) // blk
    hd = MLA_HEADS
    out_map = lambda i: (i // tps, 0, i % tps, 0)
    const = lambda i: (0, 0)
    return pl.pallas_call(
        _mlaproj_kernel,
        grid=(t // tm,),
        in_specs=[pl.BlockSpec((tm, blk), lambda i: (i, q_blk)),
                  pl.BlockSpec((tm, blk), lambda i: (i, q_blk + 1)),
                  pl.BlockSpec((tm, 1), lambda i: (i, 0)),
                  pl.BlockSpec(invf.shape, const),
                  pl.BlockSpec(gq.shape, const),
                  pl.BlockSpec(gkv.shape, const),
                  pl.BlockSpec(wq.shape, const),
                  pl.BlockSpec(wkv.shape, const)],
        out_specs=[pl.BlockSpec((1, hd, tm, QK_PAD), out_map),
                   pl.BlockSpec((1, hd, tm, QK_PAD), out_map),
                   pl.BlockSpec((1, hd, tm, V_HEAD_DIM), out_map)],
        out_shape=[jax.ShapeDtypeStruct((batch, hd, seq, QK_PAD), BF16),
                   jax.ShapeDtypeStruct((batch, hd, seq, QK_PAD), BF16),
                   jax.ShapeDtypeStruct((batch, hd, seq, V_HEAD_DIM), BF16)],
        compiler_params=pltpu.CompilerParams(dimension_semantics=("arbitrary",),
                                             vmem_limit_bytes=VMEM_LIMIT),
        name="mlaproj",
    )(p2, p2, pos2, invf, gq, gkv, wq, wkv)


def _attn_kernel(q_ref, k_ref, v_ref, o_ref, *, tq):
    i = pl.program_id(2)
    q = q_ref[0, 0]

    def scores(start):
        kj = k_ref[0, 0, pl.ds(start, tq), :]
        return lax.dot_general(q, kj, (((1,), (1,)), ((), ())), preferred_element_type=F32)

    def update(carry, s, start):
        m, l, acc = carry
        m_new = jnp.maximum(m, jnp.max(s, axis=-1, keepdims=True))
        alpha = jnp.exp(m - m_new)
        p = jnp.exp(s - m_new)
        l = alpha * l + jnp.sum(p, axis=-1, keepdims=True)
        vj = v_ref[0, 0, pl.ds(start, tq), :]
        acc = alpha * acc + jnp.dot(p.astype(BF16), vj, preferred_element_type=F32)
        return m_new, l, acc

    def body(j, carry):
        start = pl.multiple_of(j * tq, tq)
        return update(carry, scores(start), start)

    init = (jnp.full((tq, 1), -jnp.inf, F32), jnp.zeros((tq, 1), F32),
            jnp.zeros((tq, V_HEAD_DIM), F32))
    carry = lax.fori_loop(0, i, body, init)
    start = pl.multiple_of(i * tq, tq)
    row = lax.broadcasted_iota(jnp.int32, (tq, tq), 0) // CHUNK
    col = lax.broadcasted_iota(jnp.int32, (tq, tq), 1) // CHUNK
    s = jnp.where(row >= col, scores(start), -jnp.inf)
    _, l, acc = update(carry, s, start)
    o_ref[0] = (acc / l).astype(o_ref.dtype)


def _attn_call(q4, k4, v4, tq):
    b, h, s, _ = q4.shape
    return pl.pallas_call(
        functools.partial(_attn_kernel, tq=tq),
        grid=(b, h, s // tq),
        in_specs=[pl.BlockSpec((1, 1, tq, QK_PAD), lambda bi, hi, i: (bi, hi, i, 0)),
                  pl.BlockSpec((1, 1, s, QK_PAD), lambda bi, hi, i: (bi, hi, 0, 0)),
                  pl.BlockSpec((1, 1, s, V_HEAD_DIM), lambda bi, hi, i: (bi, hi, 0, 0))],
        out_specs=pl.BlockSpec((1, tq, V_HEAD_DIM), lambda bi, hi, i: (bi, i, hi)),
        out_shape=jax.ShapeDtypeStruct((b, s, h * V_HEAD_DIM), BF16),
        compiler_params=pltpu.CompilerParams(
            dimension_semantics=("arbitrary", "arbitrary", "arbitrary"),
            vmem_limit_bytes=VMEM_LIMIT),
        name="attn",
    )(q4, k4, v4)


def _outproj_kernel(yr_ref, ym_ref, x_ref, mod_ref, g_ref, w_ref, o_ref):
    half = RWKV_WIDTH
    o = (jnp.dot(yr_ref[...], w_ref[0:half, :], preferred_element_type=F32)
         + jnp.dot(ym_ref[...], w_ref[half:, :], preferred_element_type=F32))
    gate = mod_ref[0, 2:3, :]
    o_ref[...] = x_ref[...] + gate * _rms(o, g_ref[...])


def _outproj_call(yr, ym, x2, mod3, g_post, w_out, seq, tm):
    t, d = x2.shape
    tps = seq // tm
    return pl.pallas_call(
        _outproj_kernel,
        grid=(t // tm,),
        in_specs=[pl.BlockSpec((tm, RWKV_WIDTH), lambda i: (i, 0)),
                  pl.BlockSpec((tm, MLA_WIDTH), lambda i: (i, 0)),
                  pl.BlockSpec((tm, d), lambda i: (i, 0)),
                  pl.BlockSpec((1, N_MOD, d), lambda i: (i // tps, 0, 0)),
                  pl.BlockSpec((1, d), lambda i: (0, 0)),
                  pl.BlockSpec(w_out.shape, lambda i: (0, 0))],
        out_specs=pl.BlockSpec((tm, d), lambda i: (i, 0)),
        out_shape=jax.ShapeDtypeStruct((t, d), F32),
        compiler_params=pltpu.CompilerParams(dimension_semantics=("arbitrary",),
                                             vmem_limit_bytes=VMEM_LIMIT),
        name="outproj",
    )(yr, ym, x2, mod3, g_post, w_out)


def _gelu_tanh(x):
    c = 0.7978845608028654
    return 0.5 * x * (1.0 + jnp.tanh(c * (x + 0.044715 * (x * x * x))))


def _ffn_kernel(x_ref, xh_ref, mod_ref, gpre_ref, wg_ref, wv_ref, cwg_ref, cwv_ref,
                cbg_ref, cbv_ref, wd_ref, gpost_ref, o_ref, h_ref, acc_ref, *, tiles_per_seq):
    i = pl.program_id(0)
    j = pl.program_id(1)

    @pl.when(j == 0)
    def _prologue():
        shift = mod_ref[0, 3:4, :]
        scale = mod_ref[0, 4:5, :]
        g = gpre_ref[...]
        hh = _rms(xh_ref[...], g) * (1.0 + scale) + shift
        hh = jnp.where((i % tiles_per_seq) == 0, 0.0, hh)
        h_ref[0:HALO, :] = hh.astype(BF16)
        h_ref[HALO:, :] = (_rms(x_ref[...], g) * (1.0 + scale) + shift).astype(BF16)
        acc_ref[...] = jnp.zeros_like(acc_ref)

    hb = h_ref[...]

    def conv(u, cw, cb):
        u1 = pltpu.roll(u, 1, axis=0)
        u2 = pltpu.roll(u, 2, axis=0)
        return (cb + cw[2:3, :] * u + cw[1:2, :] * u1 + cw[0:1, :] * u2)[HALO:, :]

    gc = conv(jnp.dot(hb, wg_ref[...], preferred_element_type=F32), cwg_ref[...], cbg_ref[...])
    vc = conv(jnp.dot(hb, wv_ref[...], preferred_element_type=F32), cwv_ref[...], cbv_ref[...])
    act = (_gelu_tanh(gc) * vc).astype(BF16)
    acc_ref[...] += jnp.dot(act, wd_ref[...], preferred_element_type=F32)

    @pl.when(j == pl.num_programs(1) - 1)
    def _epilogue():
        gate = mod_ref[0, 5:6, :]
        o_ref[...] = x_ref[...] + gate * _rms(acc_ref[...], gpost_ref[...])


def _ffn_call(x2, mod3, g_pre, w_up, conv_w, conv_b, w_down, g_post, seq, tm, tf):
    t, d = x2.shape
    tps = seq // tm
    hb = tm // HALO
    nf = D_FF // tf
    return pl.pallas_call(
        functools.partial(_ffn_kernel, tiles_per_seq=tps),
        grid=(t // tm, nf),
        in_specs=[pl.BlockSpec((tm, d), lambda i, j: (i, 0)),
                  pl.BlockSpec((HALO, d), lambda i, j: (jnp.maximum(i * hb - 1, 0), 0)),
                  pl.BlockSpec((1, N_MOD, d), lambda i, j: (i // tps, 0, 0)),
                  pl.BlockSpec((1, d), lambda i, j: (0, 0)),
                  pl.BlockSpec((d, tf), lambda i, j: (0, j)),
                  pl.BlockSpec((d, tf), lambda i, j: (0, nf + j)),
                  pl.BlockSpec((3, tf), lambda i, j: (0, j)),
                  pl.BlockSpec((3, tf), lambda i, j: (0, nf + j)),
                  pl.BlockSpec((1, tf), lambda i, j: (0, j)),
                  pl.BlockSpec((1, tf), lambda i, j: (0, nf + j)),
                  pl.BlockSpec((tf, d), lambda i, j: (j, 0)),
                  pl.BlockSpec((1, d), lambda i, j: (0, 0))],
        out_specs=pl.BlockSpec((tm, d), lambda i, j: (i, 0)),
        out_shape=jax.ShapeDtypeStruct((t, d), F32),
        scratch_shapes=[pltpu.VMEM((tm + HALO, d), BF16), pltpu.VMEM((tm, d), F32)],
        compiler_params=pltpu.CompilerParams(dimension_semantics=("arbitrary", "arbitrary"),
                                             vmem_limit_bytes=VMEM_LIMIT),
        name="ffn",
    )(x2, x2, mod3, g_pre, w_up, w_up, conv_w, conv_w, conv_b, conv_b, w_down, g_post)


def _pad_cols(w, n):
    return jnp.pad(w, ((0, 0), (0, n - w.shape[1])))


def _layout_w_in(w_in, mu_shift):
    w3 = 3 * RWKV_WIDTH
    o_wd = w3
    o_ad = o_wd + DECAY_LORA
    o_gd = o_ad + ICLR_LORA
    o_q = o_gd + GATE_LORA
    o_kv = o_q + Q_LORA_RANK
    o_kr = o_kv + KV_LORA_RANK
    half = QK_ROPE_DIM // 2

    def lay(m):
        kr = m[:, o_kr:o_kr + QK_ROPE_DIM]
        return jnp.concatenate([
            m[:, 0:w3],
            _pad_cols(m[:, o_wd:o_ad], LANES),
            _pad_cols(m[:, o_ad:o_gd], LANES),
            m[:, o_gd:o_q],
            m[:, o_q:o_kv],
            m[:, o_kv:o_kr],
            _pad_cols(kr, LANES),
            _pad_cols(jnp.concatenate([kr[:, half:], kr[:, :half]], axis=1), LANES),
        ], axis=1)

    mu_full = jnp.concatenate([mu_shift, jnp.zeros((w_in.shape[1] - mu_shift.shape[0],), F32)])
    w_p = lay(w_in).astype(BF16)
    mu_p = lay(mu_full[None, :])
    return w_p, mu_p


def _layout_w_q(w_q_up):
    dn, dr = QK_NOPE_DIM, QK_ROPE_DIM
    half = dr // 2
    w = w_q_up.reshape(Q_LORA_RANK, MLA_HEADS, dn + dr)
    nope, u1, u2 = w[..., :dn], w[..., dn:dn + half], w[..., dn + half:]
    z = jnp.zeros(u1.shape[:-1] + (LANES - dr,), w.dtype)
    main = jnp.concatenate([nope, u1, u2, z], axis=-1).reshape(Q_LORA_RANK, MLA_HEADS * QK_PAD)
    swap = jnp.concatenate([u2, u1, z], axis=-1).reshape(Q_LORA_RANK, MLA_HEADS * LANES)
    return jnp.concatenate([main, swap], axis=1).astype(BF16)


def _layout_w_kv(w_kv_up):
    w = w_kv_up.reshape(KV_LORA_RANK, MLA_HEADS, QK_NOPE_DIM + V_HEAD_DIM)
    kn = w[..., :QK_NOPE_DIM].reshape(KV_LORA_RANK, MLA_HEADS * QK_NOPE_DIM)
    vv = w[..., QK_NOPE_DIM:].reshape(KV_LORA_RANK, MLA_HEADS * V_HEAD_DIM)
    return jnp.concatenate([kn, vv], axis=1).astype(BF16)


def _pad_rows(w, n):
    return jnp.pad(w, ((0, n - w.shape[0]), (0, 0)))


def _block(x, c, positions, w_mod, b_mod, g_pre_mix, w_in, mu_shift, w0, w_decay_up, a0,
           w_iclr_up, w_gate_up, k_k, k_a, r_k, ln_x_g, ln_x_b, q_norm_g, w_q_up, kv_norm_g,
           w_kv_up, w_out, g_post_mix, g_pre_ffn, w_ffn_up, conv_w, conv_b, w_ffn_down,
           g_post_ffn, *, tm_in, tn_in, tm_mla, tq, tm_out, tm_ffn, tf):
    b, s, d = x.shape
    t = b * s
    x2 = x.reshape(t, d)

    c_pad = jnp.pad(c, ((0, 8 - b % 8 if b % 8 else 0), (0, 0)))
    mod = _mod_call(c_pad, w_mod, b_mod[None, :])[:b]
    mod3 = mod.reshape(b, N_MOD, d)

    w_in_p, mu_p = _layout_w_in(w_in, mu_shift)
    p2 = _inproj_call(x2, mod3, g_pre_mix[None, :], w_in_p, mu_p, s, tm_in, tn_in)

    vecs = jnp.stack([w0, a0, k_k, k_a, r_k.reshape(-1), ln_x_g, ln_x_b, jnp.zeros_like(w0)])
    y_rwkv = _rwkv_call(p2.reshape(b, s, IN_COLS_PAD), vecs,
                        _pad_rows(w_decay_up, LANES).astype(BF16),
                        _pad_rows(w_iclr_up, LANES).astype(BF16),
                        w_gate_up.astype(BF16))

    half = QK_ROPE_DIM // 2
    inv_freq = ROPE_THETA ** (-jnp.arange(0, QK_ROPE_DIM, 2, dtype=F32) / QK_ROPE_DIM)
    invf = jnp.concatenate([inv_freq, inv_freq, jnp.zeros((LANES - 2 * half,), F32)])[None, :]
    q4, k4, v4 = _mlaproj_call(p2, positions.reshape(t, 1), invf, q_norm_g[None, :],
                               kv_norm_g[None, :], _layout_w_q(w_q_up), _layout_w_kv(w_kv_up),
                               b, s, tm_mla)
    y_mla = _attn_call(q4, k4, v4, tq)

    x_mid = _outproj_call(y_rwkv.reshape(t, RWKV_WIDTH), y_mla.reshape(t, MLA_WIDTH), x2, mod3,
                          g_post_mix[None, :], w_out.astype(BF16), s, tm_out)
    out = _ffn_call(x_mid, mod3, g_pre_ffn[None, :], w_ffn_up.astype(BF16), conv_w,
                    conv_b[None, :], w_ffn_down.astype(BF16), g_post_ffn[None, :], s, tm_ffn, tf)
    return out.reshape(b, s, d)


def kernel(x, c, positions, w_mod, b_mod, g_pre_mix, w_in, mu_shift, w0, w_decay_up, a0, w_iclr_up, w_gate_up, k_k, k_a, r_k, ln_x_g, ln_x_b, q_norm_g, w_q_up, kv_norm_g, w_kv_up, w_out, g_post_mix, g_pre_ffn, w_ffn_up, conv_w, conv_b, w_ffn_down, g_post_ffn):
    args = (x, c, positions, w_mod, b_mod, g_pre_mix, w_in, mu_shift, w0, w_decay_up, a0,
            w_iclr_up, w_gate_up, k_k, k_a, r_k, ln_x_g, ln_x_b, q_norm_g, w_q_up, kv_norm_g,
            w_kv_up, w_out, g_post_mix, g_pre_ffn, w_ffn_up, conv_w, conv_b, w_ffn_down,
            g_post_ffn)
    x = x
    for l in range(w_mod.shape[0]):
        layer = [a[l] for a in args[3:]]
        x = _block(x, c, positions, *layer, tm_in=512, tn_in=1536, tm_mla=512, tq=256,
                   tm_out=512, tm_ffn=512, tf=512)
    return x
```

```python
import functools

import numpy as np
import jax
import jax.numpy as jnp
from jax import lax
from jax.experimental import pallas as pl
from jax.experimental.pallas import tpu as pltpu

F32 = jnp.float32
BF16 = jnp.bfloat16

D_MODEL = 2048
CHUNK = 64
RWKV_HEAD_DIM = 64
RWKV_WIDTH = 1024
RWKV_HEADS = 16
DECAY_LORA = 96
ICLR_LORA = 96
GATE_LORA = 256
GN_EPS = 64e-5
QK_NOPE_DIM = 128
QK_ROPE_DIM = 64
V_HEAD_DIM = 128
MLA_WIDTH = 1024
MLA_HEADS = 8
Q_LORA_RANK = 512
KV_LORA_RANK = 256
ROPE_THETA = 10000.0
D_FF = 5632
NORM_EPS = 1e-6
N_MOD = 6

LANES = 128
HALO = 16
GROUP = 256
HEADS_PER_GROUP = GROUP // RWKV_HEAD_DIM
N_GROUPS = RWKV_WIDTH // GROUP
QK_PAD = 256
IN_COLS_PAD = 4608
VMEM_LIMIT = 56 * 1024 * 1024


def _bdot(a, b):
    return jnp.dot(a.astype(BF16), b.astype(BF16), preferred_element_type=F32)


def _split2(x):
    hi = x.astype(BF16)
    lo = (x - hi.astype(F32)).astype(BF16)
    return hi, lo


def _sigmoid(x):
    return 1.0 / (1.0 + jnp.exp(-x))


def _rms(xv, g):
    ms = jnp.mean(xv * xv, axis=-1, keepdims=True)
    return xv * lax.rsqrt(ms + NORM_EPS) * g


def _mod_kernel(c_ref, w_ref, b_ref, o_ref):
    c = c_ref[...]
    s = c * _sigmoid(c)
    o_ref[...] = _bdot(s, w_ref[...]) + b_ref[...]


def _mod_call(c_pad, w_mod, b_mod, tn=1024):
    rows, d = c_pad.shape
    n = w_mod.shape[1]
    return pl.pallas_call(
        _mod_kernel,
        grid=(n // tn,),
        in_specs=[pl.BlockSpec((rows, d), lambda j: (0, 0)),
                  pl.BlockSpec((d, tn), lambda j: (0, j)),
                  pl.BlockSpec((1, tn), lambda j: (0, j))],
        out_specs=pl.BlockSpec((rows, tn), lambda j: (0, j)),
        out_shape=jax.ShapeDtypeStruct((rows, n), F32),
        compiler_params=pltpu.CompilerParams(dimension_semantics=("arbitrary",),
                                             vmem_limit_bytes=VMEM_LIMIT),
        name="mod",
    )(c_pad, w_mod, b_mod)


def _inproj_kernel(x_ref, xh_ref, mod_ref, g_ref, w_ref, mu_ref, o_ref, *, tiles_per_seq):
    i = pl.program_id(1)
    first = (i % tiles_per_seq) == 0
    shift = mod_ref[0, 0:1, :]
    scale = mod_ref[0, 1:2, :]
    g = g_ref[...]
    h = _rms(x_ref[...], g) * (1.0 + scale) + shift
    hh = _rms(xh_ref[...], g) * (1.0 + scale) + shift
    hh = jnp.where(first, 0.0, hh)
    hcat = jnp.concatenate([hh, h], axis=0).astype(BF16)
    p = jnp.dot(hcat, w_ref[...], preferred_element_type=F32)
    prev = pltpu.roll(p, 1, axis=0)
    out = p + (prev - p) * mu_ref[...]
    o_ref[...] = out[HALO:, :]


def _inproj_call(x2, mod3, g_pre, w_in_p, mu_p, seq, tm, tn):
    t, d = x2.shape
    n = w_in_p.shape[1]
    tps = seq // tm
    hb = tm // HALO
    return pl.pallas_call(
        functools.partial(_inproj_kernel, tiles_per_seq=tps),
        grid=(n // tn, t // tm),
        in_specs=[pl.BlockSpec((tm, d), lambda j, i: (i, 0)),
                  pl.BlockSpec((HALO, d), lambda j, i: (jnp.maximum(i * hb - 1, 0), 0)),
                  pl.BlockSpec((1, N_MOD, d), lambda j, i: (i // tps, 0, 0)),
                  pl.BlockSpec((1, d), lambda j, i: (0, 0)),
                  pl.BlockSpec((d, tn), lambda j, i: (0, j)),
                  pl.BlockSpec((1, tn), lambda j, i: (0, j))],
        out_specs=pl.BlockSpec((tm, tn), lambda j, i: (i, j)),
        out_shape=jax.ShapeDtypeStruct((t, n), F32),
        compiler_params=pltpu.CompilerParams(dimension_semantics=("arbitrary", "arbitrary"),
                                             vmem_limit_bytes=VMEM_LIMIT),
        name="inproj",
    )(x2, x2, mod3, g_pre, w_in_p, mu_p)


def _rwkv_masks(nb):
    i = np.arange(GROUP)[:, None]
    j = np.arange(GROUP)[None, :]
    n = RWKV_HEAD_DIM
    bd = (i // n) == (j // n)
    sl = bd & ((j % n) < (i % n))
    mk = np.stack([
        bd,
        sl & ((i // 8) == (j // 8)),
        bd & ((i // 16) == (j // 16)) & ((i // 8) > (j // 8)),
        bd & ((i // 32) == (j // 32)) & ((i // 16) > (j // 16)),
        bd & ((i // 32) > (j // 32)),
        i == j,
    ]).astype(np.float32)
    t = np.arange(CHUNK)[:, None]
    mc = np.stack([(j % n) < t, (j % n) <= t]).astype(np.float32)
    ti = np.arange(nb * CHUNK)[:, None]
    tj = np.arange(nb * CHUNK)[None, :]
    tri = ((ti // CHUNK == tj // CHUNK) & (tj <= ti)).astype(np.float32)
    return mk, mc, tri


def _rwkv_kernel(r_ref, k_ref, v_ref, lo_ref, vec_ref, wd_ref, wa_ref, wg_ref,
                 mk_ref, mc_ref, tri_ref, e_ref, o_ref, st_ref, *, nb):
    @pl.when(pl.program_id(1) == 0)
    def _reset_state():
        st_ref[...] = jnp.zeros_like(st_ref)

    def rows(ref):
        return jnp.concatenate([ref[bb] for bb in range(nb)], axis=0)

    r = rows(r_ref)
    k = rows(k_ref)
    v = rows(v_ref)
    lo = rows(lo_ref)
    w0 = vec_ref[0:1, :]
    a0 = vec_ref[1:2, :]
    k_k = vec_ref[2:3, :]
    k_a = vec_ref[3:4, :]
    r_k = vec_ref[4:5, :]
    ln_g = vec_ref[5:6, :]
    ln_b = vec_ref[6:7, :]

    z = -(w0 + _bdot(jnp.tanh(lo[:, 0:LANES]), wd_ref[...]))
    softplus = jnp.maximum(z, 0.0) + jnp.log(1.0 + jnp.exp(-jnp.abs(z)))
    lw = -jnp.exp(-softplus - 0.5)
    a = _sigmoid(a0 + _bdot(lo[:, LANES:2 * LANES], wa_ref[...]))
    gate = _bdot(_sigmoid(lo[:, 2 * LANES:4 * LANES]), wg_ref[...])
    kk = k * k_k
    kp = k * (1.0 + (a - 1.0) * k_a)

    tri = tri_ref[...]
    l1 = lw.astype(BF16)
    rem = lw - l1.astype(F32)
    l2 = rem.astype(BF16)
    l3 = (rem - l2.astype(F32)).astype(BF16)
    cum = (jnp.dot(tri, l1, preferred_element_type=F32)
           + jnp.dot(tri, l2, preferred_element_type=F32)
           + jnp.dot(tri, l3, preferred_element_type=F32))
    cum_last = [cum[(bb + 1) * CHUNK - 1:(bb + 1) * CHUNK, :] for bb in range(nb)]
    cum_last_rows = jnp.concatenate(
        [jnp.broadcast_to(cl, (CHUNK, cl.shape[1])) for cl in cum_last], axis=0)
    e_pos = jnp.exp(cum)
    e_pos_x = jnp.exp(cum - lw)
    e_neg = jnp.exp(-cum)
    e_end = jnp.exp(cum_last_rows - cum)

    m_bd = mk_ref[0]
    m_b8 = mk_ref[1]
    m_l16 = mk_ref[2]
    m_l32 = mk_ref[3]
    m_l64 = mk_ref[4]
    eye = mk_ref[5]
    c_strict = mc_ref[0]
    c_incl = mc_ref[1]
    ones_bd = e_ref[...]

    def head_sum(x):
        return jnp.dot(x.astype(BF16), ones_bd, preferred_element_type=F32)

    def head_sum2(x):
        hi, lo_ = _split2(x)
        return (jnp.dot(hi, ones_bd, preferred_element_type=F32)
                + jnp.dot(lo_, ones_bd, preferred_element_type=F32))

    def tile4(x):
        return jnp.concatenate([x] * HEADS_PER_GROUP, axis=0) * m_bd

    def fold4(x):
        return (x[0:CHUNK] + x[CHUNK:2 * CHUNK] + x[2 * CHUNK:3 * CHUNK] + x[3 * CHUNK:4 * CHUNK])

    def dot3(x, y):
        xh, xl = _split2(x)
        yh, yl = _split2(y)
        return (jnp.dot(xh, yh, preferred_element_type=F32)
                + jnp.dot(xh, yl, preferred_element_type=F32)
                + jnp.dot(xl, yh, preferred_element_type=F32))

    chains = [(bb, gi) for bb in range(nb) for gi in range(N_GROUPS)]
    rsl = [(slice(bb * CHUNK, (bb + 1) * CHUNK), slice(gi * GROUP, (gi + 1) * GROUP))
           for bb, gi in chains]

    def each(fn, *lists):
        return [fn(*args) for args in zip(*lists)]

    def cut(x):
        return [x[rs, sl] for rs, sl in rsl]

    r_g, v_g, kp_g, a_g, kk_g = cut(r), cut(v), cut(kp), cut(a), cut(kk)
    norm = each(lambda x: jnp.sqrt(head_sum(x * x)), kk_g)
    kkn = each(lambda x, n: x / jnp.maximum(n, 1e-12), kk_g, norm)
    ka = each(lambda x, y: x * y, kkn, a_g)
    a_bar = each(lambda x, e: -x * e, kkn, cut(e_pos_x))
    r_bar = each(lambda x, e: x * e, r_g, cut(e_pos))
    b_til = each(lambda x, e: x * e, ka, cut(e_neg))
    k_til = each(lambda x, e: x * e, kp_g, cut(e_neg))
    b_hat = each(lambda x, e: x * e, ka, cut(e_end))
    k_hat = each(lambda x, e: x * e, kp_g, cut(e_end))

    def score_fn(ab, rb, bt, kt):
        lhs = jnp.concatenate([ab, rb], axis=0)
        rhs = jnp.concatenate([tile4(bt), tile4(kt)], axis=0)
        return lax.dot_general(lhs.astype(BF16), rhs.astype(BF16),
                               (((1,), (1,)), ((), ())), preferred_element_type=F32)

    scores = each(score_fn, a_bar, r_bar, b_til, k_til)
    a_ab = each(lambda x: x[0:CHUNK, 0:GROUP] * c_strict, scores)
    a_ak = each(lambda x: x[0:CHUNK, GROUP:] * c_strict, scores)
    a_rb = each(lambda x: x[CHUNK:, 0:GROUP] * c_incl, scores)
    a_rk = each(lambda x: x[CHUNK:, GROUP:] * c_incl, scores)

    a_full = each(lambda x: jnp.concatenate([x] * HEADS_PER_GROUP, axis=0) * m_bd, a_ab)
    n1 = each(lambda x: x * m_b8, a_full)
    n2 = each(lambda x: _bdot(x, x), n1)
    inv = each(lambda x, y: eye + x + _bdot(eye + x, y), n1, n2)
    n4 = each(lambda x: _bdot(x, x), n2)
    inv = each(lambda x, y: x + _bdot(x, y), inv, n4)
    for m_off in (m_l16, m_l32, m_l64):
        tmp = each(lambda x, af: _bdot(x, af * m_off), inv, a_full)
        inv = each(lambda x, t: x + _bdot(t, x), inv, tmp)

    v_bd = each(tile4, v_g)
    akv = each(_bdot, a_ak, v_bd)
    wu = each(lambda t, ab, x: _bdot(t, jnp.concatenate([tile4(ab), tile4(x)], axis=1)),
              inv, a_bar, akv)
    w_cat = each(lambda x: fold4(x[:, 0:GROUP]), wu)
    u_cat = each(lambda x: fold4(x[:, GROUP:]), wu)
    ry = each(_bdot, a_rb, wu)
    r_hat = each(lambda x, y: x + y[:, 0:GROUP], r_bar, ry)
    y_hat = each(lambda y, ak, vb: y[:, GROUP:] + _bdot(ak, vb), ry, a_rk, v_bd)

    state = [st_ref[bb, gi] for bb, gi in chains]
    y = each(lambda rh, st, yh: _bdot(rh, st) + yh, r_hat, state, y_hat)

    def trans_fn(bh, kh, wc, uc, vg):
        lhs_t = jnp.concatenate([bh, kh], axis=0)
        rhs_t = jnp.concatenate(
            [jnp.concatenate([wc, uc], axis=1),
             jnp.concatenate([jnp.zeros_like(vg), vg], axis=1)], axis=0)
        return lax.dot_general(lhs_t.astype(BF16), rhs_t.astype(BF16),
                               (((0,), (0,)), ((), ())), preferred_element_type=F32)

    mc = each(trans_fn, b_hat, k_hat, w_cat, u_cat, v_g)
    for (bb, gi), (_, sl), mci, st in zip(chains, rsl, mc, state):
        m_t = eye * jnp.exp(cum_last[bb][:, sl]) + mci[:, 0:GROUP] * m_bd
        st_ref[bb, gi] = dot3(m_t, st) + mci[:, GROUP:] * m_bd

    inv_n = 1.0 / RWKV_HEAD_DIM
    mean = each(lambda x: head_sum2(x) * inv_n, y)
    dlt = each(lambda x, m: x - m, y, mean)
    var = each(lambda x: head_sum(x * x) * inv_n, dlt)
    bonus = each(lambda rg, kg, vg, rsl_: head_sum(rg * kg * r_k[:, rsl_[1]]) * vg,
                 r_g, kp_g, v_g, rsl)
    for (bb, gi), (rs, sl), d, vr, bo in zip(chains, rsl, dlt, var, bonus):
        yn = d * lax.rsqrt(vr + GN_EPS) * ln_g[:, sl] + ln_b[:, sl]
        o_ref[bb, :, sl] = ((yn + bo) * gate[rs, sl]).astype(o_ref.dtype)


def _rwkv_call(p3, vecs, wd, wa, wg, nb):
    b, s, _ = p3.shape
    w = RWKV_WIDTH
    mk, mc, tri = _rwkv_masks(nb)
    mk = jnp.asarray(mk)
    mc = jnp.asarray(mc)
    tri = jnp.asarray(tri, dtype=BF16)
    ones_bd = mk[0].astype(BF16)
    const2 = lambda bi, ci: (0, 0)
    const3 = lambda bi, ci: (0, 0, 0)
    return pl.pallas_call(
        functools.partial(_rwkv_kernel, nb=nb),
        grid=(b // nb, s // CHUNK),
        in_specs=[pl.BlockSpec((nb, CHUNK, w), lambda bi, ci: (bi, ci, 0)),
                  pl.BlockSpec((nb, CHUNK, w), lambda bi, ci: (bi, ci, 1)),
                  pl.BlockSpec((nb, CHUNK, w), lambda bi, ci: (bi, ci, 2)),
                  pl.BlockSpec((nb, CHUNK, 4 * LANES), lambda bi, ci: (bi, ci, 3 * w // (4 * LANES))),
                  pl.BlockSpec(vecs.shape, const2),
                  pl.BlockSpec(wd.shape, const2),
                  pl.BlockSpec(wa.shape, const2),
                  pl.BlockSpec(wg.shape, const2),
                  pl.BlockSpec(mk.shape, const3),
                  pl.BlockSpec(mc.shape, const3),
                  pl.BlockSpec(tri.shape, const2),
                  pl.BlockSpec(ones_bd.shape, const2)],
        out_specs=pl.BlockSpec((nb, CHUNK, w), lambda bi, ci: (bi, ci, 0)),
        out_shape=jax.ShapeDtypeStruct((b, s, w), BF16),
        scratch_shapes=[pltpu.VMEM((nb, N_GROUPS, GROUP, GROUP), F32)],
        compiler_params=pltpu.CompilerParams(dimension_semantics=("arbitrary", "arbitrary"),
                                             vmem_limit_bytes=VMEM_LIMIT),
        name="rwkv",
    )(p3, p3, p3, p3, vecs, wd, wa, wg, mk, mc, tri, ones_bd)


def _mlaproj_kernel(pq_ref, pkv_ref, pos_ref, invf_ref, gq_ref, gkv_ref, wq_ref, wkv_ref,
                    q_ref, k_ref, v_ref):
    q = _bdot(_rms(pq_ref[...], gq_ref[...]), wq_ref[...])
    pkv = pkv_ref[...]
    kv = _bdot(_rms(pkv[:, 0:KV_LORA_RANK], gkv_ref[...]), wkv_ref[...])
    ang = pos_ref[...].astype(F32) * invf_ref[...]
    lane = lax.broadcasted_iota(jnp.int32, ang.shape, 1)
    half = QK_ROPE_DIM // 2
    cos_f = jnp.where(lane < QK_ROPE_DIM, jnp.cos(ang), 0.0)
    sin = jnp.sin(ang)
    sin_f = jnp.where(lane < half, -sin, jnp.where(lane < QK_ROPE_DIM, sin, 0.0))
    k_rot = pkv[:, KV_LORA_RANK:KV_LORA_RANK + LANES] * cos_f + pkv[:, KV_LORA_RANK + LANES:] * sin_f
    scale = (QK_NOPE_DIM + QK_ROPE_DIM) ** -0.5 * 1.4426950408889634
    swap0 = MLA_HEADS * QK_PAD
    for h in range(MLA_HEADS):
        q_nope = q[:, h * QK_PAD:h * QK_PAD + LANES]
        q_rot = (q[:, h * QK_PAD + LANES:(h + 1) * QK_PAD] * cos_f
                 + q[:, swap0 + h * LANES:swap0 + (h + 1) * LANES] * sin_f)
        q_ref[0, h] = (jnp.concatenate([q_nope, q_rot], axis=1) * scale).astype(q_ref.dtype)
        k_ref[0, h] = jnp.concatenate([kv[:, h * LANES:(h + 1) * LANES], k_rot], axis=1).astype(k_ref.dtype)
        v_ref[0, h] = kv[:, (MLA_HEADS + h) * LANES:(MLA_HEADS + h + 1) * LANES].astype(v_ref.dtype)


def _mlaproj_call(p2, pos2, invf, gq, gkv, wq, wkv, batch, seq, tm):
    t = p2.shape[0]
    tps = seq // tm
    blk = 4 * LANES
    q_blk = (IN_COLS_PAD - 2 * blk) // blk
    hd = MLA_HEADS
    out_map = lambda i: (i // tps, 0, i % tps, 0)
    const = lambda i: (0, 0)
    return pl.pallas_call(
        _mlaproj_kernel,
        grid=(t // tm,),
        in_specs=[pl.BlockSpec((tm, blk), lambda i: (i, q_blk)),
                  pl.BlockSpec((tm, blk), lambda i: (i, q_blk + 1)),
                  pl.BlockSpec((tm, 1), lambda i: (i, 0)),
                  pl.BlockSpec(invf.shape, const),
                  pl.BlockSpec(gq.shape, const),
                  pl.BlockSpec(gkv.shape, const),
                  pl.BlockSpec(wq.shape, const),
                  pl.BlockSpec(wkv.shape, const)],
        out_specs=[pl.BlockSpec((1, hd, tm, QK_PAD), out_map),
                   pl.BlockSpec((1, hd, tm, QK_PAD), out_map),
                   pl.BlockSpec((1, hd, tm, V_HEAD_DIM), out_map)],
        out_shape=[jax.ShapeDtypeStruct((batch, hd, seq, QK_PAD), BF16),
                   jax.ShapeDtypeStruct((batch, hd, seq, QK_PAD), BF16),
                   jax.ShapeDtypeStruct((batch, hd, seq, V_HEAD_DIM), BF16)],
        compiler_params=pltpu.CompilerParams(dimension_semantics=("arbitrary",),
                                             vmem_limit_bytes=VMEM_LIMIT),
        name="mlaproj",
    )(p2, p2, pos2, invf, gq, gkv, wq, wkv)


ATTN_HEADS_PER_STEP = 2


def _attn_kernel(q_ref, k_ref, v_ref, o_ref, *, tq):
    i = pl.program_id(2)
    ones_col = (lax.broadcasted_iota(jnp.int32, (tq, LANES), 1) == 0).astype(BF16)

    def step(carry, start, on_diagonal):
        new = []
        for hh, (m, acc) in enumerate(carry):
            kj = k_ref[0, hh, pl.ds(start, tq), :]
            s = lax.dot_general(q_ref[0, hh], kj, (((1,), (1,)), ((), ())),
                                preferred_element_type=F32)
            if on_diagonal:
                row = lax.broadcasted_iota(jnp.int32, (tq, tq), 0) // CHUNK
                col = lax.broadcasted_iota(jnp.int32, (tq, tq), 1) // CHUNK
                s = jnp.where(row >= col, s, -jnp.inf)
            m_new = jnp.maximum(m, jnp.max(s, axis=-1, keepdims=True))
            alpha = jnp.exp2(m - m_new)
            p = jnp.exp2(s - m_new).astype(BF16)
            vj = jnp.concatenate([v_ref[0, hh, pl.ds(start, tq), :], ones_col], axis=1)
            new.append((m_new, alpha * acc + jnp.dot(p, vj, preferred_element_type=F32)))
        return tuple(new)

    init = tuple((jnp.full((tq, 1), -jnp.inf, F32), jnp.zeros((tq, 2 * V_HEAD_DIM), F32))
                 for _ in range(ATTN_HEADS_PER_STEP))
    carry = lax.fori_loop(0, i, lambda j, c: step(c, pl.multiple_of(j * tq, tq), False), init)
    final = step(carry, pl.multiple_of(i * tq, tq), True)
    for hh, (_, acc) in enumerate(final):
        o_ref[0, :, hh * V_HEAD_DIM:(hh + 1) * V_HEAD_DIM] = (
            acc[:, 0:V_HEAD_DIM] / acc[:, V_HEAD_DIM:V_HEAD_DIM + 1]).astype(o_ref.dtype)


def _attn_call(q4, k4, v4, tq):
    b, h, s, _ = q4.shape
    hp = ATTN_HEADS_PER_STEP
    return pl.pallas_call(
        functools.partial(_attn_kernel, tq=tq),
        grid=(b, h // hp, s // tq),
        in_specs=[pl.BlockSpec((1, hp, tq, QK_PAD), lambda bi, hi, i: (bi, hi, i, 0)),
                  pl.BlockSpec((1, hp, s, QK_PAD), lambda bi, hi, i: (bi, hi, 0, 0)),
                  pl.BlockSpec((1, hp, s, V_HEAD_DIM), lambda bi, hi, i: (bi, hi, 0, 0))],
        out_specs=pl.BlockSpec((1, tq, hp * V_HEAD_DIM), lambda bi, hi, i: (bi, i, hi)),
        out_shape=jax.ShapeDtypeStruct((b, s, h * V_HEAD_DIM), BF16),
        compiler_params=pltpu.CompilerParams(
            dimension_semantics=("arbitrary", "arbitrary", "arbitrary"),
            vmem_limit_bytes=VMEM_LIMIT),
        name="attn",
    )(q4, k4, v4)


def _outproj_kernel(yr_ref, ym_ref, x_ref, mod_ref, g_ref, w_ref, o_ref):
    half = RWKV_WIDTH
    o = (jnp.dot(yr_ref[...], w_ref[0:half, :], preferred_element_type=F32)
         + jnp.dot(ym_ref[...], w_ref[half:, :], preferred_element_type=F32))
    gate = mod_ref[0, 2:3, :]
    o_ref[...] = x_ref[...] + gate * _rms(o, g_ref[...])


def _outproj_call(yr, ym, x2, mod3, g_post, w_out, seq, tm):
    t, d = x2.shape
    tps = seq // tm
    return pl.pallas_call(
        _outproj_kernel,
        grid=(t // tm,),
        in_specs=[pl.BlockSpec((tm, RWKV_WIDTH), lambda i: (i, 0)),
                  pl.BlockSpec((tm, MLA_WIDTH), lambda i: (i, 0)),
                  pl.BlockSpec((tm, d), lambda i: (i, 0)),
                  pl.BlockSpec((1, N_MOD, d), lambda i: (i // tps, 0, 0)),
                  pl.BlockSpec((1, d), lambda i: (0, 0)),
                  pl.BlockSpec(w_out.shape, lambda i: (0, 0))],
        out_specs=pl.BlockSpec((tm, d), lambda i: (i, 0)),
        out_shape=jax.ShapeDtypeStruct((t, d), F32),
        compiler_params=pltpu.CompilerParams(dimension_semantics=("arbitrary",),
                                             vmem_limit_bytes=VMEM_LIMIT),
        name="outproj",
    )(yr, ym, x2, mod3, g_post, w_out)


def _gelu_tanh(x):
    c = 0.7978845608028654
    return 0.5 * x * (1.0 + jnp.tanh(c * (x + 0.044715 * (x * x * x))))


def _ffn_kernel(x_ref, xh_ref, mod_ref, gpre_ref, wg_ref, wv_ref, cwg_ref, cwv_ref,
                cbg_ref, cbv_ref, wd_ref, gpost_ref, o_ref, h_ref, acc_ref, *, tiles_per_seq):
    i = pl.program_id(0)
    j = pl.program_id(1)

    @pl.when(j == 0)
    def _prologue():
        shift = mod_ref[0, 3:4, :]
        scale = mod_ref[0, 4:5, :]
        g = gpre_ref[...]
        hh = _rms(xh_ref[...], g) * (1.0 + scale) + shift
        hh = jnp.where((i % tiles_per_seq) == 0, 0.0, hh)
        h_ref[0:HALO, :] = hh.astype(BF16)
        h_ref[HALO:, :] = (_rms(x_ref[...], g) * (1.0 + scale) + shift).astype(BF16)
        acc_ref[...] = jnp.zeros_like(acc_ref)

    hb = h_ref[...]

    def conv(u, cw, cb):
        u1 = pltpu.roll(u, 1, axis=0)
        u2 = pltpu.roll(u, 2, axis=0)
        return (cb + cw[2:3, :] * u + cw[1:2, :] * u1 + cw[0:1, :] * u2)[HALO:, :]

    gc = conv(jnp.dot(hb, wg_ref[...], preferred_element_type=F32), cwg_ref[...], cbg_ref[...])
    vc = conv(jnp.dot(hb, wv_ref[...], preferred_element_type=F32), cwv_ref[...], cbv_ref[...])
    act = (_gelu_tanh(gc) * vc).astype(BF16)
    acc_ref[...] += jnp.dot(act, wd_ref[...], preferred_element_type=F32)

    @pl.when(j == pl.num_programs(1) - 1)
    def _epilogue():
        gate = mod_ref[0, 5:6, :]
        o_ref[...] = x_ref[...] + gate * _rms(acc_ref[...], gpost_ref[...])


def _ffn_call(x2, mod3, g_pre, w_up, conv_w, conv_b, w_down, g_post, seq, tm, tf):
    t, d = x2.shape
    tps = seq // tm
    hb = tm // HALO
    nf = D_FF // tf
    return pl.pallas_call(
        functools.partial(_ffn_kernel, tiles_per_seq=tps),
        grid=(t // tm, nf),
        in_specs=[pl.BlockSpec((tm, d), lambda i, j: (i, 0)),
                  pl.BlockSpec((HALO, d), lambda i, j: (jnp.maximum(i * hb - 1, 0), 0)),
                  pl.BlockSpec((1, N_MOD, d), lambda i, j: (i // tps, 0, 0)),
                  pl.BlockSpec((1, d), lambda i, j: (0, 0)),
                  pl.BlockSpec((d, tf), lambda i, j: (0, j)),
                  pl.BlockSpec((d, tf), lambda i, j: (0, nf + j)),
                  pl.BlockSpec((3, tf), lambda i, j: (0, j)),
                  pl.BlockSpec((3, tf), lambda i, j: (0, nf + j)),
                  pl.BlockSpec((1, tf), lambda i, j: (0, j)),
                  pl.BlockSpec((1, tf), lambda i, j: (0, nf + j)),
                  pl.BlockSpec((tf, d), lambda i, j: (j, 0)),
                  pl.BlockSpec((1, d), lambda i, j: (0, 0))],
        out_specs=pl.BlockSpec((tm, d), lambda i, j: (i, 0)),
        out_shape=jax.ShapeDtypeStruct((t, d), F32),
        scratch_shapes=[pltpu.VMEM((tm + HALO, d), BF16), pltpu.VMEM((tm, d), F32)],
        compiler_params=pltpu.CompilerParams(dimension_semantics=("arbitrary", "arbitrary"),
                                             vmem_limit_bytes=VMEM_LIMIT),
        name="ffn",
    )(x2, x2, mod3, g_pre, w_up, w_up, conv_w, conv_w, conv_b, conv_b, w_down, g_post)


def _pad_cols(w, n):
    return jnp.pad(w, ((0, 0), (0, n - w.shape[1])))


def _layout_w_in(w_in, mu_shift):
    w3 = 3 * RWKV_WIDTH
    o_wd = w3
    o_ad = o_wd + DECAY_LORA
    o_gd = o_ad + ICLR_LORA
    o_q = o_gd + GATE_LORA
    o_kv = o_q + Q_LORA_RANK
    o_kr = o_kv + KV_LORA_RANK
    half = QK_ROPE_DIM // 2

    def lay(m):
        kr = m[:, o_kr:o_kr + QK_ROPE_DIM]
        return jnp.concatenate([
            m[:, 0:w3],
            _pad_cols(m[:, o_wd:o_ad], LANES),
            _pad_cols(m[:, o_ad:o_gd], LANES),
            m[:, o_gd:o_q],
            m[:, o_q:o_kv],
            m[:, o_kv:o_kr],
            _pad_cols(kr, LANES),
            _pad_cols(jnp.concatenate([kr[:, half:], kr[:, :half]], axis=1), LANES),
        ], axis=1)

    mu_full = jnp.concatenate([mu_shift, jnp.zeros((w_in.shape[1] - mu_shift.shape[0],), F32)])
    w_p = lay(w_in).astype(BF16)
    mu_p = lay(mu_full[None, :])
    return w_p, mu_p


def _layout_w_q(w_q_up):
    dn, dr = QK_NOPE_DIM, QK_ROPE_DIM
    half = dr // 2
    w = w_q_up.reshape(Q_LORA_RANK, MLA_HEADS, dn + dr)
    nope, u1, u2 = w[..., :dn], w[..., dn:dn + half], w[..., dn + half:]
    z = jnp.zeros(u1.shape[:-1] + (LANES - dr,), w.dtype)
    main = jnp.concatenate([nope, u1, u2, z], axis=-1).reshape(Q_LORA_RANK, MLA_HEADS * QK_PAD)
    swap = jnp.concatenate([u2, u1, z], axis=-1).reshape(Q_LORA_RANK, MLA_HEADS * LANES)
    return jnp.concatenate([main, swap], axis=1).astype(BF16)


def _layout_w_kv(w_kv_up):
    w = w_kv_up.reshape(KV_LORA_RANK, MLA_HEADS, QK_NOPE_DIM + V_HEAD_DIM)
    kn = w[..., :QK_NOPE_DIM].reshape(KV_LORA_RANK, MLA_HEADS * QK_NOPE_DIM)
    vv = w[..., QK_NOPE_DIM:].reshape(KV_LORA_RANK, MLA_HEADS * V_HEAD_DIM)
    return jnp.concatenate([kn, vv], axis=1).astype(BF16)


def _pad_rows(w, n):
    return jnp.pad(w, ((0, n - w.shape[0]), (0, 0)))


def _block(x, c, positions, w_mod, b_mod, g_pre_mix, w_in, mu_shift, w0, w_decay_up, a0,
           w_iclr_up, w_gate_up, k_k, k_a, r_k, ln_x_g, ln_x_b, q_norm_g, w_q_up, kv_norm_g,
           w_kv_up, w_out, g_post_mix, g_pre_ffn, w_ffn_up, conv_w, conv_b, w_ffn_down,
           g_post_ffn, *, tm_in, tn_in, nb_rwkv, tm_mla, tq, tm_out, tm_ffn, tf):
    b, s, d = x.shape
    t = b * s
    x2 = x.reshape(t, d)

    c_pad = jnp.pad(c, ((0, 8 - b % 8 if b % 8 else 0), (0, 0)))
    mod = _mod_call(c_pad, w_mod, b_mod[None, :])[:b]
    mod3 = mod.reshape(b, N_MOD, d)

    w_in_p, mu_p = _layout_w_in(w_in, mu_shift)
    p2 = _inproj_call(x2, mod3, g_pre_mix[None, :], w_in_p, mu_p, s, tm_in, tn_in)

    vecs = jnp.stack([w0, a0, k_k, k_a, r_k.reshape(-1), ln_x_g, ln_x_b, jnp.zeros_like(w0)])
    y_rwkv = _rwkv_call(p2.reshape(b, s, IN_COLS_PAD), vecs,
                        _pad_rows(w_decay_up, LANES).astype(BF16),
                        _pad_rows(w_iclr_up, LANES).astype(BF16),
                        w_gate_up.astype(BF16), nb_rwkv)

    half = QK_ROPE_DIM // 2
    inv_freq = ROPE_THETA ** (-jnp.arange(0, QK_ROPE_DIM, 2, dtype=F32) / QK_ROPE_DIM)
    invf = jnp.concatenate([inv_freq, inv_freq, jnp.zeros((LANES - 2 * half,), F32)])[None, :]
    q4, k4, v4 = _mlaproj_call(p2, positions.reshape(t, 1), invf, q_norm_g[None, :],
                               kv_norm_g[None, :], _layout_w_q(w_q_up), _layout_w_kv(w_kv_up),
                               b, s, tm_mla)
    y_mla = _attn_call(q4, k4, v4, tq)

    x_mid = _outproj_call(y_rwkv.reshape(t, RWKV_WIDTH), y_mla.reshape(t, MLA_WIDTH), x2, mod3,
                          g_post_mix[None, :], w_out.astype(BF16), s, tm_out)
    out = _ffn_call(x_mid, mod3, g_pre_ffn[None, :], w_ffn_up.astype(BF16), conv_w,
                    conv_b[None, :], w_ffn_down.astype(BF16), g_post_ffn[None, :], s, tm_ffn, tf)
    return out.reshape(b, s, d)


def kernel(x, c, positions, w_mod, b_mod, g_pre_mix, w_in, mu_shift, w0, w_decay_up, a0, w_iclr_up, w_gate_up, k_k, k_a, r_k, ln_x_g, ln_x_b, q_norm_g, w_q_up, kv_norm_g, w_kv_up, w_out, g_post_mix, g_pre_ffn, w_ffn_up, conv_w, conv_b, w_ffn_down, g_post_ffn):
    args = (x, c, positions, w_mod, b_mod, g_pre_mix, w_in, mu_shift, w0, w_decay_up, a0,
            w_iclr_up, w_gate_up, k_k, k_a, r_k, ln_x_g, ln_x_b, q_norm_g, w_q_up, kv_norm_g,
            w_kv_up, w_out, g_post_mix, g_pre_ffn, w_ffn_up, conv_w, conv_b, w_ffn_down,
            g_post_ffn)
    for l in range(w_mod.shape[0]):
        layer = [a[l] for a in args[3:]]
        x = _block(x, c, positions, *layer, tm_in=512, tn_in=1536, nb_rwkv=2, tm_mla=512, tq=512,
                   tm_out=512, tm_ffn=512, tf=512)
    return x
```

```python
import functools

import numpy as np
import jax
import jax.numpy as jnp
from jax import lax
from jax.experimental import pallas as pl
from jax.experimental.pallas import tpu as pltpu

F32 = jnp.float32
BF16 = jnp.bfloat16

D_MODEL = 2048
CHUNK = 64
RWKV_HEAD_DIM = 64
RWKV_WIDTH = 1024
RWKV_HEADS = 16
DECAY_LORA = 96
ICLR_LORA = 96
GATE_LORA = 256
GN_EPS = 64e-5
QK_NOPE_DIM = 128
QK_ROPE_DIM = 64
V_HEAD_DIM = 128
MLA_WIDTH = 1024
MLA_HEADS = 8
Q_LORA_RANK = 512
KV_LORA_RANK = 256
ROPE_THETA = 10000.0
D_FF = 5632
NORM_EPS = 1e-6
N_MOD = 6

LANES = 128
HALO = 16
GROUP = 256
HEADS_PER_GROUP = GROUP // RWKV_HEAD_DIM
N_GROUPS = RWKV_WIDTH // GROUP
QK_PAD = 256
IN_COLS_PAD = 4608
VMEM_LIMIT = 56 * 1024 * 1024


def _bdot(a, b):
    return jnp.dot(a.astype(BF16), b.astype(BF16), preferred_element_type=F32)


def _split2(x):
    hi = x.astype(BF16)
    lo = (x - hi.astype(F32)).astype(BF16)
    return hi, lo


def _sigmoid(x):
    return 1.0 / (1.0 + jnp.exp(-x))


def _rms(xv, g):
    ms = jnp.mean(xv * xv, axis=-1, keepdims=True)
    return xv * lax.rsqrt(ms + NORM_EPS) * g


def _mod_kernel(c_ref, w_ref, b_ref, o_ref):
    c = c_ref[...]
    s = c * _sigmoid(c)
    o_ref[...] = _bdot(s, w_ref[...]) + b_ref[...]


def _mod_call(c_pad, w_mod, b_mod, tn=1024):
    rows, d = c_pad.shape
    n = w_mod.shape[1]
    return pl.pallas_call(
        _mod_kernel,
        grid=(n // tn,),
        in_specs=[pl.BlockSpec((rows, d), lambda j: (0, 0)),
                  pl.BlockSpec((d, tn), lambda j: (0, j)),
                  pl.BlockSpec((1, tn), lambda j: (0, j))],
        out_specs=pl.BlockSpec((rows, tn), lambda j: (0, j)),
        out_shape=jax.ShapeDtypeStruct((rows, n), F32),
        compiler_params=pltpu.CompilerParams(dimension_semantics=("arbitrary",),
                                             vmem_limit_bytes=VMEM_LIMIT),
        name="mod",
    )(c_pad, w_mod, b_mod)


def _inproj_kernel(x_ref, xh_ref, mod_ref, g_ref, w_ref, mu_ref, o_ref, *, tiles_per_seq):
    i = pl.program_id(1)
    first = (i % tiles_per_seq) == 0
    shift = mod_ref[0, 0:1, :]
    scale = mod_ref[0, 1:2, :]
    g = g_ref[...]
    h = _rms(x_ref[...], g) * (1.0 + scale) + shift
    hh = _rms(xh_ref[...], g) * (1.0 + scale) + shift
    hh = jnp.where(first, 0.0, hh)
    hcat = jnp.concatenate([hh, h], axis=0).astype(BF16)
    p = jnp.dot(hcat, w_ref[...], preferred_element_type=F32)
    prev = pltpu.roll(p, 1, axis=0)
    out = p + (prev - p) * mu_ref[...]
    o_ref[...] = out[HALO:, :]


def _inproj_call(x2, mod3, g_pre, w_in_p, mu_p, seq, tm, tn):
    t, d = x2.shape
    n = w_in_p.shape[1]
    tps = seq // tm
    hb = tm // HALO
    return pl.pallas_call(
        functools.partial(_inproj_kernel, tiles_per_seq=tps),
        grid=(n // tn, t // tm),
        in_specs=[pl.BlockSpec((tm, d), lambda j, i: (i, 0)),
                  pl.BlockSpec((HALO, d), lambda j, i: (jnp.maximum(i * hb - 1, 0), 0)),
                  pl.BlockSpec((1, N_MOD, d), lambda j, i: (i // tps, 0, 0)),
                  pl.BlockSpec((1, d), lambda j, i: (0, 0)),
                  pl.BlockSpec((d, tn), lambda j, i: (0, j)),
                  pl.BlockSpec((1, tn), lambda j, i: (0, j))],
        out_specs=pl.BlockSpec((tm, tn), lambda j, i: (i, j)),
        out_shape=jax.ShapeDtypeStruct((t, n), F32),
        compiler_params=pltpu.CompilerParams(dimension_semantics=("arbitrary", "arbitrary"),
                                             vmem_limit_bytes=VMEM_LIMIT),
        name="inproj",
    )(x2, x2, mod3, g_pre, w_in_p, mu_p)


def _rwkv_masks(nb):
    i = np.arange(GROUP)[:, None]
    j = np.arange(GROUP)[None, :]
    n = RWKV_HEAD_DIM
    bd = (i // n) == (j // n)
    sl = bd & ((j % n) < (i % n))
    mk = np.stack([
        bd,
        sl & ((i // 8) == (j // 8)),
        bd & ((i // 16) == (j // 16)) & ((i // 8) > (j // 8)),
        bd & ((i // 32) == (j // 32)) & ((i // 16) > (j // 16)),
        bd & ((i // 32) > (j // 32)),
        i == j,
    ]).astype(np.float32)
    t = np.arange(CHUNK)[:, None]
    mc = np.stack([(j % n) < t, (j % n) <= t]).astype(np.float32)
    ti = np.arange(nb * CHUNK)[:, None]
    tj = np.arange(nb * CHUNK)[None, :]
    tri = ((ti // CHUNK == tj // CHUNK) & (tj <= ti)).astype(np.float32)
    return mk, mc, tri


def _rwkv_kernel(r_ref, k_ref, v_ref, lo_ref, vec_ref, wd_ref, wa_ref, wg_ref,
                 mk_ref, mc_ref, tri_ref, e_ref, o_ref, st_ref, *, nb):
    @pl.when(pl.program_id(1) == 0)
    def _reset_state():
        st_ref[...] = jnp.zeros_like(st_ref)

    def rows(ref):
        return jnp.concatenate([ref[bb] for bb in range(nb)], axis=0)

    r = rows(r_ref)
    k = rows(k_ref)
    v = rows(v_ref)
    lo = rows(lo_ref)
    w0 = vec_ref[0:1, :]
    a0 = vec_ref[1:2, :]
    k_k = vec_ref[2:3, :]
    k_a = vec_ref[3:4, :]
    r_k = vec_ref[4:5, :]
    ln_g = vec_ref[5:6, :]
    ln_b = vec_ref[6:7, :]

    z = -(w0 + _bdot(jnp.tanh(lo[:, 0:LANES]), wd_ref[...]))
    softplus = jnp.maximum(z, 0.0) + jnp.log(1.0 + jnp.exp(-jnp.abs(z)))
    lw = -jnp.exp(-softplus - 0.5)
    a = _sigmoid(a0 + _bdot(lo[:, LANES:2 * LANES], wa_ref[...]))
    gate = _bdot(_sigmoid(lo[:, 2 * LANES:4 * LANES]), wg_ref[...])
    kk = k * k_k
    kp = k * (1.0 + (a - 1.0) * k_a)

    tri = tri_ref[...]
    l1 = lw.astype(BF16)
    rem = lw - l1.astype(F32)
    l2 = rem.astype(BF16)
    l3 = (rem - l2.astype(F32)).astype(BF16)
    cum = (jnp.dot(tri, l1, preferred_element_type=F32)
           + jnp.dot(tri, l2, preferred_element_type=F32)
           + jnp.dot(tri, l3, preferred_element_type=F32))
    cum_last = [cum[(bb + 1) * CHUNK - 1:(bb + 1) * CHUNK, :] for bb in range(nb)]
    cum_last_rows = jnp.concatenate(
        [jnp.broadcast_to(cl, (CHUNK, cl.shape[1])) for cl in cum_last], axis=0)
    e_pos = jnp.exp(cum)
    e_pos_x = jnp.exp(cum - lw)
    e_neg = jnp.exp(-cum)
    e_end = jnp.exp(cum_last_rows - cum)

    m_bd = mk_ref[0]
    m_b8 = mk_ref[1]
    m_l16 = mk_ref[2]
    m_l32 = mk_ref[3]
    m_l64 = mk_ref[4]
    eye = mk_ref[5]
    c_strict = mc_ref[0]
    c_incl = mc_ref[1]
    ones_bd = e_ref[...]

    def head_sum(x):
        return jnp.dot(x.astype(BF16), ones_bd, preferred_element_type=F32)

    def head_sum2(x):
        hi, lo_ = _split2(x)
        return (jnp.dot(hi, ones_bd, preferred_element_type=F32)
                + jnp.dot(lo_, ones_bd, preferred_element_type=F32))

    def tile4(x):
        return jnp.concatenate([x] * HEADS_PER_GROUP, axis=0) * m_bd

    def fold4(x):
        return (x[0:CHUNK] + x[CHUNK:2 * CHUNK] + x[2 * CHUNK:3 * CHUNK] + x[3 * CHUNK:4 * CHUNK])

    def dot3(x, y):
        xh, xl = _split2(x)
        yh, yl = _split2(y)
        return (jnp.dot(xh, yh, preferred_element_type=F32)
                + jnp.dot(xh, yl, preferred_element_type=F32)
                + jnp.dot(xl, yh, preferred_element_type=F32))

    chains = [(bb, gi) for bb in range(nb) for gi in range(N_GROUPS)]
    rsl = [(slice(bb * CHUNK, (bb + 1) * CHUNK), slice(gi * GROUP, (gi + 1) * GROUP))
           for bb, gi in chains]

    def each(fn, *lists):
        return [fn(*args) for args in zip(*lists)]

    def cut(x):
        return [x[rs, sl] for rs, sl in rsl]

    r_g, v_g, kp_g, a_g, kk_g = cut(r), cut(v), cut(kp), cut(a), cut(kk)
    norm = each(lambda x: jnp.sqrt(head_sum(x * x)), kk_g)
    kkn = each(lambda x, n: x / jnp.maximum(n, 1e-12), kk_g, norm)
    ka = each(lambda x, y: x * y, kkn, a_g)
    a_bar = each(lambda x, e: -x * e, kkn, cut(e_pos_x))
    r_bar = each(lambda x, e: x * e, r_g, cut(e_pos))
    b_til = each(lambda x, e: x * e, ka, cut(e_neg))
    k_til = each(lambda x, e: x * e, kp_g, cut(e_neg))
    b_hat = each(lambda x, e: x * e, ka, cut(e_end))
    k_hat = each(lambda x, e: x * e, kp_g, cut(e_end))

    def score_fn(ab, rb, bt, kt):
        lhs = jnp.concatenate([ab, rb], axis=0)
        rhs = jnp.concatenate([tile4(bt), tile4(kt)], axis=0)
        return lax.dot_general(lhs.astype(BF16), rhs.astype(BF16),
                               (((1,), (1,)), ((), ())), preferred_element_type=F32)

    scores = each(score_fn, a_bar, r_bar, b_til, k_til)
    a_ab = each(lambda x: x[0:CHUNK, 0:GROUP] * c_strict, scores)
    a_ak = each(lambda x: x[0:CHUNK, GROUP:] * c_strict, scores)
    a_rb = each(lambda x: x[CHUNK:, 0:GROUP] * c_incl, scores)
    a_rk = each(lambda x: x[CHUNK:, GROUP:] * c_incl, scores)

    a_full = each(lambda x: jnp.concatenate([x] * HEADS_PER_GROUP, axis=0) * m_bd, a_ab)
    n1 = each(lambda x: x * m_b8, a_full)
    n2 = each(lambda x: _bdot(x, x), n1)
    inv = each(lambda x, y: eye + x + _bdot(eye + x, y), n1, n2)
    n4 = each(lambda x: _bdot(x, x), n2)
    inv = each(lambda x, y: x + _bdot(x, y), inv, n4)
    for m_off in (m_l16, m_l32, m_l64):
        tmp = each(lambda x, af: _bdot(x, af * m_off), inv, a_full)
        inv = each(lambda x, t: x + _bdot(t, x), inv, tmp)

    v_bd = each(tile4, v_g)
    akv = each(_bdot, a_ak, v_bd)
    wu = each(lambda t, ab, x: _bdot(t, jnp.concatenate([tile4(ab), tile4(x)], axis=1)),
              inv, a_bar, akv)
    w_cat = each(lambda x: fold4(x[:, 0:GROUP]), wu)
    u_cat = each(lambda x: fold4(x[:, GROUP:]), wu)
    ry = each(_bdot, a_rb, wu)
    r_hat = each(lambda x, y: x + y[:, 0:GROUP], r_bar, ry)
    y_hat = each(lambda y, ak, vb: y[:, GROUP:] + _bdot(ak, vb), ry, a_rk, v_bd)

    state = [st_ref[bb, gi] for bb, gi in chains]
    y = each(lambda rh, st, yh: _bdot(rh, st) + yh, r_hat, state, y_hat)

    def trans_fn(bh, kh, wc, uc, vg):
        lhs_t = jnp.concatenate([bh, kh], axis=0)
        rhs_t = jnp.concatenate(
            [jnp.concatenate([wc, uc], axis=1),
             jnp.concatenate([jnp.zeros_like(vg), vg], axis=1)], axis=0)
        return lax.dot_general(lhs_t.astype(BF16), rhs_t.astype(BF16),
                               (((0,), (0,)), ((), ())), preferred_element_type=F32)

    mc = each(trans_fn, b_hat, k_hat, w_cat, u_cat, v_g)
    for (bb, gi), (_, sl), mci, st in zip(chains, rsl, mc, state):
        m_t = eye * jnp.exp(cum_last[bb][:, sl]) + mci[:, 0:GROUP] * m_bd
        st_ref[bb, gi] = dot3(m_t, st) + mci[:, GROUP:] * m_bd

    inv_n = 1.0 / RWKV_HEAD_DIM
    mean = each(lambda x: head_sum2(x) * inv_n, y)
    dlt = each(lambda x, m: x - m, y, mean)
    var = each(lambda x: head_sum(x * x) * inv_n, dlt)
    bonus = each(lambda rg, kg, vg, rsl_: head_sum(rg * kg * r_k[:, rsl_[1]]) * vg,
                 r_g, kp_g, v_g, rsl)
    for (bb, gi), (rs, sl), d, vr, bo in zip(chains, rsl, dlt, var, bonus):
        yn = d * lax.rsqrt(vr + GN_EPS) * ln_g[:, sl] + ln_b[:, sl]
        o_ref[bb, :, sl] = ((yn + bo) * gate[rs, sl]).astype(o_ref.dtype)


def _rwkv_call(p3, vecs, wd, wa, wg, nb):
    b, s, _ = p3.shape
    w = RWKV_WIDTH
    mk, mc, tri = _rwkv_masks(nb)
    mk = jnp.asarray(mk)
    mc = jnp.asarray(mc)
    tri = jnp.asarray(tri, dtype=BF16)
    ones_bd = mk[0].astype(BF16)
    const2 = lambda bi, ci: (0, 0)
    const3 = lambda bi, ci: (0, 0, 0)
    return pl.pallas_call(
        functools.partial(_rwkv_kernel, nb=nb),
        grid=(b // nb, s // CHUNK),
        in_specs=[pl.BlockSpec((nb, CHUNK, w), lambda bi, ci: (bi, ci, 0)),
                  pl.BlockSpec((nb, CHUNK, w), lambda bi, ci: (bi, ci, 1)),
                  pl.BlockSpec((nb, CHUNK, w), lambda bi, ci: (bi, ci, 2)),
                  pl.BlockSpec((nb, CHUNK, 4 * LANES), lambda bi, ci: (bi, ci, 3 * w // (4 * LANES))),
                  pl.BlockSpec(vecs.shape, const2),
                  pl.BlockSpec(wd.shape, const2),
                  pl.BlockSpec(wa.shape, const2),
                  pl.BlockSpec(wg.shape, const2),
                  pl.BlockSpec(mk.shape, const3),
                  pl.BlockSpec(mc.shape, const3),
                  pl.BlockSpec(tri.shape, const2),
                  pl.BlockSpec(ones_bd.shape, const2)],
        out_specs=pl.BlockSpec((nb, CHUNK, w), lambda bi, ci: (bi, ci, 0)),
        out_shape=jax.ShapeDtypeStruct((b, s, w), BF16),
        scratch_shapes=[pltpu.VMEM((nb, N_GROUPS, GROUP, GROUP), F32)],
        compiler_params=pltpu.CompilerParams(dimension_semantics=("arbitrary", "arbitrary"),
                                             vmem_limit_bytes=VMEM_LIMIT),
        name="rwkv",
    )(p3, p3, p3, p3, vecs, wd, wa, wg, mk, mc, tri, ones_bd)


def _mlaproj_kernel(pq_ref, pkv_ref, pos_ref, invf_ref, gq_ref, gkv_ref, wq_ref, wkv_ref,
                    q_ref, k_ref, v_ref):
    q = _bdot(_rms(pq_ref[...], gq_ref[...]), wq_ref[...])
    pkv = pkv_ref[...]
    kv = _bdot(_rms(pkv[:, 0:KV_LORA_RANK], gkv_ref[...]), wkv_ref[...])
    ang = pos_ref[...].astype(F32) * invf_ref[...]
    lane = lax.broadcasted_iota(jnp.int32, ang.shape, 1)
    half = QK_ROPE_DIM // 2
    cos_f = jnp.where(lane < QK_ROPE_DIM, jnp.cos(ang), 0.0)
    sin = jnp.sin(ang)
    sin_f = jnp.where(lane < half, -sin, jnp.where(lane < QK_ROPE_DIM, sin, 0.0))
    k_rot = pkv[:, KV_LORA_RANK:KV_LORA_RANK + LANES] * cos_f + pkv[:, KV_LORA_RANK + LANES:] * sin_f
    scale = (QK_NOPE_DIM + QK_ROPE_DIM) ** -0.5 * 1.4426950408889634
    swap0 = MLA_HEADS * QK_PAD
    for h in range(MLA_HEADS):
        q_nope = q[:, h * QK_PAD:h * QK_PAD + LANES]
        q_rot = (q[:, h * QK_PAD + LANES:(h + 1) * QK_PAD] * cos_f
                 + q[:, swap0 + h * LANES:swap0 + (h + 1) * LANES] * sin_f)
        q_ref[0, h] = (jnp.concatenate([q_nope, q_rot], axis=1) * scale).astype(q_ref.dtype)
        k_ref[0, h] = jnp.concatenate([kv[:, h * LANES:(h + 1) * LANES], k_rot], axis=1).astype(k_ref.dtype)
        v_ref[0, h] = kv[:, (MLA_HEADS + h) * LANES:(MLA_HEADS + h + 1) * LANES].astype(v_ref.dtype)


def _mlaproj_call(p2, pos2, invf, gq, gkv, wq, wkv, batch, seq, tm):
    t = p2.shape[0]
    tps = seq // tm
    blk = 4 * LANES
    q_blk = (IN_COLS_PAD - 2 * blk) // blk
    hd = MLA_HEADS
    out_map = lambda i: (i // tps, 0, i % tps, 0)
    const = lambda i: (0, 0)
    return pl.pallas_call(
        _mlaproj_kernel,
        grid=(t // tm,),
        in_specs=[pl.BlockSpec((tm, blk), lambda i: (i, q_blk)),
                  pl.BlockSpec((tm, blk), lambda i: (i, q_blk + 1)),
                  pl.BlockSpec((tm, 1), lambda i: (i, 0)),
                  pl.BlockSpec(invf.shape, const),
                  pl.BlockSpec(gq.shape, const),
                  pl.BlockSpec(gkv.shape, const),
                  pl.BlockSpec(wq.shape, const),
                  pl.BlockSpec(wkv.shape, const)],
        out_specs=[pl.BlockSpec((1, hd, tm, QK_PAD), out_map),
                   pl.BlockSpec((1, hd, tm, QK_PAD), out_map),
                   pl.BlockSpec((1, hd, tm, V_HEAD_DIM), out_map)],
        out_shape=[jax.ShapeDtypeStruct((batch, hd, seq, QK_PAD), BF16),
                   jax.ShapeDtypeStruct((batch, hd, seq, QK_PAD), BF16),
                   jax.ShapeDtypeStruct((batch, hd, seq, V_HEAD_DIM), BF16)],
        compiler_params=pltpu.CompilerParams(dimension_semantics=("arbitrary",),
                                             vmem_limit_bytes=VMEM_LIMIT),
        name="mlaproj",
    )(p2, p2, pos2, invf, gq, gkv, wq, wkv)


ATTN_HEADS_PER_STEP = 1


def _attn_kernel(q_ref, k_ref, v_ref, o_ref, *, tq):
    i = pl.program_id(2)
    ones_col = (lax.broadcasted_iota(jnp.int32, (tq, LANES), 1) == 0).astype(BF16)

    heads = range(ATTN_HEADS_PER_STEP)

    def qk(start):
        return [lax.dot_general(q_ref[0, hh], k_ref[0, hh, pl.ds(start, tq), :],
                                (((1,), (1,)), ((), ())), preferred_element_type=F32)
                for hh in heads]

    def softmax_pv(state, scores, start):
        new = []
        for hh, ((m, acc), s) in enumerate(zip(state, scores)):
            m_new = jnp.maximum(m, jnp.max(s, axis=-1, keepdims=True))
            alpha = jnp.exp2(m - m_new)
            p = jnp.exp2(s - m_new).astype(BF16)
            vj = jnp.concatenate([v_ref[0, hh, pl.ds(start, tq), :], ones_col], axis=1)
            new.append((m_new, alpha * acc + jnp.dot(p, vj, preferred_element_type=F32)))
        return tuple(new)

    def body(j, carry):
        state, scores = carry
        nxt = qk(pl.multiple_of((j + 1) * tq, tq))
        return softmax_pv(state, scores, pl.multiple_of(j * tq, tq)), tuple(nxt)

    init = tuple((jnp.full((tq, 1), -jnp.inf, F32), jnp.zeros((tq, 2 * V_HEAD_DIM), F32))
                 for _ in heads)
    state, scores = lax.fori_loop(0, i, body, (init, tuple(qk(0))))
    row = lax.broadcasted_iota(jnp.int32, (tq, tq), 0) // CHUNK
    col = lax.broadcasted_iota(jnp.int32, (tq, tq), 1) // CHUNK
    scores = [jnp.where(row >= col, s, -jnp.inf) for s in scores]
    final = softmax_pv(state, scores, pl.multiple_of(i * tq, tq))
    for hh, (_, acc) in enumerate(final):
        o_ref[0, :, hh * V_HEAD_DIM:(hh + 1) * V_HEAD_DIM] = (
            acc[:, 0:V_HEAD_DIM] / acc[:, V_HEAD_DIM:V_HEAD_DIM + 1]).astype(o_ref.dtype)


def _attn_call(q4, k4, v4, tq):
    b, h, s, _ = q4.shape
    hp = ATTN_HEADS_PER_STEP
    return pl.pallas_call(
        functools.partial(_attn_kernel, tq=tq),
        grid=(b, h // hp, s // tq),
        in_specs=[pl.BlockSpec((1, hp, tq, QK_PAD), lambda bi, hi, i: (bi, hi, i, 0)),
                  pl.BlockSpec((1, hp, s, QK_PAD), lambda bi, hi, i: (bi, hi, 0, 0)),
                  pl.BlockSpec((1, hp, s, V_HEAD_DIM), lambda bi, hi, i: (bi, hi, 0, 0))],
        out_specs=pl.BlockSpec((1, tq, hp * V_HEAD_DIM), lambda bi, hi, i: (bi, i, hi)),
        out_shape=jax.ShapeDtypeStruct((b, s, h * V_HEAD_DIM), BF16),
        compiler_params=pltpu.CompilerParams(
            dimension_semantics=("arbitrary", "arbitrary", "arbitrary"),
            vmem_limit_bytes=VMEM_LIMIT),
        name="attn",
    )(q4, k4, v4)


def _outproj_kernel(yr_ref, ym_ref, x_ref, mod_ref, g_ref, gffn_ref, w_ref, o_ref, h_ref):
    half = RWKV_WIDTH
    o = (jnp.dot(yr_ref[...], w_ref[0:half, :], preferred_element_type=F32)
         + jnp.dot(ym_ref[...], w_ref[half:, :], preferred_element_type=F32))
    x_mid = x_ref[...] + mod_ref[0, 2:3, :] * _rms(o, g_ref[...])
    o_ref[...] = x_mid
    h = _rms(x_mid, gffn_ref[...]) * (1.0 + mod_ref[0, 4:5, :]) + mod_ref[0, 3:4, :]
    h_ref[...] = h.astype(h_ref.dtype)


def _outproj_call(yr, ym, x2, mod3, g_post, g_pre_ffn, w_out, seq, tm):
    t, d = x2.shape
    tps = seq // tm
    return pl.pallas_call(
        _outproj_kernel,
        grid=(t // tm,),
        in_specs=[pl.BlockSpec((tm, RWKV_WIDTH), lambda i: (i, 0)),
                  pl.BlockSpec((tm, MLA_WIDTH), lambda i: (i, 0)),
                  pl.BlockSpec((tm, d), lambda i: (i, 0)),
                  pl.BlockSpec((1, N_MOD, d), lambda i: (i // tps, 0, 0)),
                  pl.BlockSpec((1, d), lambda i: (0, 0)),
                  pl.BlockSpec((1, d), lambda i: (0, 0)),
                  pl.BlockSpec(w_out.shape, lambda i: (0, 0))],
        out_specs=[pl.BlockSpec((tm, d), lambda i: (i, 0)),
                   pl.BlockSpec((tm, d), lambda i: (i, 0))],
        out_shape=[jax.ShapeDtypeStruct((t, d), F32), jax.ShapeDtypeStruct((t, d), BF16)],
        compiler_params=pltpu.CompilerParams(dimension_semantics=("arbitrary",),
                                             vmem_limit_bytes=VMEM_LIMIT),
        name="outproj",
    )(yr, ym, x2, mod3, g_post, g_pre_ffn, w_out)


FFN_ROWS = 64
FFN_COL_SPLIT = 1


def _gelu_tanh(x):
    c = 0.7978845608028654
    return 0.5 * x * (1.0 + jnp.tanh(c * (x + 0.044715 * (x * x * x))))


def _ffn_kernel(h_ref, hh_ref, x_ref, mod_ref, wg_ref, wv_ref, cwg_ref, cwv_ref,
                cbg_ref, cbv_ref, wd_ref, gpost_ref, o_ref, hcat_ref, acc_ref, ug_ref, uv_ref, act_ref,
                *, tiles_per_seq):
    i = pl.program_id(0)
    j = pl.program_id(1)
    tm, tf = act_ref.shape
    w = tf // FFN_COL_SPLIT

    @pl.when(j == 0)
    def _prologue():
        keep = (i % tiles_per_seq) != 0
        hcat_ref[0:HALO, :] = jnp.where(keep, hh_ref[...], jnp.zeros_like(hh_ref))
        hcat_ref[HALO:, :] = h_ref[...]
        acc_ref[...] = jnp.zeros_like(acc_ref)

    hb = hcat_ref[...]

    def up(c):
        cs = slice(c * w, (c + 1) * w)
        ug_ref[:, cs] = jnp.dot(hb, wg_ref[:, cs], preferred_element_type=F32)
        uv_ref[:, cs] = jnp.dot(hb, wv_ref[:, cs], preferred_element_type=F32)

    def conv(u_ref, r0, cs, cw_ref, cb_ref):
        return (cb_ref[:, cs] + cw_ref[2:3, cs] * u_ref[pl.ds(r0, FFN_ROWS), cs]
                + cw_ref[1:2, cs] * u_ref[pl.ds(r0 - 1, FFN_ROWS), cs]
                + cw_ref[0:1, cs] * u_ref[pl.ds(r0 - 2, FFN_ROWS), cs])

    def activate(c):
        cs = slice(c * w, (c + 1) * w)
        for rc in range(tm // FFN_ROWS):
            r0 = HALO + rc * FFN_ROWS
            act = (_gelu_tanh(conv(ug_ref, r0, cs, cwg_ref, cbg_ref))
                   * conv(uv_ref, r0, cs, cwv_ref, cbv_ref))
            act_ref[rc * FFN_ROWS:(rc + 1) * FFN_ROWS, cs] = act.astype(BF16)

    def down(c):
        cs = slice(c * w, (c + 1) * w)
        return jnp.dot(act_ref[:, cs], wd_ref[cs, :], preferred_element_type=F32)

    up(0)
    total = None
    for c in range(FFN_COL_SPLIT):
        if c + 1 < FFN_COL_SPLIT:
            up(c + 1)
        activate(c)
        part = down(c)
        total = part if total is None else total + part
    acc_ref[...] += total

    @pl.when(j == pl.num_programs(1) - 1)
    def _epilogue():
        gate = mod_ref[0, 5:6, :]
        o_ref[...] = x_ref[...] + gate * _rms(acc_ref[...], gpost_ref[...])


def _ffn_call(h2, x2, mod3, w_up, conv_w, conv_b, w_down, g_post, seq, tm, tf):
    t, d = x2.shape
    tps = seq // tm
    hb = tm // HALO
    nf = D_FF // tf
    return pl.pallas_call(
        functools.partial(_ffn_kernel, tiles_per_seq=tps),
        grid=(t // tm, nf),
        in_specs=[pl.BlockSpec((tm, d), lambda i, j: (i, 0)),
                  pl.BlockSpec((HALO, d), lambda i, j: (jnp.maximum(i * hb - 1, 0), 0)),
                  pl.BlockSpec((tm, d), lambda i, j: (i, 0)),
                  pl.BlockSpec((1, N_MOD, d), lambda i, j: (i // tps, 0, 0)),
                  pl.BlockSpec((d, tf), lambda i, j: (0, j)),
                  pl.BlockSpec((d, tf), lambda i, j: (0, nf + j)),
                  pl.BlockSpec((3, tf), lambda i, j: (0, j)),
                  pl.BlockSpec((3, tf), lambda i, j: (0, nf + j)),
                  pl.BlockSpec((1, tf), lambda i, j: (0, j)),
                  pl.BlockSpec((1, tf), lambda i, j: (0, nf + j)),
                  pl.BlockSpec((tf, d), lambda i, j: (j, 0)),
                  pl.BlockSpec((1, d), lambda i, j: (0, 0))],
        out_specs=pl.BlockSpec((tm, d), lambda i, j: (i, 0)),
        out_shape=jax.ShapeDtypeStruct((t, d), F32),
        scratch_shapes=[pltpu.VMEM((tm + HALO, d), BF16), pltpu.VMEM((tm, d), F32),
                        pltpu.VMEM((tm + HALO, tf), F32), pltpu.VMEM((tm + HALO, tf), F32),
                        pltpu.VMEM((tm, tf), BF16)],
        compiler_params=pltpu.CompilerParams(dimension_semantics=("arbitrary", "arbitrary"),
                                             vmem_limit_bytes=VMEM_LIMIT),
        name="ffn",
    )(h2, h2, x2, mod3, w_up, w_up, conv_w, conv_w, conv_b, conv_b, w_down, g_post)


def _pad_cols(w, n):
    return jnp.pad(w, ((0, 0), (0, n - w.shape[1])))


def _layout_w_in(w_in, mu_shift):
    w3 = 3 * RWKV_WIDTH
    o_wd = w3
    o_ad = o_wd + DECAY_LORA
    o_gd = o_ad + ICLR_LORA
    o_q = o_gd + GATE_LORA
    o_kv = o_q + Q_LORA_RANK
    o_kr = o_kv + KV_LORA_RANK
    half = QK_ROPE_DIM // 2

    def lay(m):
        kr = m[:, o_kr:o_kr + QK_ROPE_DIM]
        return jnp.concatenate([
            m[:, 0:w3],
            _pad_cols(m[:, o_wd:o_ad], LANES),
            _pad_cols(m[:, o_ad:o_gd], LANES),
            m[:, o_gd:o_q],
            m[:, o_q:o_kv],
            m[:, o_kv:o_kr],
            _pad_cols(kr, LANES),
            _pad_cols(jnp.concatenate([kr[:, half:], kr[:, :half]], axis=1), LANES),
        ], axis=1)

    mu_full = jnp.concatenate([mu_shift, jnp.zeros((w_in.shape[1] - mu_shift.shape[0],), F32)])
    w_p = lay(w_in).astype(BF16)
    mu_p = lay(mu_full[None, :])
    return w_p, mu_p


def _layout_w_q(w_q_up):
    dn, dr = QK_NOPE_DIM, QK_ROPE_DIM
    half = dr // 2
    w = w_q_up.reshape(Q_LORA_RANK, MLA_HEADS, dn + dr)
    nope, u1, u2 = w[..., :dn], w[..., dn:dn + half], w[..., dn + half:]
    z = jnp.zeros(u1.shape[:-1] + (LANES - dr,), w.dtype)
    main = jnp.concatenate([nope, u1, u2, z], axis=-1).reshape(Q_LORA_RANK, MLA_HEADS * QK_PAD)
    swap = jnp.concatenate([u2, u1, z], axis=-1).reshape(Q_LORA_RANK, MLA_HEADS * LANES)
    return jnp.concatenate([main, swap], axis=1).astype(BF16)


def _layout_w_kv(w_kv_up):
    w = w_kv_up.reshape(KV_LORA_RANK, MLA_HEADS, QK_NOPE_DIM + V_HEAD_DIM)
    kn = w[..., :QK_NOPE_DIM].reshape(KV_LORA_RANK, MLA_HEADS * QK_NOPE_DIM)
    vv = w[..., QK_NOPE_DIM:].reshape(KV_LORA_RANK, MLA_HEADS * V_HEAD_DIM)
    return jnp.concatenate([kn, vv], axis=1).astype(BF16)


def _pad_rows(w, n):
    return jnp.pad(w, ((0, n - w.shape[0]), (0, 0)))


def _block(x, c, positions, w_mod, b_mod, g_pre_mix, w_in, mu_shift, w0, w_decay_up, a0,
           w_iclr_up, w_gate_up, k_k, k_a, r_k, ln_x_g, ln_x_b, q_norm_g, w_q_up, kv_norm_g,
           w_kv_up, w_out, g_post_mix, g_pre_ffn, w_ffn_up, conv_w, conv_b, w_ffn_down,
           g_post_ffn, *, tm_in, tn_in, nb_rwkv, tm_mla, tq, tm_out, tm_ffn, tf):
    b, s, d = x.shape
    t = b * s
    x2 = x.reshape(t, d)

    c_pad = jnp.pad(c, ((0, 8 - b % 8 if b % 8 else 0), (0, 0)))
    mod = _mod_call(c_pad, w_mod, b_mod[None, :])[:b]
    mod3 = mod.reshape(b, N_MOD, d)

    w_in_p, mu_p = _layout_w_in(w_in, mu_shift)
    p2 = _inproj_call(x2, mod3, g_pre_mix[None, :], w_in_p, mu_p, s, tm_in, tn_in)

    vecs = jnp.stack([w0, a0, k_k, k_a, r_k.reshape(-1), ln_x_g, ln_x_b, jnp.zeros_like(w0)])
    y_rwkv = _rwkv_call(p2.reshape(b, s, IN_COLS_PAD), vecs,
                        _pad_rows(w_decay_up, LANES).astype(BF16),
                        _pad_rows(w_iclr_up, LANES).astype(BF16),
                        w_gate_up.astype(BF16), nb_rwkv)

    half = QK_ROPE_DIM // 2
    inv_freq = ROPE_THETA ** (-jnp.arange(0, QK_ROPE_DIM, 2, dtype=F32) / QK_ROPE_DIM)
    invf = jnp.concatenate([inv_freq, inv_freq, jnp.zeros((LANES - 2 * half,), F32)])[None, :]
    q4, k4, v4 = _mlaproj_call(p2, positions.reshape(t, 1), invf, q_norm_g[None, :],
                               kv_norm_g[None, :], _layout_w_q(w_q_up), _layout_w_kv(w_kv_up),
                               b, s, tm_mla)
    y_mla = _attn_call(q4, k4, v4, tq)

    x_mid, h_ffn = _outproj_call(y_rwkv.reshape(t, RWKV_WIDTH), y_mla.reshape(t, MLA_WIDTH), x2,
                                 mod3, g_post_mix[None, :], g_pre_ffn[None, :],
                                 w_out.astype(BF16), s, tm_out)
    out = _ffn_call(h_ffn, x_mid, mod3, w_ffn_up.astype(BF16), conv_w, conv_b[None, :],
                    w_ffn_down.astype(BF16), g_post_ffn[None, :], s, tm_ffn, tf)
    return out.reshape(b, s, d)


def kernel(x, c, positions, w_mod, b_mod, g_pre_mix, w_in, mu_shift, w0, w_decay_up, a0, w_iclr_up, w_gate_up, k_k, k_a, r_k, ln_x_g, ln_x_b, q_norm_g, w_q_up, kv_norm_g, w_kv_up, w_out, g_post_mix, g_pre_ffn, w_ffn_up, conv_w, conv_b, w_ffn_down, g_post_ffn):
    args = (x, c, positions, w_mod, b_mod, g_pre_mix, w_in, mu_shift, w0, w_decay_up, a0,
            w_iclr_up, w_gate_up, k_k, k_a, r_k, ln_x_g, ln_x_b, q_norm_g, w_q_up, kv_norm_g,
            w_kv_up, w_out, g_post_mix, g_pre_ffn, w_ffn_up, conv_w, conv_b, w_ffn_down,
            g_post_ffn)
    for l in range(w_mod.shape[0]):
        layer = [a[l] for a in args[3:]]
        x = _block(x, c, positions, *layer, tm_in=512, tn_in=1536, nb_rwkv=2, tm_mla=512, tq=1024,
                   tm_out=512, tm_ffn=512, tf=512)
    return x
```

```python
import functools

import numpy as np
import jax
import jax.numpy as jnp
from jax import lax
from jax.experimental import pallas as pl
from jax.experimental.pallas import tpu as pltpu

F32 = jnp.float32
BF16 = jnp.bfloat16

D_MODEL = 2048
CHUNK = 64
RWKV_HEAD_DIM = 64
RWKV_WIDTH = 1024
RWKV_HEADS = 16
DECAY_LORA = 96
ICLR_LORA = 96
GATE_LORA = 256
GN_EPS = 64e-5
QK_NOPE_DIM = 128
QK_ROPE_DIM = 64
V_HEAD_DIM = 128
MLA_WIDTH = 1024
MLA_HEADS = 8
Q_LORA_RANK = 512
KV_LORA_RANK = 256
ROPE_THETA = 10000.0
D_FF = 5632
NORM_EPS = 1e-6
N_MOD = 6

LANES = 128
HALO = 16
GROUP = 256
HEADS_PER_GROUP = GROUP // RWKV_HEAD_DIM
N_GROUPS = RWKV_WIDTH // GROUP
QK_PAD = 256
V_PAD = 256
IN_COLS_PAD = 4608
VMEM_LIMIT = 56 * 1024 * 1024


def _bdot(a, b):
    return jnp.dot(a.astype(BF16), b.astype(BF16), preferred_element_type=F32)


def _split2(x):
    hi = x.astype(BF16)
    lo = (x - hi.astype(F32)).astype(BF16)
    return hi, lo


def _sigmoid(x):
    return 1.0 / (1.0 + jnp.exp(-x))


def _rms(xv, g):
    ms = jnp.mean(xv * xv, axis=-1, keepdims=True)
    return xv * lax.rsqrt(ms + NORM_EPS) * g


def _mod_kernel(c_ref, w_ref, b_ref, o_ref):
    c = c_ref[...]
    s = c * _sigmoid(c)
    o_ref[...] = _bdot(s, w_ref[...]) + b_ref[...]


def _mod_call(c_pad, w_mod, b_mod, tn=1024):
    rows, d = c_pad.shape
    n = w_mod.shape[1]
    return pl.pallas_call(
        _mod_kernel,
        grid=(n // tn,),
        in_specs=[pl.BlockSpec((rows, d), lambda j: (0, 0)),
                  pl.BlockSpec((d, tn), lambda j: (0, j)),
                  pl.BlockSpec((1, tn), lambda j: (0, j))],
        out_specs=pl.BlockSpec((rows, tn), lambda j: (0, j)),
        out_shape=jax.ShapeDtypeStruct((rows, n), F32),
        compiler_params=pltpu.CompilerParams(dimension_semantics=("arbitrary",),
                                             vmem_limit_bytes=VMEM_LIMIT),
        name="mod",
    )(c_pad, w_mod, b_mod)


def _inproj_kernel(x_ref, xh_ref, mod_ref, g_ref, w_ref, mu_ref, o_ref, *, tiles_per_seq):
    i = pl.program_id(1)
    first = (i % tiles_per_seq) == 0
    shift = mod_ref[0, 0:1, :]
    scale = mod_ref[0, 1:2, :]
    g = g_ref[...]
    h = _rms(x_ref[...], g) * (1.0 + scale) + shift
    hh = _rms(xh_ref[...], g) * (1.0 + scale) + shift
    hh = jnp.where(first, 0.0, hh)
    hcat = jnp.concatenate([hh, h], axis=0).astype(BF16)
    p = jnp.dot(hcat, w_ref[...], preferred_element_type=F32)
    prev = pltpu.roll(p, 1, axis=0)
    out = p + (prev - p) * mu_ref[...]
    o_ref[...] = out[HALO:, :]


def _inproj_call(x2, mod3, g_pre, w_in_p, mu_p, seq, tm, tn):
    t, d = x2.shape
    n = w_in_p.shape[1]
    tps = seq // tm
    hb = tm // HALO
    return pl.pallas_call(
        functools.partial(_inproj_kernel, tiles_per_seq=tps),
        grid=(n // tn, t // tm),
        in_specs=[pl.BlockSpec((tm, d), lambda j, i: (i, 0)),
                  pl.BlockSpec((HALO, d), lambda j, i: (jnp.maximum(i * hb - 1, 0), 0)),
                  pl.BlockSpec((1, N_MOD, d), lambda j, i: (i // tps, 0, 0)),
                  pl.BlockSpec((1, d), lambda j, i: (0, 0)),
                  pl.BlockSpec((d, tn), lambda j, i: (0, j)),
                  pl.BlockSpec((1, tn), lambda j, i: (0, j))],
        out_specs=pl.BlockSpec((tm, tn), lambda j, i: (i, j)),
        out_shape=jax.ShapeDtypeStruct((t, n), F32),
        compiler_params=pltpu.CompilerParams(dimension_semantics=("arbitrary", "arbitrary"),
                                             vmem_limit_bytes=VMEM_LIMIT),
        name="inproj",
    )(x2, x2, mod3, g_pre, w_in_p, mu_p)


def _rwkv_masks(nb):
    i = np.arange(GROUP)[:, None]
    j = np.arange(GROUP)[None, :]
    n = RWKV_HEAD_DIM
    bd = (i // n) == (j // n)
    sl = bd & ((j % n) < (i % n))
    mk = np.stack([
        bd,
        sl & ((i // 8) == (j // 8)),
        bd & ((i // 16) == (j // 16)) & ((i // 8) > (j // 8)),
        bd & ((i // 32) == (j // 32)) & ((i // 16) > (j // 16)),
        bd & ((i // 32) > (j // 32)),
        i == j,
    ]).astype(np.float32)
    t = np.arange(CHUNK)[:, None]
    mc = np.stack([(j % n) < t, (j % n) <= t]).astype(np.float32)
    ti = np.arange(nb * CHUNK)[:, None]
    tj = np.arange(nb * CHUNK)[None, :]
    tri = ((ti // CHUNK == tj // CHUNK) & (tj <= ti)).astype(np.float32)
    return mk, mc, tri


def _rwkv_kernel(r_ref, k_ref, v_ref, lo_ref, vec_ref, wd_ref, wa_ref, wg_ref,
                 mk_ref, mc_ref, tri_ref, e_ref, o_ref, st_ref, *, nb):
    @pl.when(pl.program_id(1) == 0)
    def _reset_state():
        st_ref[...] = jnp.zeros_like(st_ref)

    def rows(ref):
        return jnp.concatenate([ref[bb] for bb in range(nb)], axis=0)

    r = rows(r_ref)
    k = rows(k_ref)
    v = rows(v_ref)
    lo = rows(lo_ref)
    w0 = vec_ref[0:1, :]
    a0 = vec_ref[1:2, :]
    k_k = vec_ref[2:3, :]
    k_a = vec_ref[3:4, :]
    r_k = vec_ref[4:5, :]
    ln_g = vec_ref[5:6, :]
    ln_b = vec_ref[6:7, :]

    z = -(w0 + _bdot(jnp.tanh(lo[:, 0:LANES]), wd_ref[...]))
    softplus = jnp.maximum(z, 0.0) + jnp.log(1.0 + jnp.exp(-jnp.abs(z)))
    lw = -jnp.exp(-softplus - 0.5)
    a = _sigmoid(a0 + _bdot(lo[:, LANES:2 * LANES], wa_ref[...]))
    gate = _bdot(_sigmoid(lo[:, 2 * LANES:4 * LANES]), wg_ref[...])
    kk = k * k_k
    kp = k * (1.0 + (a - 1.0) * k_a)

    tri = tri_ref[...]
    l1 = lw.astype(BF16)
    rem = lw - l1.astype(F32)
    l2 = rem.astype(BF16)
    l3 = (rem - l2.astype(F32)).astype(BF16)
    cum = (jnp.dot(tri, l1, preferred_element_type=F32)
           + jnp.dot(tri, l2, preferred_element_type=F32)
           + jnp.dot(tri, l3, preferred_element_type=F32))
    cum_last = [cum[(bb + 1) * CHUNK - 1:(bb + 1) * CHUNK, :] for bb in range(nb)]
    cum_last_rows = jnp.concatenate(
        [jnp.broadcast_to(cl, (CHUNK, cl.shape[1])) for cl in cum_last], axis=0)
    e_pos = jnp.exp(cum)
    e_pos_x = jnp.exp(cum - lw)
    e_neg = jnp.exp(-cum)
    e_end = jnp.exp(cum_last_rows - cum)

    m_bd = mk_ref[0]
    m_b8 = mk_ref[1]
    m_l16 = mk_ref[2]
    m_l32 = mk_ref[3]
    m_l64 = mk_ref[4]
    eye = mk_ref[5]
    c_strict = mc_ref[0]
    c_incl = mc_ref[1]
    ones_bd = e_ref[...]

    def head_sum(x):
        return jnp.dot(x.astype(BF16), ones_bd, preferred_element_type=F32)

    def head_sum2(x):
        hi, lo_ = _split2(x)
        return (jnp.dot(hi, ones_bd, preferred_element_type=F32)
                + jnp.dot(lo_, ones_bd, preferred_element_type=F32))

    def tile4(x):
        return jnp.concatenate([x] * HEADS_PER_GROUP, axis=0) * m_bd

    def fold4(x):
        return (x[0:CHUNK] + x[CHUNK:2 * CHUNK] + x[2 * CHUNK:3 * CHUNK] + x[3 * CHUNK:4 * CHUNK])

    chains = [(bb, gi) for bb in range(nb) for gi in range(N_GROUPS)]
    rsl = [(slice(bb * CHUNK, (bb + 1) * CHUNK), slice(gi * GROUP, (gi + 1) * GROUP))
           for bb, gi in chains]

    def each(fn, *lists):
        return [fn(*args) for args in zip(*lists)]

    def cut(x):
        return [x[rs, sl] for rs, sl in rsl]

    r_g, v_g, kp_g, a_g, kk_g = cut(r), cut(v), cut(kp), cut(a), cut(kk)
    norm = each(lambda x: jnp.sqrt(head_sum(x * x)), kk_g)
    kkn = each(lambda x, n: x / jnp.maximum(n, 1e-12), kk_g, norm)
    ka = each(lambda x, y: x * y, kkn, a_g)
    a_bar = each(lambda x, e: -x * e, kkn, cut(e_pos_x))
    r_bar = each(lambda x, e: x * e, r_g, cut(e_pos))
    b_til = each(lambda x, e: x * e, ka, cut(e_neg))
    k_til = each(lambda x, e: x * e, kp_g, cut(e_neg))
    b_hat = each(lambda x, e: x * e, ka, cut(e_end))
    k_hat = each(lambda x, e: x * e, kp_g, cut(e_end))

    def score_fn(ab, rb, bt, kt):
        lhs = jnp.concatenate([ab, rb], axis=0)
        rhs = jnp.concatenate([tile4(bt), tile4(kt)], axis=0)
        return lax.dot_general(lhs.astype(BF16), rhs.astype(BF16),
                               (((1,), (1,)), ((), ())), preferred_element_type=F32)

    scores = each(score_fn, a_bar, r_bar, b_til, k_til)
    a_ab = each(lambda x: x[0:CHUNK, 0:GROUP] * c_strict, scores)
    a_ak = each(lambda x: x[0:CHUNK, GROUP:] * c_strict, scores)
    a_rb = each(lambda x: x[CHUNK:, 0:GROUP] * c_incl, scores)
    a_rk = each(lambda x: x[CHUNK:, GROUP:] * c_incl, scores)

    a_full = each(lambda x: jnp.concatenate([x] * HEADS_PER_GROUP, axis=0) * m_bd, a_ab)
    n1 = each(lambda x: x * m_b8, a_full)
    n2 = each(lambda x: _bdot(x, x), n1)
    inv = each(lambda x, y: eye + x + _bdot(eye + x, y), n1, n2)
    n4 = each(lambda x: _bdot(x, x), n2)
    inv = each(lambda x, y: x + _bdot(x, y), inv, n4)
    for m_off in (m_l16, m_l32, m_l64):
        tmp = each(lambda x, af: _bdot(x, af * m_off), inv, a_full)
        inv = each(lambda x, t: x + _bdot(t, x), inv, tmp)

    v_bd = each(tile4, v_g)
    akv = each(_bdot, a_ak, v_bd)
    wu = each(lambda t, ab, x: _bdot(t, jnp.concatenate([tile4(ab), tile4(x)], axis=1)),
              inv, a_bar, akv)
    w_cat = each(lambda x: fold4(x[:, 0:GROUP]), wu)
    u_cat = each(lambda x: fold4(x[:, GROUP:]), wu)
    ry = each(_bdot, a_rb, wu)
    r_hat = each(lambda x, y: x + y[:, 0:GROUP], r_bar, ry)
    y_hat = each(lambda y, ak, vb: y[:, GROUP:] + _bdot(ak, vb), ry, a_rk, v_bd)

    state = [st_ref[bb, gi] for bb, gi in chains]
    y = each(lambda rh, st, yh: lax.dot_general(
        rh.astype(BF16), st.astype(BF16), (((1,), (1,)), ((), ())),
        preferred_element_type=F32) + yh, r_hat, state, y_hat)

    def trans_fn(bh, kh, wc, uc, vg):
        lhs_t = jnp.concatenate(
            [jnp.concatenate([wc, uc], axis=1),
             jnp.concatenate([jnp.zeros_like(vg), vg], axis=1)], axis=0)
        rhs_t = jnp.concatenate([bh, kh], axis=0)
        return lax.dot_general(lhs_t.astype(BF16), rhs_t.astype(BF16),
                               (((0,), (0,)), ((), ())), preferred_element_type=F32)

    mc = each(trans_fn, b_hat, k_hat, w_cat, u_cat, v_g)
    for (bb, gi), (_, sl), mci, st in zip(chains, rsl, mc, state):
        st_ref[bb, gi] = (st * jnp.exp(cum_last[bb][:, sl])
                          + _bdot(st, mci[0:GROUP, :] * m_bd) + mci[GROUP:, :] * m_bd)

    inv_n = 1.0 / RWKV_HEAD_DIM
    mean = each(lambda x: head_sum2(x) * inv_n, y)
    dlt = each(lambda x, m: x - m, y, mean)
    var = each(lambda x: head_sum(x * x) * inv_n, dlt)
    bonus = each(lambda rg, kg, vg, rsl_: head_sum(rg * kg * r_k[:, rsl_[1]]) * vg,
                 r_g, kp_g, v_g, rsl)
    for (bb, gi), (rs, sl), d, vr, bo in zip(chains, rsl, dlt, var, bonus):
        yn = d * lax.rsqrt(vr + GN_EPS) * ln_g[:, sl] + ln_b[:, sl]
        o_ref[bb, :, sl] = ((yn + bo) * gate[rs, sl]).astype(o_ref.dtype)


def _rwkv_call(p3, vecs, wd, wa, wg, nb):
    b, s, _ = p3.shape
    w = RWKV_WIDTH
    mk, mc, tri = _rwkv_masks(nb)
    mk = jnp.asarray(mk)
    mc = jnp.asarray(mc)
    tri = jnp.asarray(tri, dtype=BF16)
    ones_bd = mk[0].astype(BF16)
    const2 = lambda bi, ci: (0, 0)
    const3 = lambda bi, ci: (0, 0, 0)
    return pl.pallas_call(
        functools.partial(_rwkv_kernel, nb=nb),
        grid=(b // nb, s // CHUNK),
        in_specs=[pl.BlockSpec((nb, CHUNK, w), lambda bi, ci: (bi, ci, 0)),
                  pl.BlockSpec((nb, CHUNK, w), lambda bi, ci: (bi, ci, 1)),
                  pl.BlockSpec((nb, CHUNK, w), lambda bi, ci: (bi, ci, 2)),
                  pl.BlockSpec((nb, CHUNK, 4 * LANES), lambda bi, ci: (bi, ci, 3 * w // (4 * LANES))),
                  pl.BlockSpec(vecs.shape, const2),
                  pl.BlockSpec(wd.shape, const2),
                  pl.BlockSpec(wa.shape, const2),
                  pl.BlockSpec(wg.shape, const2),
                  pl.BlockSpec(mk.shape, const3),
                  pl.BlockSpec(mc.shape, const3),
                  pl.BlockSpec(tri.shape, const2),
                  pl.BlockSpec(ones_bd.shape, const2)],
        out_specs=pl.BlockSpec((nb, CHUNK, w), lambda bi, ci: (bi, ci, 0)),
        out_shape=jax.ShapeDtypeStruct((b, s, w), BF16),
        scratch_shapes=[pltpu.VMEM((nb, N_GROUPS, GROUP, GROUP), F32)],
        compiler_params=pltpu.CompilerParams(dimension_semantics=("arbitrary", "arbitrary"),
                                             vmem_limit_bytes=VMEM_LIMIT),
        name="rwkv",
    )(p3, p3, p3, p3, vecs, wd, wa, wg, mk, mc, tri, ones_bd)


def _mlaproj_kernel(pq_ref, pkv_ref, pos_ref, invf_ref, gq_ref, gkv_ref, wq_ref, wkv_ref,
                    q_ref, k_ref, v_ref):
    q = _bdot(_rms(pq_ref[...], gq_ref[...]), wq_ref[...])
    pkv = pkv_ref[...]
    kv = _bdot(_rms(pkv[:, 0:KV_LORA_RANK], gkv_ref[...]), wkv_ref[...])
    ang = pos_ref[...].astype(F32) * invf_ref[...]
    lane = lax.broadcasted_iota(jnp.int32, ang.shape, 1)
    half = QK_ROPE_DIM // 2
    cos_f = jnp.where(lane < QK_ROPE_DIM, jnp.cos(ang), 0.0)
    sin = jnp.sin(ang)
    sin_f = jnp.where(lane < half, -sin, jnp.where(lane < QK_ROPE_DIM, sin, 0.0))
    k_rot = pkv[:, KV_LORA_RANK:KV_LORA_RANK + LANES] * cos_f + pkv[:, KV_LORA_RANK + LANES:] * sin_f
    scale = (QK_NOPE_DIM + QK_ROPE_DIM) ** -0.5 * 1.4426950408889634
    swap0 = MLA_HEADS * QK_PAD
    ones_col = (lane == 0).astype(F32)
    for h in range(MLA_HEADS):
        q_nope = q[:, h * QK_PAD:h * QK_PAD + LANES]
        q_rot = (q[:, h * QK_PAD + LANES:(h + 1) * QK_PAD] * cos_f
                 + q[:, swap0 + h * LANES:swap0 + (h + 1) * LANES] * sin_f)
        q_ref[0, h] = (jnp.concatenate([q_nope, q_rot], axis=1) * scale).astype(q_ref.dtype)
        k_ref[0, h] = jnp.concatenate([kv[:, h * LANES:(h + 1) * LANES], k_rot], axis=1).astype(k_ref.dtype)
        v_ref[0, h] = jnp.concatenate(
            [kv[:, (MLA_HEADS + h) * LANES:(MLA_HEADS + h + 1) * LANES], ones_col],
            axis=1).astype(v_ref.dtype)


def _mlaproj_call(p2, pos2, invf, gq, gkv, wq, wkv, batch, seq, tm):
    t = p2.shape[0]
    tps = seq // tm
    blk = 4 * LANES
    q_blk = (IN_COLS_PAD - 2 * blk) // blk
    hd = MLA_HEADS
    out_map = lambda i: (i // tps, 0, i % tps, 0)
    const = lambda i: (0, 0)
    return pl.pallas_call(
        _mlaproj_kernel,
        grid=(t // tm,),
        in_specs=[pl.BlockSpec((tm, blk), lambda i: (i, q_blk)),
                  pl.BlockSpec((tm, blk), lambda i: (i, q_blk + 1)),
                  pl.BlockSpec((tm, 1), lambda i: (i, 0)),
                  pl.BlockSpec(invf.shape, const),
                  pl.BlockSpec(gq.shape, const),
                  pl.BlockSpec(gkv.shape, const),
                  pl.BlockSpec(wq.shape, const),
                  pl.BlockSpec(wkv.shape, const)],
        out_specs=[pl.BlockSpec((1, hd, tm, QK_PAD), out_map),
                   pl.BlockSpec((1, hd, tm, QK_PAD), out_map),
                   pl.BlockSpec((1, hd, tm, V_PAD), out_map)],
        out_shape=[jax.ShapeDtypeStruct((batch, hd, seq, QK_PAD), BF16),
                   jax.ShapeDtypeStruct((batch, hd, seq, QK_PAD), BF16),
                   jax.ShapeDtypeStruct((batch, hd, seq, V_PAD), BF16)],
        compiler_params=pltpu.CompilerParams(dimension_semantics=("arbitrary",),
                                             vmem_limit_bytes=VMEM_LIMIT),
        name="mlaproj",
    )(p2, p2, pos2, invf, gq, gkv, wq, wkv)


ATTN_ROW_PARTS = 4
ATTN_SM_ROWS = 32


def _attn_kernel(q_ref, k_ref, v_ref, o_ref, s_ref, p_ref, m_ref, a_ref, acc_ref, *, tq):
    i = pl.program_id(2)
    rp = tq // ATTN_ROW_PARTS
    m_ref[...] = jnp.full(m_ref.shape, -jnp.inf, F32)
    acc_ref[...] = jnp.zeros_like(acc_ref)

    def block(start, on_diagonal):
        def keys(part):
            return (part + 1) * rp if on_diagonal else tq

        def scores(part):
            rows = slice(part * rp, (part + 1) * rp)
            nk = keys(part)
            s_ref[rows, 0:nk] = lax.dot_general(
                q_ref[0, 0, rows, :], k_ref[0, 0, pl.ds(start, nk), :],
                (((1,), (1,)), ((), ())), preferred_element_type=F32)

        def softmax(part):
            nk = keys(part)
            for c in range(rp // ATTN_SM_ROWS):
                r0 = part * rp + c * ATTN_SM_ROWS
                rows = slice(r0, r0 + ATTN_SM_ROWS)
                sc = s_ref[rows, 0:nk]
                if on_diagonal:
                    row = (r0 + lax.broadcasted_iota(jnp.int32, sc.shape, 0)) // CHUNK
                    col = lax.broadcasted_iota(jnp.int32, sc.shape, 1) // CHUNK
                    sc = jnp.where(row >= col, sc, -jnp.inf)
                m_old = m_ref[rows, :]
                m_new = jnp.maximum(m_old, jnp.max(sc, axis=-1, keepdims=True))
                m_ref[rows, :] = m_new
                a_ref[rows, :] = jnp.exp2(m_old - m_new)
                p_ref[rows, 0:nk] = jnp.exp2(sc - m_new).astype(BF16)

        def weighted_values(part):
            rows = slice(part * rp, (part + 1) * rp)
            nk = keys(part)
            acc_ref[rows, :] = a_ref[rows, :] * acc_ref[rows, :] + jnp.dot(
                p_ref[rows, 0:nk], v_ref[0, 0, pl.ds(start, nk), :], preferred_element_type=F32)

        scores(0)
        for part in range(ATTN_ROW_PARTS):
            if part + 1 < ATTN_ROW_PARTS:
                scores(part + 1)
            softmax(part)
            weighted_values(part)

    def body(j, carry):
        block(pl.multiple_of(j * tq, tq), False)
        return carry

    lax.fori_loop(0, i, body, 0)
    block(pl.multiple_of(i * tq, tq), True)
    acc = acc_ref[...]
    o_ref[0] = (acc[:, 0:V_HEAD_DIM] / acc[:, V_HEAD_DIM:V_HEAD_DIM + 1]).astype(o_ref.dtype)


def _attn_call(q4, k4, v4, tq):
    b, h, s, _ = q4.shape
    return pl.pallas_call(
        functools.partial(_attn_kernel, tq=tq),
        grid=(b, h, s // tq),
        in_specs=[pl.BlockSpec((1, 1, tq, QK_PAD), lambda bi, hi, i: (bi, hi, i, 0)),
                  pl.BlockSpec((1, 1, s, QK_PAD), lambda bi, hi, i: (bi, hi, 0, 0)),
                  pl.BlockSpec((1, 1, s, V_PAD), lambda bi, hi, i: (bi, hi, 0, 0))],
        out_specs=pl.BlockSpec((1, tq, V_HEAD_DIM), lambda bi, hi, i: (bi, i, hi)),
        out_shape=jax.ShapeDtypeStruct((b, s, h * V_HEAD_DIM), BF16),
        scratch_shapes=[pltpu.VMEM((tq, tq), F32), pltpu.VMEM((tq, tq), BF16),
                        pltpu.VMEM((tq, 1), F32), pltpu.VMEM((tq, 1), F32),
                        pltpu.VMEM((tq, V_PAD), F32)],
        compiler_params=pltpu.CompilerParams(
            dimension_semantics=("arbitrary", "arbitrary", "arbitrary"),
            vmem_limit_bytes=VMEM_LIMIT),
        name="attn",
    )(q4, k4, v4)


OUT_ROW_PARTS = 1
NORM_ROWS = 32


def _outproj_kernel(yr_ref, ym_ref, x_ref, mod_ref, g_ref, gffn_ref, w_ref, o_ref, h_ref, acc_ref):
    half = RWKV_WIDTH
    tm = acc_ref.shape[0]
    rp = tm // OUT_ROW_PARTS

    def project(part):
        rows = slice(part * rp, (part + 1) * rp)
        acc_ref[rows, :] = (
            jnp.dot(yr_ref[rows, :], w_ref[0:half, :], preferred_element_type=F32)
            + jnp.dot(ym_ref[rows, :], w_ref[half:, :], preferred_element_type=F32))

    def finish(part):
        for c in range(rp // NORM_ROWS):
            r0 = part * rp + c * NORM_ROWS
            rows = slice(r0, r0 + NORM_ROWS)
            x_mid = x_ref[rows, :] + mod_ref[0, 2:3, :] * _rms(acc_ref[rows, :], g_ref[...])
            o_ref[rows, :] = x_mid
            h = _rms(x_mid, gffn_ref[...]) * (1.0 + mod_ref[0, 4:5, :]) + mod_ref[0, 3:4, :]
            h_ref[rows, :] = h.astype(h_ref.dtype)

    for part in range(OUT_ROW_PARTS):
        project(part)
        finish(part)


def _outproj_call(yr, ym, x2, mod3, g_post, g_pre_ffn, w_out, seq, tm):
    t, d = x2.shape
    tps = seq // tm
    return pl.pallas_call(
        _outproj_kernel,
        grid=(t // tm,),
        in_specs=[pl.BlockSpec((tm, RWKV_WIDTH), lambda i: (i, 0)),
                  pl.BlockSpec((tm, MLA_WIDTH), lambda i: (i, 0)),
                  pl.BlockSpec((tm, d), lambda i: (i, 0)),
                  pl.BlockSpec((1, N_MOD, d), lambda i: (i // tps, 0, 0)),
                  pl.BlockSpec((1, d), lambda i: (0, 0)),
                  pl.BlockSpec((1, d), lambda i: (0, 0)),
                  pl.BlockSpec(w_out.shape, lambda i: (0, 0))],
        out_specs=[pl.BlockSpec((tm, d), lambda i: (i, 0)),
                   pl.BlockSpec((tm, d), lambda i: (i, 0))],
        out_shape=[jax.ShapeDtypeStruct((t, d), F32), jax.ShapeDtypeStruct((t, d), BF16)],
        scratch_shapes=[pltpu.VMEM((tm, d), F32)],
        compiler_params=pltpu.CompilerParams(dimension_semantics=("arbitrary",),
                                             vmem_limit_bytes=VMEM_LIMIT),
        name="outproj",
    )(yr, ym, x2, mod3, g_post, g_pre_ffn, w_out)


FFN_ROWS = 64
FFN_COL_SPLIT = 1


def _gelu_tanh(x):
    c = 0.7978845608028654
    return 0.5 * x * (1.0 + jnp.tanh(c * (x + 0.044715 * (x * x * x))))


def _ffn_kernel(h_ref, hh_ref, x_ref, mod_ref, wg_ref, wv_ref, cwg_ref, cwv_ref,
                cbg_ref, cbv_ref, wd_ref, gpost_ref, o_ref, hcat_ref, acc_ref, ug_ref, uv_ref, act_ref,
                *, tiles_per_seq):
    i = pl.program_id(0)
    j = pl.program_id(1)
    tm, tf = act_ref.shape
    w = tf // FFN_COL_SPLIT

    @pl.when(j == 0)
    def _prologue():
        keep = (i % tiles_per_seq) != 0
        hcat_ref[0:HALO, :] = jnp.where(keep, hh_ref[...], jnp.zeros_like(hh_ref))
        hcat_ref[HALO:, :] = h_ref[...]
        acc_ref[...] = jnp.zeros_like(acc_ref)

    hb = hcat_ref[...]

    def up(c):
        cs = slice(c * w, (c + 1) * w)
        ug_ref[:, cs] = jnp.dot(hb, wg_ref[:, cs], preferred_element_type=F32)
        uv_ref[:, cs] = jnp.dot(hb, wv_ref[:, cs], preferred_element_type=F32)

    def conv(u_ref, r0, cs, cw_ref, cb_ref):
        return (cb_ref[:, cs] + cw_ref[2:3, cs] * u_ref[pl.ds(r0, FFN_ROWS), cs]
                + cw_ref[1:2, cs] * u_ref[pl.ds(r0 - 1, FFN_ROWS), cs]
                + cw_ref[0:1, cs] * u_ref[pl.ds(r0 - 2, FFN_ROWS), cs])

    def activate(c):
        cs = slice(c * w, (c + 1) * w)
        for rc in range(tm // FFN_ROWS):
            r0 = HALO + rc * FFN_ROWS
            act = (_gelu_tanh(conv(ug_ref, r0, cs, cwg_ref, cbg_ref))
                   * conv(uv_ref, r0, cs, cwv_ref, cbv_ref))
            act_ref[rc * FFN_ROWS:(rc + 1) * FFN_ROWS, cs] = act.astype(BF16)

    def down(c):
        cs = slice(c * w, (c + 1) * w)
        return jnp.dot(act_ref[:, cs], wd_ref[cs, :], preferred_element_type=F32)

    up(0)
    total = None
    for c in range(FFN_COL_SPLIT):
        if c + 1 < FFN_COL_SPLIT:
            up(c + 1)
        activate(c)
        part = down(c)
        total = part if total is None else total + part
    acc_ref[...] += total

    @pl.when(j == pl.num_programs(1) - 1)
    def _epilogue():
        for c in range(tm // NORM_ROWS):
            rows = slice(c * NORM_ROWS, (c + 1) * NORM_ROWS)
            o_ref[rows, :] = x_ref[rows, :] + mod_ref[0, 5:6, :] * _rms(acc_ref[rows, :], gpost_ref[...])


def _ffn_call(h2, x2, mod3, w_up, conv_w, conv_b, w_down, g_post, seq, tm, tf):
    t, d = x2.shape
    tps = seq // tm
    hb = tm // HALO
    nf = D_FF // tf
    return pl.pallas_call(
        functools.partial(_ffn_kernel, tiles_per_seq=tps),
        grid=(t // tm, nf),
        in_specs=[pl.BlockSpec((tm, d), lambda i, j: (i, 0)),
                  pl.BlockSpec((HALO, d), lambda i, j: (jnp.maximum(i * hb - 1, 0), 0)),
                  pl.BlockSpec((tm, d), lambda i, j: (i, 0)),
                  pl.BlockSpec((1, N_MOD, d), lambda i, j: (i // tps, 0, 0)),
                  pl.BlockSpec((d, tf), lambda i, j: (0, j)),
                  pl.BlockSpec((d, tf), lambda i, j: (0, nf + j)),
                  pl.BlockSpec((3, tf), lambda i, j: (0, j)),
                  pl.BlockSpec((3, tf), lambda i, j: (0, nf + j)),
                  pl.BlockSpec((1, tf), lambda i, j: (0, j)),
                  pl.BlockSpec((1, tf), lambda i, j: (0, nf + j)),
                  pl.BlockSpec((tf, d), lambda i, j: (j, 0)),
                  pl.BlockSpec((1, d), lambda i, j: (0, 0))],
        out_specs=pl.BlockSpec((tm, d), lambda i, j: (i, 0)),
        out_shape=jax.ShapeDtypeStruct((t, d), F32),
        scratch_shapes=[pltpu.VMEM((tm + HALO, d), BF16), pltpu.VMEM((tm, d), F32),
                        pltpu.VMEM((tm + HALO, tf), F32), pltpu.VMEM((tm + HALO, tf), F32),
                        pltpu.VMEM((tm, tf), BF16)],
        compiler_params=pltpu.CompilerParams(dimension_semantics=("arbitrary", "arbitrary"),
                                             vmem_limit_bytes=VMEM_LIMIT),
        name="ffn",
    )(h2, h2, x2, mod3, w_up, w_up, conv_w, conv_w, conv_b, conv_b, w_down, g_post)


def _pad_cols(w, n):
    return jnp.pad(w, ((0, 0), (0, n - w.shape[1])))


def _layout_w_in(w_in, mu_shift):
    w3 = 3 * RWKV_WIDTH
    o_wd = w3
    o_ad = o_wd + DECAY_LORA
    o_gd = o_ad + ICLR_LORA
    o_q = o_gd + GATE_LORA
    o_kv = o_q + Q_LORA_RANK
    o_kr = o_kv + KV_LORA_RANK
    half = QK_ROPE_DIM // 2

    def lay(m):
        kr = m[:, o_kr:o_kr + QK_ROPE_DIM]
        return jnp.concatenate([
            m[:, 0:w3],
            _pad_cols(m[:, o_wd:o_ad], LANES),
            _pad_cols(m[:, o_ad:o_gd], LANES),
            m[:, o_gd:o_q],
            m[:, o_q:o_kv],
            m[:, o_kv:o_kr],
            _pad_cols(kr, LANES),
            _pad_cols(jnp.concatenate([kr[:, half:], kr[:, :half]], axis=1), LANES),
        ], axis=1)

    mu_full = jnp.concatenate([mu_shift, jnp.zeros((w_in.shape[1] - mu_shift.shape[0],), F32)])
    w_p = lay(w_in).astype(BF16)
    mu_p = lay(mu_full[None, :])
    return w_p, mu_p


def _layout_w_q(w_q_up):
    dn, dr = QK_NOPE_DIM, QK_ROPE_DIM
    half = dr // 2
    w = w_q_up.reshape(Q_LORA_RANK, MLA_HEADS, dn + dr)
    nope, u1, u2 = w[..., :dn], w[..., dn:dn + half], w[..., dn + half:]
    z = jnp.zeros(u1.shape[:-1] + (LANES - dr,), w.dtype)
    main = jnp.concatenate([nope, u1, u2, z], axis=-1).reshape(Q_LORA_RANK, MLA_HEADS * QK_PAD)
    swap = jnp.concatenate([u2, u1, z], axis=-1).reshape(Q_LORA_RANK, MLA_HEADS * LANES)
    return jnp.concatenate([main, swap], axis=1).astype(BF16)


def _layout_w_kv(w_kv_up):
    w = w_kv_up.reshape(KV_LORA_RANK, MLA_HEADS, QK_NOPE_DIM + V_HEAD_DIM)
    kn = w[..., :QK_NOPE_DIM].reshape(KV_LORA_RANK, MLA_HEADS * QK_NOPE_DIM)
    vv = w[..., QK_NOPE_DIM:].reshape(KV_LORA_RANK, MLA_HEADS * V_HEAD_DIM)
    return jnp.concatenate([kn, vv], axis=1).astype(BF16)


def _pad_rows(w, n):
    return jnp.pad(w, ((0, n - w.shape[0]), (0, 0)))


def _block(x, c, positions, w_mod, b_mod, g_pre_mix, w_in, mu_shift, w0, w_decay_up, a0,
           w_iclr_up, w_gate_up, k_k, k_a, r_k, ln_x_g, ln_x_b, q_norm_g, w_q_up, kv_norm_g,
           w_kv_up, w_out, g_post_mix, g_pre_ffn, w_ffn_up, conv_w, conv_b, w_ffn_down,
           g_post_ffn, *, tm_in, tn_in, nb_rwkv, tm_mla, tq, tm_out, tm_ffn, tf):
    b, s, d = x.shape
    t = b * s
    x2 = x.reshape(t, d)

    c_pad = jnp.pad(c, ((0, 8 - b % 8 if b % 8 else 0), (0, 0)))
    mod = _mod_call(c_pad, w_mod, b_mod[None, :])[:b]
    mod3 = mod.reshape(b, N_MOD, d)

    w_in_p, mu_p = _layout_w_in(w_in, mu_shift)
    p2 = _inproj_call(x2, mod3, g_pre_mix[None, :], w_in_p, mu_p, s, tm_in, tn_in)

    vecs = jnp.stack([w0, a0, k_k, k_a, r_k.reshape(-1), ln_x_g, ln_x_b, jnp.zeros_like(w0)])
    y_rwkv = _rwkv_call(p2.reshape(b, s, IN_COLS_PAD), vecs,
                        _pad_rows(w_decay_up, LANES).astype(BF16),
                        _pad_rows(w_iclr_up, LANES).astype(BF16),
                        w_gate_up.astype(BF16), nb_rwkv)

    half = QK_ROPE_DIM // 2
    inv_freq = ROPE_THETA ** (-jnp.arange(0, QK_ROPE_DIM, 2, dtype=F32) / QK_ROPE_DIM)
    invf = jnp.concatenate([inv_freq, inv_freq, jnp.zeros((LANES - 2 * half,), F32)])[None, :]
    q4, k4, v4 = _mlaproj_call(p2, positions.reshape(t, 1), invf, q_norm_g[None, :],
                               kv_norm_g[None, :], _layout_w_q(w_q_up), _layout_w_kv(w_kv_up),
                               b, s, tm_mla)
    y_mla = _attn_call(q4, k4, v4, tq)

    x_mid, h_ffn = _outproj_call(y_rwkv.reshape(t, RWKV_WIDTH), y_mla.reshape(t, MLA_WIDTH), x2,
                                 mod3, g_post_mix[None, :], g_pre_ffn[None, :],
                                 w_out.astype(BF16), s, tm_out)
    out = _ffn_call(h_ffn, x_mid, mod3, w_ffn_up.astype(BF16), conv_w, conv_b[None, :],
                    w_ffn_down.astype(BF16), g_post_ffn[None, :], s, tm_ffn, tf)
    return out.reshape(b, s, d)


def kernel(x, c, positions, w_mod, b_mod, g_pre_mix, w_in, mu_shift, w0, w_decay_up, a0, w_iclr_up, w_gate_up, k_k, k_a, r_k, ln_x_g, ln_x_b, q_norm_g, w_q_up, kv_norm_g, w_kv_up, w_out, g_post_mix, g_pre_ffn, w_ffn_up, conv_w, conv_b, w_ffn_down, g_post_ffn):
    args = (x, c, positions, w_mod, b_mod, g_pre_mix, w_in, mu_shift, w0, w_decay_up, a0,
            w_iclr_up, w_gate_up, k_k, k_a, r_k, ln_x_g, ln_x_b, q_norm_g, w_q_up, kv_norm_g,
            w_kv_up, w_out, g_post_mix, g_pre_ffn, w_ffn_up, conv_w, conv_b, w_ffn_down,
            g_post_ffn)
    for l in range(w_mod.shape[0]):
        layer = [a[l] for a in args[3:]]
        x = _block(x, c, positions, *layer, tm_in=512, tn_in=1536, nb_rwkv=2, tm_mla=512, tq=1024,
                   tm_out=512, tm_ffn=512, tf=512)
    return x
```

```python
import functools

import numpy as np
import jax
import jax.numpy as jnp
from jax import lax
from jax.experimental import pallas as pl
from jax.experimental.pallas import tpu as pltpu

F32 = jnp.float32
BF16 = jnp.bfloat16

D_MODEL = 2048
CHUNK = 64
RWKV_HEAD_DIM = 64
RWKV_WIDTH = 1024
RWKV_HEADS = 16
DECAY_LORA = 96
ICLR_LORA = 96
GATE_LORA = 256
GN_EPS = 64e-5
QK_NOPE_DIM = 128
QK_ROPE_DIM = 64
V_HEAD_DIM = 128
MLA_WIDTH = 1024
MLA_HEADS = 8
Q_LORA_RANK = 512
KV_LORA_RANK = 256
ROPE_THETA = 10000.0
D_FF = 5632
NORM_EPS = 1e-6
N_MOD = 6

LANES = 128
HALO = 16
GROUP = 256
HEADS_PER_GROUP = GROUP // RWKV_HEAD_DIM
N_GROUPS = RWKV_WIDTH // GROUP
QK_PAD = 256
V_PAD = 256
IN_COLS_PAD = 4608
VMEM_LIMIT = 56 * 1024 * 1024


def _bdot(a, b):
    return jnp.dot(a.astype(BF16), b.astype(BF16), preferred_element_type=F32)


def _split2(x):
    hi = x.astype(BF16)
    lo = (x - hi.astype(F32)).astype(BF16)
    return hi, lo


def _sigmoid(x):
    return 1.0 / (1.0 + jnp.exp(-x))


def _rms(xv, g):
    ms = jnp.mean(xv * xv, axis=-1, keepdims=True)
    return xv * lax.rsqrt(ms + NORM_EPS) * g


def _mod_kernel(c_ref, w_ref, b_ref, o_ref):
    c = c_ref[...]
    s = c * _sigmoid(c)
    o_ref[...] = _bdot(s, w_ref[...]) + b_ref[...]


def _mod_call(c_pad, w_mod, b_mod, tn=1024):
    rows, d = c_pad.shape
    n = w_mod.shape[1]
    return pl.pallas_call(
        _mod_kernel,
        grid=(n // tn,),
        in_specs=[pl.BlockSpec((rows, d), lambda j: (0, 0)),
                  pl.BlockSpec((d, tn), lambda j: (0, j)),
                  pl.BlockSpec((1, tn), lambda j: (0, j))],
        out_specs=pl.BlockSpec((rows, tn), lambda j: (0, j)),
        out_shape=jax.ShapeDtypeStruct((rows, n), F32),
        compiler_params=pltpu.CompilerParams(dimension_semantics=("arbitrary",),
                                             vmem_limit_bytes=VMEM_LIMIT),
        name="mod",
    )(c_pad, w_mod, b_mod)


def _inproj_kernel(x_ref, xh_ref, mod_ref, g_ref, w_ref, mu_ref, o_ref, *, tiles_per_seq):
    i = pl.program_id(1)
    first = (i % tiles_per_seq) == 0
    shift = mod_ref[0, 0:1, :]
    scale = mod_ref[0, 1:2, :]
    g = g_ref[...]
    h = _rms(x_ref[...], g) * (1.0 + scale) + shift
    hh = _rms(xh_ref[...], g) * (1.0 + scale) + shift
    hh = jnp.where(first, 0.0, hh)
    hcat = jnp.concatenate([hh, h], axis=0).astype(BF16)
    p = jnp.dot(hcat, w_ref[...], preferred_element_type=F32)
    prev = pltpu.roll(p, 1, axis=0)
    out = p + (prev - p) * mu_ref[...]
    o_ref[...] = out[HALO:, :]


def _inproj_call(x2, mod3, g_pre, w_in_p, mu_p, seq, tm, tn):
    t, d = x2.shape
    n = w_in_p.shape[1]
    tps = seq // tm
    hb = tm // HALO
    return pl.pallas_call(
        functools.partial(_inproj_kernel, tiles_per_seq=tps),
        grid=(n // tn, t // tm),
        in_specs=[pl.BlockSpec((tm, d), lambda j, i: (i, 0)),
                  pl.BlockSpec((HALO, d), lambda j, i: (jnp.maximum(i * hb - 1, 0), 0)),
                  pl.BlockSpec((1, N_MOD, d), lambda j, i: (i // tps, 0, 0)),
                  pl.BlockSpec((1, d), lambda j, i: (0, 0)),
                  pl.BlockSpec((d, tn), lambda j, i: (0, j)),
                  pl.BlockSpec((1, tn), lambda j, i: (0, j))],
        out_specs=pl.BlockSpec((tm, tn), lambda j, i: (i, j)),
        out_shape=jax.ShapeDtypeStruct((t, n), F32),
        compiler_params=pltpu.CompilerParams(dimension_semantics=("arbitrary", "arbitrary"),
                                             vmem_limit_bytes=VMEM_LIMIT),
        name="inproj",
    )(x2, x2, mod3, g_pre, w_in_p, mu_p)


def _rwkv_masks(nb):
    i = np.arange(GROUP)[:, None]
    j = np.arange(GROUP)[None, :]
    n = RWKV_HEAD_DIM
    bd = (i // n) == (j // n)
    sl = bd & ((j % n) < (i % n))
    mk = np.stack([
        bd,
        sl & ((i // 8) == (j // 8)),
        bd & ((i // 16) == (j // 16)) & ((i // 8) > (j // 8)),
        bd & ((i // 32) == (j // 32)) & ((i // 16) > (j // 16)),
        bd & ((i // 32) > (j // 32)),
        i == j,
    ]).astype(np.float32)
    t = np.arange(CHUNK)[:, None]
    mc = np.stack([(j % n) < t, (j % n) <= t]).astype(np.float32)
    ti = np.arange(nb * CHUNK)[:, None]
    tj = np.arange(nb * CHUNK)[None, :]
    tri = ((ti // CHUNK == tj // CHUNK) & (tj <= ti)).astype(np.float32)
    return mk, mc, tri


def _rwkv_kernel(r_ref, k_ref, v_ref, lo_ref, vec_ref, wd_ref, wa_ref, wg_ref,
                 mk_ref, mc_ref, tri_ref, e_ref, o_ref, st_ref, *, nb):
    @pl.when(pl.program_id(1) == 0)
    def _reset_state():
        st_ref[...] = jnp.zeros_like(st_ref)

    def rows(ref):
        return jnp.concatenate([ref[bb] for bb in range(nb)], axis=0)

    r = rows(r_ref)
    k = rows(k_ref)
    v = rows(v_ref)
    lo = rows(lo_ref)
    w0 = vec_ref[0:1, :]
    a0 = vec_ref[1:2, :]
    k_k = vec_ref[2:3, :]
    k_a = vec_ref[3:4, :]
    r_k = vec_ref[4:5, :]
    ln_g = vec_ref[5:6, :]
    ln_b = vec_ref[6:7, :]

    z = -(w0 + _bdot(jnp.tanh(lo[:, 0:LANES]), wd_ref[...]))
    softplus = jnp.maximum(z, 0.0) + jnp.log(1.0 + jnp.exp(-jnp.abs(z)))
    lw = -jnp.exp(-softplus - 0.5)
    a = _sigmoid(a0 + _bdot(lo[:, LANES:2 * LANES], wa_ref[...]))
    gate = _bdot(_sigmoid(lo[:, 2 * LANES:4 * LANES]), wg_ref[...])
    kk = k * k_k
    kp = k * (1.0 + (a - 1.0) * k_a)

    tri = tri_ref[...]
    l1 = lw.astype(BF16)
    rem = lw - l1.astype(F32)
    l2 = rem.astype(BF16)
    l3 = (rem - l2.astype(F32)).astype(BF16)
    cum = (jnp.dot(tri, l1, preferred_element_type=F32)
           + jnp.dot(tri, l2, preferred_element_type=F32)
           + jnp.dot(tri, l3, preferred_element_type=F32))
    cum_last = [cum[(bb + 1) * CHUNK - 1:(bb + 1) * CHUNK, :] for bb in range(nb)]
    cum_last_rows = jnp.concatenate(
        [jnp.broadcast_to(cl, (CHUNK, cl.shape[1])) for cl in cum_last], axis=0)
    e_pos = jnp.exp(cum)
    e_pos_x = jnp.exp(cum - lw)
    e_neg = jnp.exp(-cum)
    e_end = jnp.exp(cum_last_rows - cum)

    m_bd = mk_ref[0]
    m_b8 = mk_ref[1].astype(BF16)
    m_l16 = mk_ref[2].astype(BF16)
    m_l32 = mk_ref[3].astype(BF16)
    m_l64 = mk_ref[4].astype(BF16)
    eye = mk_ref[5]
    c_strict = mc_ref[0]
    c_incl = mc_ref[1]
    ones_bd = e_ref[...]

    def head_sum(x):
        return jnp.dot(x.astype(BF16), ones_bd, preferred_element_type=F32)

    def head_sum2(x):
        hi, lo_ = _split2(x)
        return (jnp.dot(hi, ones_bd, preferred_element_type=F32)
                + jnp.dot(lo_, ones_bd, preferred_element_type=F32))

    def tile4(x):
        return jnp.concatenate([x] * HEADS_PER_GROUP, axis=0) * m_bd

    def fold4(x):
        return (x[0:CHUNK] + x[CHUNK:2 * CHUNK] + x[2 * CHUNK:3 * CHUNK] + x[3 * CHUNK:4 * CHUNK])

    chains = [(bb, gi) for bb in range(nb) for gi in range(N_GROUPS)]
    rsl = [(slice(bb * CHUNK, (bb + 1) * CHUNK), slice(gi * GROUP, (gi + 1) * GROUP))
           for bb, gi in chains]

    def each(fn, *lists):
        return [fn(*args) for args in zip(*lists)]

    def cut(x):
        return [x[rs, sl] for rs, sl in rsl]

    r_g, v_g, kp_g, a_g, kk_g = cut(r), cut(v), cut(kp), cut(a), cut(kk)
    norm = each(lambda x: jnp.sqrt(head_sum(x * x)), kk_g)
    kkn = each(lambda x, n: x / jnp.maximum(n, 1e-12), kk_g, norm)
    ka = each(lambda x, y: x * y, kkn, a_g)
    a_bar = each(lambda x, e: -x * e, kkn, cut(e_pos_x))
    r_bar = each(lambda x, e: x * e, r_g, cut(e_pos))
    b_til = each(lambda x, e: x * e, ka, cut(e_neg))
    k_til = each(lambda x, e: x * e, kp_g, cut(e_neg))
    b_hat = each(lambda x, e: x * e, ka, cut(e_end))
    k_hat = each(lambda x, e: x * e, kp_g, cut(e_end))

    def score_fn(ab, rb, bt, kt):
        lhs = jnp.concatenate([ab, rb], axis=0)
        rhs = jnp.concatenate([tile4(bt), tile4(kt)], axis=0)
        return lax.dot_general(lhs.astype(BF16), rhs.astype(BF16),
                               (((1,), (1,)), ((), ())), preferred_element_type=F32)

    scores = each(score_fn, a_bar, r_bar, b_til, k_til)
    a_ab = each(lambda x: (x[0:CHUNK, 0:GROUP] * c_strict).astype(BF16), scores)
    a_ak = each(lambda x: (x[0:CHUNK, GROUP:] * c_strict).astype(BF16), scores)
    a_rb = each(lambda x: (x[CHUNK:, 0:GROUP] * c_incl).astype(BF16), scores)
    a_rk = each(lambda x: (x[CHUNK:, GROUP:] * c_incl).astype(BF16), scores)

    a_full = each(lambda x: jnp.concatenate([x] * HEADS_PER_GROUP, axis=0) * ones_bd, a_ab)
    n1 = each(lambda x: x * m_b8, a_full)
    n2 = each(lambda x: _bdot(x, x).astype(BF16), n1)
    inv = each(lambda x, y: eye + x + _bdot(eye + x, y), n1, n2)
    n4 = each(lambda x: _bdot(x, x).astype(BF16), n2)
    inv = each(lambda x, y: x + _bdot(x, y), inv, n4)
    for m_off in (m_l16, m_l32, m_l64):
        tmp = each(lambda x, af: _bdot(x, af * m_off).astype(BF16), inv, a_full)
        inv = each(lambda x, t: x + _bdot(t, x), inv, tmp)

    v_bd = each(lambda x: tile4(x).astype(BF16), v_g)
    akv = each(_bdot, a_ak, v_bd)
    wu = each(lambda t, ab, x: _bdot(t, jnp.concatenate([tile4(ab), tile4(x)], axis=1)).astype(BF16),
              inv, a_bar, akv)
    w_cat = each(lambda x: fold4(x[:, 0:GROUP]), wu)
    u_cat = each(lambda x: fold4(x[:, GROUP:]), wu)
    ry = each(_bdot, a_rb, wu)
    r_hat = each(lambda x, y: x + y[:, 0:GROUP], r_bar, ry)
    y_hat = each(lambda y, ak, vb: y[:, GROUP:] + _bdot(ak, vb), ry, a_rk, v_bd)

    state = [st_ref[bb, gi] for bb, gi in chains]
    y = each(lambda rh, st, yh: lax.dot_general(
        rh.astype(BF16), st.astype(BF16), (((1,), (1,)), ((), ())),
        preferred_element_type=F32) + yh, r_hat, state, y_hat)

    def trans_fn(bh, kh, wc, uc, vg):
        vb = vg.astype(BF16)
        lhs_t = jnp.concatenate(
            [jnp.concatenate([wc, uc], axis=1),
             jnp.concatenate([jnp.zeros_like(vb), vb], axis=1)], axis=0)
        rhs_t = jnp.concatenate([bh, kh], axis=0)
        return lax.dot_general(lhs_t, rhs_t.astype(BF16),
                               (((0,), (0,)), ((), ())), preferred_element_type=F32)

    mc = each(trans_fn, b_hat, k_hat, w_cat, u_cat, v_g)
    for (bb, gi), (_, sl), mci, st in zip(chains, rsl, mc, state):
        st_ref[bb, gi] = (st * jnp.exp(cum_last[bb][:, sl])
                          + _bdot(st, mci[0:GROUP, :] * m_bd) + mci[GROUP:, :] * m_bd)

    inv_n = 1.0 / RWKV_HEAD_DIM
    mean = each(lambda x: head_sum2(x) * inv_n, y)
    dlt = each(lambda x, m: x - m, y, mean)
    var = each(lambda x: head_sum(x * x) * inv_n, dlt)
    bonus = each(lambda rg, kg, vg, rsl_: head_sum(rg * kg * r_k[:, rsl_[1]]) * vg,
                 r_g, kp_g, v_g, rsl)
    for (bb, gi), (rs, sl), d, vr, bo in zip(chains, rsl, dlt, var, bonus):
        yn = d * lax.rsqrt(vr + GN_EPS) * ln_g[:, sl] + ln_b[:, sl]
        o_ref[bb, :, sl] = ((yn + bo) * gate[rs, sl]).astype(o_ref.dtype)


def _rwkv_call(p3, vecs, wd, wa, wg, nb):
    b, s, _ = p3.shape
    w = RWKV_WIDTH
    mk, mc, tri = _rwkv_masks(nb)
    mk = jnp.asarray(mk)
    mc = jnp.asarray(mc)
    tri = jnp.asarray(tri, dtype=BF16)
    ones_bd = mk[0].astype(BF16)
    const2 = lambda bi, ci: (0, 0)
    const3 = lambda bi, ci: (0, 0, 0)
    return pl.pallas_call(
        functools.partial(_rwkv_kernel, nb=nb),
        grid=(b // nb, s // CHUNK),
        in_specs=[pl.BlockSpec((nb, CHUNK, w), lambda bi, ci: (bi, ci, 0)),
                  pl.BlockSpec((nb, CHUNK, w), lambda bi, ci: (bi, ci, 1)),
                  pl.BlockSpec((nb, CHUNK, w), lambda bi, ci: (bi, ci, 2)),
                  pl.BlockSpec((nb, CHUNK, 4 * LANES), lambda bi, ci: (bi, ci, 3 * w // (4 * LANES))),
                  pl.BlockSpec(vecs.shape, const2),
                  pl.BlockSpec(wd.shape, const2),
                  pl.BlockSpec(wa.shape, const2),
                  pl.BlockSpec(wg.shape, const2),
                  pl.BlockSpec(mk.shape, const3),
                  pl.BlockSpec(mc.shape, const3),
                  pl.BlockSpec(tri.shape, const2),
                  pl.BlockSpec(ones_bd.shape, const2)],
        out_specs=pl.BlockSpec((nb, CHUNK, w), lambda bi, ci: (bi, ci, 0)),
        out_shape=jax.ShapeDtypeStruct((b, s, w), BF16),
        scratch_shapes=[pltpu.VMEM((nb, N_GROUPS, GROUP, GROUP), F32)],
        compiler_params=pltpu.CompilerParams(dimension_semantics=("arbitrary", "arbitrary"),
                                             vmem_limit_bytes=VMEM_LIMIT),
        name="rwkv",
    )(p3, p3, p3, p3, vecs, wd, wa, wg, mk, mc, tri, ones_bd)


def _mlaproj_kernel(pq_ref, pkv_ref, pos_ref, invf_ref, gq_ref, gkv_ref, wq_ref, wkv_ref,
                    q_ref, k_ref, v_ref):
    q = _bdot(_rms(pq_ref[...], gq_ref[...]), wq_ref[...])
    pkv = pkv_ref[...]
    kv = _bdot(_rms(pkv[:, 0:KV_LORA_RANK], gkv_ref[...]), wkv_ref[...])
    ang = pos_ref[...].astype(F32) * invf_ref[...]
    lane = lax.broadcasted_iota(jnp.int32, ang.shape, 1)
    half = QK_ROPE_DIM // 2
    cos_f = jnp.where(lane < QK_ROPE_DIM, jnp.cos(ang), 0.0)
    sin = jnp.sin(ang)
    sin_f = jnp.where(lane < half, -sin, jnp.where(lane < QK_ROPE_DIM, sin, 0.0))
    k_rot = pkv[:, KV_LORA_RANK:KV_LORA_RANK + LANES] * cos_f + pkv[:, KV_LORA_RANK + LANES:] * sin_f
    scale = (QK_NOPE_DIM + QK_ROPE_DIM) ** -0.5 * 1.4426950408889634
    swap0 = MLA_HEADS * QK_PAD
    ones_col = (lane == 0).astype(F32)
    for h in range(MLA_HEADS):
        q_nope = q[:, h * QK_PAD:h * QK_PAD + LANES]
        q_rot = (q[:, h * QK_PAD + LANES:(h + 1) * QK_PAD] * cos_f
                 + q[:, swap0 + h * LANES:swap0 + (h + 1) * LANES] * sin_f)
        q_ref[0, h] = (jnp.concatenate([q_nope, q_rot], axis=1) * scale).astype(q_ref.dtype)
        k_ref[0, h] = jnp.concatenate([kv[:, h * LANES:(h + 1) * LANES], k_rot], axis=1).astype(k_ref.dtype)
        v_ref[0, h] = jnp.concatenate(
            [kv[:, (MLA_HEADS + h) * LANES:(MLA_HEADS + h + 1) * LANES], ones_col],
            axis=1).astype(v_ref.dtype)


def _mlaproj_call(p2, pos2, invf, gq, gkv, wq, wkv, batch, seq, tm):
    t = p2.shape[0]
    tps = seq // tm
    blk = 4 * LANES
    q_blk = (IN_COLS_PAD - 2 * blk) // blk
    hd = MLA_HEADS
    out_map = lambda i: (i // tps, 0, i % tps, 0)
    const = lambda i: (0, 0)
    return pl.pallas_call(
        _mlaproj_kernel,
        grid=(t // tm,),
        in_specs=[pl.BlockSpec((tm, blk), lambda i: (i, q_blk)),
                  pl.BlockSpec((tm, blk), lambda i: (i, q_blk + 1)),
                  pl.BlockSpec((tm, 1), lambda i: (i, 0)),
                  pl.BlockSpec(invf.shape, const),
                  pl.BlockSpec(gq.shape, const),
                  pl.BlockSpec(gkv.shape, const),
                  pl.BlockSpec(wq.shape, const),
                  pl.BlockSpec(wkv.shape, const)],
        out_specs=[pl.BlockSpec((1, hd, tm, QK_PAD), out_map),
                   pl.BlockSpec((1, hd, tm, QK_PAD), out_map),
                   pl.BlockSpec((1, hd, tm, V_PAD), out_map)],
        out_shape=[jax.ShapeDtypeStruct((batch, hd, seq, QK_PAD), BF16),
                   jax.ShapeDtypeStruct((batch, hd, seq, QK_PAD), BF16),
                   jax.ShapeDtypeStruct((batch, hd, seq, V_PAD), BF16)],
        compiler_params=pltpu.CompilerParams(dimension_semantics=("arbitrary",),
                                             vmem_limit_bytes=VMEM_LIMIT),
        name="mlaproj",
    )(p2, p2, pos2, invf, gq, gkv, wq, wkv)


ATTN_ROW_PARTS = 4
ATTN_SM_ROWS = 32


def _attn_kernel(q_ref, k_ref, v_ref, o_ref, s_ref, p_ref, m_ref, a_ref, acc_ref, *, tq):
    i = pl.program_id(2)
    rp = tq // ATTN_ROW_PARTS
    m_ref[...] = jnp.full(m_ref.shape, -jnp.inf, F32)
    acc_ref[...] = jnp.zeros_like(acc_ref)

    def block(start, on_diagonal):
        def keys(part):
            return (part + 1) * rp if on_diagonal else tq

        def scores(part):
            rows = slice(part * rp, (part + 1) * rp)
            nk = keys(part)
            s_ref[rows, 0:nk] = lax.dot_general(
                q_ref[0, 0, rows, :], k_ref[0, 0, pl.ds(start, nk), :],
                (((1,), (1,)), ((), ())), preferred_element_type=F32)

        def softmax(part):
            nk = keys(part)
            for c in range(rp // ATTN_SM_ROWS):
                r0 = part * rp + c * ATTN_SM_ROWS
                rows = slice(r0, r0 + ATTN_SM_ROWS)
                sc = s_ref[rows, 0:nk]
                if on_diagonal:
                    row = (r0 + lax.broadcasted_iota(jnp.int32, sc.shape, 0)) // CHUNK
                    col = lax.broadcasted_iota(jnp.int32, sc.shape, 1) // CHUNK
                    sc = jnp.where(row >= col, sc, -jnp.inf)
                m_old = m_ref[rows, :]
                m_new = jnp.maximum(m_old, jnp.max(sc, axis=-1, keepdims=True))
                m_ref[rows, :] = m_new
                a_ref[rows, :] = jnp.exp2(m_old - m_new)
                p_ref[rows, 0:nk] = jnp.exp2(sc - m_new).astype(BF16)

        def weighted_values(part):
            rows = slice(part * rp, (part + 1) * rp)
            nk = keys(part)
            acc_ref[rows, :] = a_ref[rows, :] * acc_ref[rows, :] + jnp.dot(
                p_ref[rows, 0:nk], v_ref[0, 0, pl.ds(start, nk), :], preferred_element_type=F32)

        scores(0)
        for part in range(ATTN_ROW_PARTS):
            if part + 1 < ATTN_ROW_PARTS:
                scores(part + 1)
            softmax(part)
            weighted_values(part)

    def body(j, carry):
        block(pl.multiple_of(j * tq, tq), False)
        return carry

    lax.fori_loop(0, i, body, 0)
    block(pl.multiple_of(i * tq, tq), True)
    acc = acc_ref[...]
    o_ref[0] = (acc[:, 0:V_HEAD_DIM] / acc[:, V_HEAD_DIM:V_HEAD_DIM + 1]).astype(o_ref.dtype)


def _attn_call(q4, k4, v4, tq):
    b, h, s, _ = q4.shape
    return pl.pallas_call(
        functools.partial(_attn_kernel, tq=tq),
        grid=(b, h, s // tq),
        in_specs=[pl.BlockSpec((1, 1, tq, QK_PAD), lambda bi, hi, i: (bi, hi, i, 0)),
                  pl.BlockSpec((1, 1, s, QK_PAD), lambda bi, hi, i: (bi, hi, 0, 0)),
                  pl.BlockSpec((1, 1, s, V_PAD), lambda bi, hi, i: (bi, hi, 0, 0))],
        out_specs=pl.BlockSpec((1, tq, V_HEAD_DIM), lambda bi, hi, i: (bi, i, hi)),
        out_shape=jax.ShapeDtypeStruct((b, s, h * V_HEAD_DIM), BF16),
        scratch_shapes=[pltpu.VMEM((tq, tq), F32), pltpu.VMEM((tq, tq), BF16),
                        pltpu.VMEM((tq, 1), F32), pltpu.VMEM((tq, 1), F32),
                        pltpu.VMEM((tq, V_PAD), F32)],
        compiler_params=pltpu.CompilerParams(
            dimension_semantics=("arbitrary", "arbitrary", "arbitrary"),
            vmem_limit_bytes=VMEM_LIMIT),
        name="attn",
    )(q4, k4, v4)


NORM_ROWS = 32


def _outproj_kernel(yr_ref, ym_ref, x_ref, mod_ref, g_ref, gffn_ref, w_ref, o_ref, h_ref):
    half = RWKV_WIDTH
    o = (jnp.dot(yr_ref[...], w_ref[0:half, :], preferred_element_type=F32)
         + jnp.dot(ym_ref[...], w_ref[half:, :], preferred_element_type=F32))
    x_mid = x_ref[...] + mod_ref[0, 2:3, :] * _rms(o, g_ref[...])
    o_ref[...] = x_mid
    h = _rms(x_mid, gffn_ref[...]) * (1.0 + mod_ref[0, 4:5, :]) + mod_ref[0, 3:4, :]
    h_ref[...] = h.astype(h_ref.dtype)


def _outproj_call(yr, ym, x2, mod3, g_post, g_pre_ffn, w_out, seq, tm):
    t, d = x2.shape
    tps = seq // tm
    return pl.pallas_call(
        _outproj_kernel,
        grid=(t // tm,),
        in_specs=[pl.BlockSpec((tm, RWKV_WIDTH), lambda i: (i, 0)),
                  pl.BlockSpec((tm, MLA_WIDTH), lambda i: (i, 0)),
                  pl.BlockSpec((tm, d), lambda i: (i, 0)),
                  pl.BlockSpec((1, N_MOD, d), lambda i: (i // tps, 0, 0)),
                  pl.BlockSpec((1, d), lambda i: (0, 0)),
                  pl.BlockSpec((1, d), lambda i: (0, 0)),
                  pl.BlockSpec(w_out.shape, lambda i: (0, 0))],
        out_specs=[pl.BlockSpec((tm, d), lambda i: (i, 0)),
                   pl.BlockSpec((tm, d), lambda i: (i, 0))],
        out_shape=[jax.ShapeDtypeStruct((t, d), F32), jax.ShapeDtypeStruct((t, d), BF16)],
        compiler_params=pltpu.CompilerParams(dimension_semantics=("arbitrary",),
                                             vmem_limit_bytes=VMEM_LIMIT),
        name="outproj",
    )(yr, ym, x2, mod3, g_post, g_pre_ffn, w_out)


FFN_ROWS = 64
FFN_COL_SPLIT = 1


def _gelu_tanh(x):
    c = 0.7978845608028654
    return 0.5 * x * (1.0 + jnp.tanh(c * (x + 0.044715 * (x * x * x))))


def _ffn_kernel(h_ref, hh_ref, x_ref, mod_ref, wg_ref, wv_ref, cwg_ref, cwv_ref,
                cbg_ref, cbv_ref, wd_ref, gpost_ref, o_ref, hcat_ref, acc_ref, ug_ref, uv_ref, act_ref,
                *, tiles_per_seq):
    i = pl.program_id(0)
    j = pl.program_id(1)
    tm, tf = act_ref.shape
    w = tf // FFN_COL_SPLIT

    @pl.when(j == 0)
    def _prologue():
        keep = (i % tiles_per_seq) != 0
        hcat_ref[0:HALO, :] = jnp.where(keep, hh_ref[...], jnp.zeros_like(hh_ref))
        hcat_ref[HALO:, :] = h_ref[...]
        acc_ref[...] = jnp.zeros_like(acc_ref)

    hb = hcat_ref[...]

    def up(c):
        cs = slice(c * w, (c + 1) * w)
        ug_ref[:, cs] = jnp.dot(hb, wg_ref[:, cs], preferred_element_type=F32)
        uv_ref[:, cs] = jnp.dot(hb, wv_ref[:, cs], preferred_element_type=F32)

    def conv(u_ref, r0, cs, cw_ref, cb_ref):
        return (cb_ref[:, cs] + cw_ref[2:3, cs] * u_ref[pl.ds(r0, FFN_ROWS), cs]
                + cw_ref[1:2, cs] * u_ref[pl.ds(r0 - 1, FFN_ROWS), cs]
                + cw_ref[0:1, cs] * u_ref[pl.ds(r0 - 2, FFN_ROWS), cs])

    def activate(c):
        cs = slice(c * w, (c + 1) * w)
        for rc in range(tm // FFN_ROWS):
            r0 = HALO + rc * FFN_ROWS
            act = (_gelu_tanh(conv(ug_ref, r0, cs, cwg_ref, cbg_ref))
                   * conv(uv_ref, r0, cs, cwv_ref, cbv_ref))
            act_ref[rc * FFN_ROWS:(rc + 1) * FFN_ROWS, cs] = act.astype(BF16)

    def down(c):
        cs = slice(c * w, (c + 1) * w)
        return jnp.dot(act_ref[:, cs], wd_ref[cs, :], preferred_element_type=F32)

    up(0)
    total = None
    for c in range(FFN_COL_SPLIT):
        if c + 1 < FFN_COL_SPLIT:
            up(c + 1)
        activate(c)
        part = down(c)
        total = part if total is None else total + part
    acc_ref[...] += total

    @pl.when(j == pl.num_programs(1) - 1)
    def _epilogue():
        for c in range(tm // NORM_ROWS):
            rows = slice(c * NORM_ROWS, (c + 1) * NORM_ROWS)
            o_ref[rows, :] = x_ref[rows, :] + mod_ref[0, 5:6, :] * _rms(acc_ref[rows, :], gpost_ref[...])


def _ffn_call(h2, x2, mod3, w_up, conv_w, conv_b, w_down, g_post, seq, tm, tf):
    t, d = x2.shape
    tps = seq // tm
    hb = tm // HALO
    nf = D_FF // tf
    return pl.pallas_call(
        functools.partial(_ffn_kernel, tiles_per_seq=tps),
        grid=(t // tm, nf),
        in_specs=[pl.BlockSpec((tm, d), lambda i, j: (i, 0)),
                  pl.BlockSpec((HALO, d), lambda i, j: (jnp.maximum(i * hb - 1, 0), 0)),
                  pl.BlockSpec((tm, d), lambda i, j: (i, 0)),
                  pl.BlockSpec((1, N_MOD, d), lambda i, j: (i // tps, 0, 0)),
                  pl.BlockSpec((d, tf), lambda i, j: (0, j)),
                  pl.BlockSpec((d, tf), lambda i, j: (0, nf + j)),
                  pl.BlockSpec((3, tf), lambda i, j: (0, j)),
                  pl.BlockSpec((3, tf), lambda i, j: (0, nf + j)),
                  pl.BlockSpec((1, tf), lambda i, j: (0, j)),
                  pl.BlockSpec((1, tf), lambda i, j: (0, nf + j)),
                  pl.BlockSpec((tf, d), lambda i, j: (j, 0)),
                  pl.BlockSpec((1, d), lambda i, j: (0, 0))],
        out_specs=pl.BlockSpec((tm, d), lambda i, j: (i, 0)),
        out_shape=jax.ShapeDtypeStruct((t, d), F32),
        scratch_shapes=[pltpu.VMEM((tm + HALO, d), BF16), pltpu.VMEM((tm, d), F32),
                        pltpu.VMEM((tm + HALO, tf), F32), pltpu.VMEM((tm + HALO, tf), F32),
                        pltpu.VMEM((tm, tf), BF16)],
        compiler_params=pltpu.CompilerParams(dimension_semantics=("arbitrary", "arbitrary"),
                                             vmem_limit_bytes=VMEM_LIMIT),
        name="ffn",
    )(h2, h2, x2, mod3, w_up, w_up, conv_w, conv_w, conv_b, conv_b, w_down, g_post)


def _pad_cols(w, n):
    return jnp.pad(w, ((0, 0), (0, n - w.shape[1])))


def _layout_w_in(w_in, mu_shift):
    w3 = 3 * RWKV_WIDTH
    o_wd = w3
    o_ad = o_wd + DECAY_LORA
    o_gd = o_ad + ICLR_LORA
    o_q = o_gd + GATE_LORA
    o_kv = o_q + Q_LORA_RANK
    o_kr = o_kv + KV_LORA_RANK
    half = QK_ROPE_DIM // 2

    def lay(m):
        kr = m[:, o_kr:o_kr + QK_ROPE_DIM]
        return jnp.concatenate([
            m[:, 0:w3],
            _pad_cols(m[:, o_wd:o_ad], LANES),
            _pad_cols(m[:, o_ad:o_gd], LANES),
            m[:, o_gd:o_q],
            m[:, o_q:o_kv],
            m[:, o_kv:o_kr],
            _pad_cols(kr, LANES),
            _pad_cols(jnp.concatenate([kr[:, half:], kr[:, :half]], axis=1), LANES),
        ], axis=1)

    mu_full = jnp.concatenate([mu_shift, jnp.zeros((w_in.shape[1] - mu_shift.shape[0],), F32)])
    w_p = lay(w_in).astype(BF16)
    mu_p = lay(mu_full[None, :])
    return w_p, mu_p


def _layout_w_q(w_q_up):
    dn, dr = QK_NOPE_DIM, QK_ROPE_DIM
    half = dr // 2
    w = w_q_up.reshape(Q_LORA_RANK, MLA_HEADS, dn + dr)
    nope, u1, u2 = w[..., :dn], w[..., dn:dn + half], w[..., dn + half:]
    z = jnp.zeros(u1.shape[:-1] + (LANES - dr,), w.dtype)
    main = jnp.concatenate([nope, u1, u2, z], axis=-1).reshape(Q_LORA_RANK, MLA_HEADS * QK_PAD)
    swap = jnp.concatenate([u2, u1, z], axis=-1).reshape(Q_LORA_RANK, MLA_HEADS * LANES)
    return jnp.concatenate([main, swap], axis=1).astype(BF16)


def _layout_w_kv(w_kv_up):
    w = w_kv_up.reshape(KV_LORA_RANK, MLA_HEADS, QK_NOPE_DIM + V_HEAD_DIM)
    kn = w[..., :QK_NOPE_DIM].reshape(KV_LORA_RANK, MLA_HEADS * QK_NOPE_DIM)
    vv = w[..., QK_NOPE_DIM:].reshape(KV_LORA_RANK, MLA_HEADS * V_HEAD_DIM)
    return jnp.concatenate([kn, vv], axis=1).astype(BF16)


def _pad_rows(w, n):
    return jnp.pad(w, ((0, n - w.shape[0]), (0, 0)))


def _block(x, c, positions, w_mod, b_mod, g_pre_mix, w_in, mu_shift, w0, w_decay_up, a0,
           w_iclr_up, w_gate_up, k_k, k_a, r_k, ln_x_g, ln_x_b, q_norm_g, w_q_up, kv_norm_g,
           w_kv_up, w_out, g_post_mix, g_pre_ffn, w_ffn_up, conv_w, conv_b, w_ffn_down,
           g_post_ffn, *, tm_in, tn_in, nb_rwkv, tm_mla, tq, tm_out, tm_ffn, tf):
    b, s, d = x.shape
    t = b * s
    x2 = x.reshape(t, d)

    c_pad = jnp.pad(c, ((0, 8 - b % 8 if b % 8 else 0), (0, 0)))
    mod = _mod_call(c_pad, w_mod, b_mod[None, :])[:b]
    mod3 = mod.reshape(b, N_MOD, d)

    w_in_p, mu_p = _layout_w_in(w_in, mu_shift)
    p2 = _inproj_call(x2, mod3, g_pre_mix[None, :], w_in_p, mu_p, s, tm_in, tn_in)

    vecs = jnp.stack([w0, a0, k_k, k_a, r_k.reshape(-1), ln_x_g, ln_x_b, jnp.zeros_like(w0)])
    y_rwkv = _rwkv_call(p2.reshape(b, s, IN_COLS_PAD), vecs,
                        _pad_rows(w_decay_up, LANES).astype(BF16),
                        _pad_rows(w_iclr_up, LANES).astype(BF16),
                        w_gate_up.astype(BF16), nb_rwkv)

    half = QK_ROPE_DIM // 2
    inv_freq = ROPE_THETA ** (-jnp.arange(0, QK_ROPE_DIM, 2, dtype=F32) / QK_ROPE_DIM)
    invf = jnp.concatenate([inv_freq, inv_freq, jnp.zeros((LANES - 2 * half,), F32)])[None, :]
    q4, k4, v4 = _mlaproj_call(p2, positions.reshape(t, 1), invf, q_norm_g[None, :],
                               kv_norm_g[None, :], _layout_w_q(w_q_up), _layout_w_kv(w_kv_up),
                               b, s, tm_mla)
    y_mla = _attn_call(q4, k4, v4, tq)

    x_mid, h_ffn = _outproj_call(y_rwkv.reshape(t, RWKV_WIDTH), y_mla.reshape(t, MLA_WIDTH), x2,
                                 mod3, g_post_mix[None, :], g_pre_ffn[None, :],
                                 w_out.astype(BF16), s, tm_out)
    out = _ffn_call(h_ffn, x_mid, mod3, w_ffn_up.astype(BF16), conv_w, conv_b[None, :],
                    w_ffn_down.astype(BF16), g_post_ffn[None, :], s, tm_ffn, tf)
    return out.reshape(b, s, d)


def kernel(x, c, positions, w_mod, b_mod, g_pre_mix, w_in, mu_shift, w0, w_decay_up, a0, w_iclr_up, w_gate_up, k_k, k_a, r_k, ln_x_g, ln_x_b, q_norm_g, w_q_up, kv_norm_g, w_kv_up, w_out, g_post_mix, g_pre_ffn, w_ffn_up, conv_w, conv_b, w_ffn_down, g_post_ffn):
    args = (x, c, positions, w_mod, b_mod, g_pre_mix, w_in, mu_shift, w0, w_decay_up, a0,
            w_iclr_up, w_gate_up, k_k, k_a, r_k, ln_x_g, ln_x_b, q_norm_g, w_q_up, kv_norm_g,
            w_kv_up, w_out, g_post_mix, g_pre_ffn, w_ffn_up, conv_w, conv_b, w_ffn_down,
            g_post_ffn)
    for l in range(w_mod.shape[0]):
        layer = [a[l] for a in args[3:]]
        x = _block(x, c, positions, *layer, tm_in=512, tn_in=1536, nb_rwkv=2, tm_mla=512, tq=1024,
                   tm_out=512, tm_ffn=512, tf=512)
    return x
```

```python
import functools

import numpy as np
import jax
import jax.numpy as jnp
from jax import lax
from jax.experimental import pallas as pl
from jax.experimental.pallas import tpu as pltpu

F32 = jnp.float32
BF16 = jnp.bfloat16

D_MODEL = 2048
CHUNK = 64
RWKV_HEAD_DIM = 64
RWKV_WIDTH = 1024
RWKV_HEADS = 16
DECAY_LORA = 96
ICLR_LORA = 96
GATE_LORA = 256
GN_EPS = 64e-5
QK_NOPE_DIM = 128
QK_ROPE_DIM = 64
V_HEAD_DIM = 128
MLA_WIDTH = 1024
MLA_HEADS = 8
Q_LORA_RANK = 512
KV_LORA_RANK = 256
ROPE_THETA = 10000.0
D_FF = 5632
NORM_EPS = 1e-6
N_MOD = 6

LANES = 128
HALO = 16
GROUP = 256
HEADS_PER_GROUP = GROUP // RWKV_HEAD_DIM
N_GROUPS = RWKV_WIDTH // GROUP
QK_PAD = 256
V_PAD = 256
IN_COLS_PAD = 4608
VMEM_LIMIT = 56 * 1024 * 1024


def _bdot(a, b):
    return jnp.dot(a.astype(BF16), b.astype(BF16), preferred_element_type=F32)


def _split2(x):
    hi = x.astype(BF16)
    lo = (x - hi.astype(F32)).astype(BF16)
    return hi, lo


def _sigmoid(x):
    return 1.0 / (1.0 + jnp.exp(-x))


def _rms(xv, g):
    ms = jnp.mean(xv * xv, axis=-1, keepdims=True)
    return xv * lax.rsqrt(ms + NORM_EPS) * g


def _mod_kernel(c_ref, w_ref, b_ref, o_ref):
    c = c_ref[...]
    s = c * _sigmoid(c)
    o_ref[...] = _bdot(s, w_ref[...]) + b_ref[...]


def _mod_call(c_pad, w_mod, b_mod, tn=1024):
    rows, d = c_pad.shape
    n = w_mod.shape[1]
    return pl.pallas_call(
        _mod_kernel,
        grid=(n // tn,),
        in_specs=[pl.BlockSpec((rows, d), lambda j: (0, 0)),
                  pl.BlockSpec((d, tn), lambda j: (0, j)),
                  pl.BlockSpec((1, tn), lambda j: (0, j))],
        out_specs=pl.BlockSpec((rows, tn), lambda j: (0, j)),
        out_shape=jax.ShapeDtypeStruct((rows, n), F32),
        compiler_params=pltpu.CompilerParams(dimension_semantics=("arbitrary",),
                                             vmem_limit_bytes=VMEM_LIMIT),
        name="mod",
    )(c_pad, w_mod, b_mod)


def _inproj_kernel(x_ref, xh_ref, mod_ref, g_ref, w_ref, mu_ref, o_ref, *, tiles_per_seq):
    i = pl.program_id(1)
    first = (i % tiles_per_seq) == 0
    shift = mod_ref[0, 0:1, :]
    scale = mod_ref[0, 1:2, :]
    g = g_ref[...]
    h = _rms(x_ref[...], g) * (1.0 + scale) + shift
    hh = _rms(xh_ref[...], g) * (1.0 + scale) + shift
    hh = jnp.where(first, 0.0, hh)
    hcat = jnp.concatenate([hh, h], axis=0).astype(BF16)
    p = jnp.dot(hcat, w_ref[...], preferred_element_type=F32)
    prev = pltpu.roll(p, 1, axis=0)
    out = p + (prev - p) * mu_ref[...]
    o_ref[...] = out[HALO:, :]


def _inproj_call(x2, mod3, g_pre, w_in_p, mu_p, seq, tm, tn):
    t, d = x2.shape
    n = w_in_p.shape[1]
    tps = seq // tm
    hb = tm // HALO
    return pl.pallas_call(
        functools.partial(_inproj_kernel, tiles_per_seq=tps),
        grid=(n // tn, t // tm),
        in_specs=[pl.BlockSpec((tm, d), lambda j, i: (i, 0)),
                  pl.BlockSpec((HALO, d), lambda j, i: (jnp.maximum(i * hb - 1, 0), 0)),
                  pl.BlockSpec((1, N_MOD, d), lambda j, i: (i // tps, 0, 0)),
                  pl.BlockSpec((1, d), lambda j, i: (0, 0)),
                  pl.BlockSpec((d, tn), lambda j, i: (0, j)),
                  pl.BlockSpec((1, tn), lambda j, i: (0, j))],
        out_specs=pl.BlockSpec((tm, tn), lambda j, i: (i, j)),
        out_shape=jax.ShapeDtypeStruct((t, n), F32),
        compiler_params=pltpu.CompilerParams(dimension_semantics=("arbitrary", "arbitrary"),
                                             vmem_limit_bytes=VMEM_LIMIT),
        name="inproj",
    )(x2, x2, mod3, g_pre, w_in_p, mu_p)


def _rwkv_masks(nb):
    i = np.arange(GROUP)[:, None]
    j = np.arange(GROUP)[None, :]
    n = RWKV_HEAD_DIM
    bd = (i // n) == (j // n)
    mk = bd.astype(np.float32)[None]
    t = np.arange(CHUNK)[:, None]
    sj = j % n
    mc = np.stack([
        sj < t,
        sj <= t,
        sj == t,
        (sj < t) & ((t // 8) == (sj // 8)),
        ((t // 16) == (sj // 16)) & ((t // 8) > (sj // 8)),
        ((t // 32) == (sj // 32)) & ((t // 16) > (sj // 16)),
        (t // 32) > (sj // 32),
    ]).astype(np.float32)
    ti = np.arange(nb * CHUNK)[:, None]
    tj = np.arange(nb * CHUNK)[None, :]
    tri = ((ti // CHUNK == tj // CHUNK) & (tj <= ti)).astype(np.float32)
    return mk, mc, tri


def _rwkv_kernel(r_ref, k_ref, v_ref, lo_ref, vec_ref, wd_ref, wa_ref, wg_ref,
                 mk_ref, mc_ref, tri_ref, e_ref, o_ref, st_ref, *, nb):
    @pl.when(pl.program_id(1) == 0)
    def _reset_state():
        st_ref[...] = jnp.zeros_like(st_ref)

    def rows(ref):
        return jnp.concatenate([ref[bb] for bb in range(nb)], axis=0)

    r = rows(r_ref)
    k = rows(k_ref)
    v = rows(v_ref)
    lo = rows(lo_ref)
    w0 = vec_ref[0:1, :]
    a0 = vec_ref[1:2, :]
    k_k = vec_ref[2:3, :]
    k_a = vec_ref[3:4, :]
    r_k = vec_ref[4:5, :]
    ln_g = vec_ref[5:6, :]
    ln_b = vec_ref[6:7, :]

    z = -(w0 + _bdot(jnp.tanh(lo[:, 0:LANES]), wd_ref[...]))
    softplus = jnp.maximum(z, 0.0) + jnp.log(1.0 + jnp.exp(-jnp.abs(z)))
    lw = -jnp.exp(-softplus - 0.5)
    a = _sigmoid(a0 + _bdot(lo[:, LANES:2 * LANES], wa_ref[...]))
    gate = _bdot(_sigmoid(lo[:, 2 * LANES:4 * LANES]), wg_ref[...])
    kk = k * k_k
    kp = k * (1.0 + (a - 1.0) * k_a)

    tri = tri_ref[...]
    l1 = lw.astype(BF16)
    rem = lw - l1.astype(F32)
    l2 = rem.astype(BF16)
    l3 = (rem - l2.astype(F32)).astype(BF16)
    cum = (jnp.dot(tri, l1, preferred_element_type=F32)
           + jnp.dot(tri, l2, preferred_element_type=F32)
           + jnp.dot(tri, l3, preferred_element_type=F32))
    cum_last = [cum[(bb + 1) * CHUNK - 1:(bb + 1) * CHUNK, :] for bb in range(nb)]
    cum_last_rows = jnp.concatenate(
        [jnp.broadcast_to(cl, (CHUNK, cl.shape[1])) for cl in cum_last], axis=0)
    e_pos = jnp.exp(cum)
    e_pos_x = jnp.exp(cum - lw)
    e_neg = jnp.exp(-cum)
    e_end = jnp.exp(cum_last_rows - cum)

    m_bd = mk_ref[0]
    c_strict = mc_ref[0]
    c_incl = mc_ref[1]
    c_eye = mc_ref[2]
    c_b8 = mc_ref[3].astype(BF16)
    c_levels = [mc_ref[lv].astype(BF16) for lv in (4, 5, 6)]
    ones_bd = e_ref[...]

    def head_sum(x):
        return jnp.dot(x.astype(BF16), ones_bd, preferred_element_type=F32)

    def head_sum2(x):
        hi, lo_ = _split2(x)
        return (jnp.dot(hi, ones_bd, preferred_element_type=F32)
                + jnp.dot(lo_, ones_bd, preferred_element_type=F32))

    def bd(x):
        return jnp.concatenate([x.astype(BF16)] * HEADS_PER_GROUP, axis=0) * ones_bd

    def fold4(x):
        return (x[0:CHUNK] + x[CHUNK:2 * CHUNK] + x[2 * CHUNK:3 * CHUNK] + x[3 * CHUNK:4 * CHUNK])

    chains = [(bb, gi) for bb in range(nb) for gi in range(N_GROUPS)]
    rsl = [(slice(bb * CHUNK, (bb + 1) * CHUNK), slice(gi * GROUP, (gi + 1) * GROUP))
           for bb, gi in chains]

    def each(fn, *lists):
        return [fn(*args) for args in zip(*lists)]

    def cut(x):
        return [x[rs, sl] for rs, sl in rsl]

    r_g, v_g, kp_g, a_g, kk_g = cut(r), cut(v), cut(kp), cut(a), cut(kk)
    norm = each(lambda x: jnp.sqrt(head_sum(x * x)), kk_g)
    kkn = each(lambda x, n: x / jnp.maximum(n, 1e-12), kk_g, norm)
    ka = each(lambda x, y: x * y, kkn, a_g)
    a_bar = each(lambda x, e: -x * e, kkn, cut(e_pos_x))
    r_bar = each(lambda x, e: x * e, r_g, cut(e_pos))
    b_til = each(lambda x, e: x * e, ka, cut(e_neg))
    k_til = each(lambda x, e: x * e, kp_g, cut(e_neg))
    b_hat = each(lambda x, e: x * e, ka, cut(e_end))
    k_hat = each(lambda x, e: x * e, kp_g, cut(e_end))

    def score_fn(ab, rb, bt, kt):
        lhs = jnp.concatenate([ab, rb], axis=0)
        rhs = jnp.concatenate([bd(bt), bd(kt)], axis=0)
        return lax.dot_general(lhs.astype(BF16), rhs,
                               (((1,), (1,)), ((), ())), preferred_element_type=F32)

    scores = each(score_fn, a_bar, r_bar, b_til, k_til)
    a_ab = each(lambda x: (x[0:CHUNK, 0:GROUP] * c_strict).astype(BF16), scores)
    a_ak = each(lambda x: (x[0:CHUNK, GROUP:] * c_strict).astype(BF16), scores)
    a_rb = each(lambda x: (x[CHUNK:, 0:GROUP] * c_incl).astype(BF16), scores)
    a_rk = each(lambda x: (x[CHUNK:, GROUP:] * c_incl).astype(BF16), scores)

    n1 = each(lambda x: x * c_b8, a_ab)
    n2 = each(lambda x: _bdot(x, bd(x)).astype(BF16), n1)
    inv = each(lambda x, y: c_eye + x + _bdot(c_eye + x, bd(y)), n1, n2)
    n4 = each(lambda x: _bdot(x, bd(x)).astype(BF16), n2)
    inv = each(lambda x, y: x + _bdot(x, bd(y)), inv, n4)
    for c_off in c_levels:
        tmp = each(lambda x, a: _bdot(x, bd(a * c_off)).astype(BF16), inv, a_ab)
        inv = each(lambda x, t: x + _bdot(t, bd(x)), inv, tmp)

    v_bd = each(bd, v_g)
    akv = each(_bdot, a_ak, v_bd)
    wu = each(lambda t, ab, x: _bdot(t, jnp.concatenate([bd(ab), bd(x)], axis=1)),
              inv, a_bar, akv)
    ry = each(lambda a, x: _bdot(a, jnp.concatenate([bd(x[:, 0:GROUP]), bd(x[:, GROUP:])], axis=1)),
              a_rb, wu)
    r_hat = each(lambda x, y: x + y[:, 0:GROUP], r_bar, ry)
    y_hat = each(lambda y, ak, vb: y[:, GROUP:] + _bdot(ak, vb), ry, a_rk, v_bd)

    state = [st_ref[bb, gi] for bb, gi in chains]
    y = each(lambda rh, st, yh: lax.dot_general(
        rh.astype(BF16), bd(st), (((1,), (1,)), ((), ())),
        preferred_element_type=F32) + yh, r_hat, state, y_hat)

    def trans_fn(bh, kh, x, vg):
        vb = vg.astype(BF16)
        lhs_t = jnp.concatenate(
            [x.astype(BF16), jnp.concatenate([jnp.zeros_like(vb), vb], axis=1)], axis=0)
        rhs_t = jnp.concatenate([bh, kh], axis=0)
        return lax.dot_general(lhs_t, rhs_t.astype(BF16),
                               (((0,), (0,)), ((), ())), preferred_element_type=F32)

    mc = each(trans_fn, b_hat, k_hat, wu, v_g)
    for (bb, gi), (_, sl), mci, st in zip(chains, rsl, mc, state):
        st_ref[bb, gi] = (st * jnp.exp(cum_last[bb][:, sl])
                          + _bdot(st, mci[0:GROUP, :] * m_bd) + fold4(mci[GROUP:, :] * m_bd))

    inv_n = 1.0 / RWKV_HEAD_DIM
    mean = each(lambda x: head_sum2(x) * inv_n, y)
    dlt = each(lambda x, m: x - m, y, mean)
    var = each(lambda x: head_sum(x * x) * inv_n, dlt)
    bonus = each(lambda rg, kg, vg, rsl_: head_sum(rg * kg * r_k[:, rsl_[1]]) * vg,
                 r_g, kp_g, v_g, rsl)
    for (bb, gi), (rs, sl), d, vr, bo in zip(chains, rsl, dlt, var, bonus):
        yn = d * lax.rsqrt(vr + GN_EPS) * ln_g[:, sl] + ln_b[:, sl]
        o_ref[bb, :, sl] = ((yn + bo) * gate[rs, sl]).astype(o_ref.dtype)


def _rwkv_call(p3, vecs, wd, wa, wg, nb):
    b, s, _ = p3.shape
    w = RWKV_WIDTH
    mk, mc, tri = _rwkv_masks(nb)
    mk = jnp.asarray(mk)
    mc = jnp.asarray(mc)
    tri = jnp.asarray(tri, dtype=BF16)
    ones_bd = mk[0].astype(BF16)
    const2 = lambda bi, ci: (0, 0)
    const3 = lambda bi, ci: (0, 0, 0)
    return pl.pallas_call(
        functools.partial(_rwkv_kernel, nb=nb),
        grid=(b // nb, s // CHUNK),
        in_specs=[pl.BlockSpec((nb, CHUNK, w), lambda bi, ci: (bi, ci, 0)),
                  pl.BlockSpec((nb, CHUNK, w), lambda bi, ci: (bi, ci, 1)),
                  pl.BlockSpec((nb, CHUNK, w), lambda bi, ci: (bi, ci, 2)),
                  pl.BlockSpec((nb, CHUNK, 4 * LANES), lambda bi, ci: (bi, ci, 3 * w // (4 * LANES))),
                  pl.BlockSpec(vecs.shape, const2),
                  pl.BlockSpec(wd.shape, const2),
                  pl.BlockSpec(wa.shape, const2),
                  pl.BlockSpec(wg.shape, const2),
                  pl.BlockSpec(mk.shape, const3),
                  pl.BlockSpec(mc.shape, const3),
                  pl.BlockSpec(tri.shape, const2),
                  pl.BlockSpec(ones_bd.shape, const2)],
        out_specs=pl.BlockSpec((nb, CHUNK, w), lambda bi, ci: (bi, ci, 0)),
        out_shape=jax.ShapeDtypeStruct((b, s, w), BF16),
        scratch_shapes=[pltpu.VMEM((nb, N_GROUPS, CHUNK, GROUP), F32)],
        compiler_params=pltpu.CompilerParams(dimension_semantics=("arbitrary", "arbitrary"),
                                             vmem_limit_bytes=VMEM_LIMIT),
        name="rwkv",
    )(p3, p3, p3, p3, vecs, wd, wa, wg, mk, mc, tri, ones_bd)


def _mlaproj_kernel(pq_ref, pkv_ref, pos_ref, invf_ref, gq_ref, gkv_ref, wq_ref, wkv_ref,
                    q_ref, k_ref, v_ref):
    q = _bdot(_rms(pq_ref[...], gq_ref[...]), wq_ref[...])
    pkv = pkv_ref[...]
    kv = _bdot(_rms(pkv[:, 0:KV_LORA_RANK], gkv_ref[...]), wkv_ref[...])
    ang = pos_ref[...].astype(F32) * invf_ref[...]
    lane = lax.broadcasted_iota(jnp.int32, ang.shape, 1)
    half = QK_ROPE_DIM // 2
    cos_f = jnp.where(lane < QK_ROPE_DIM, jnp.cos(ang), 0.0)
    sin = jnp.sin(ang)
    sin_f = jnp.where(lane < half, -sin, jnp.where(lane < QK_ROPE_DIM, sin, 0.0))
    k_rot = pkv[:, KV_LORA_RANK:KV_LORA_RANK + LANES] * cos_f + pkv[:, KV_LORA_RANK + LANES:] * sin_f
    scale = (QK_NOPE_DIM + QK_ROPE_DIM) ** -0.5 * 1.4426950408889634
    swap0 = MLA_HEADS * QK_PAD
    ones_col = (lane == 0).astype(F32)
    for h in range(MLA_HEADS):
        q_nope = q[:, h * QK_PAD:h * QK_PAD + LANES]
        q_rot = (q[:, h * QK_PAD + LANES:(h + 1) * QK_PAD] * cos_f
                 + q[:, swap0 + h * LANES:swap0 + (h + 1) * LANES] * sin_f)
        q_ref[0, h] = (jnp.concatenate([q_nope, q_rot], axis=1) * scale).astype(q_ref.dtype)
        k_ref[0, h] = jnp.concatenate([kv[:, h * LANES:(h + 1) * LANES], k_rot], axis=1).astype(k_ref.dtype)
        v_ref[0, h] = jnp.concatenate(
            [kv[:, (MLA_HEADS + h) * LANES:(MLA_HEADS + h + 1) * LANES], ones_col],
            axis=1).astype(v_ref.dtype)


def _mlaproj_call(p2, pos2, invf, gq, gkv, wq, wkv, batch, seq, tm):
    t = p2.shape[0]
    tps = seq // tm
    blk = 4 * LANES
    q_blk = (IN_COLS_PAD - 2 * blk) // blk
    hd = MLA_HEADS
    out_map = lambda i: (i // tps, 0, i % tps, 0)
    const = lambda i: (0, 0)
    return pl.pallas_call(
        _mlaproj_kernel,
        grid=(t // tm,),
        in_specs=[pl.BlockSpec((tm, blk), lambda i: (i, q_blk)),
                  pl.BlockSpec((tm, blk), lambda i: (i, q_blk + 1)),
                  pl.BlockSpec((tm, 1), lambda i: (i, 0)),
                  pl.BlockSpec(invf.shape, const),
                  pl.BlockSpec(gq.shape, const),
                  pl.BlockSpec(gkv.shape, const),
                  pl.BlockSpec(wq.shape, const),
                  pl.BlockSpec(wkv.shape, const)],
        out_specs=[pl.BlockSpec((1, hd, tm, QK_PAD), out_map),
                   pl.BlockSpec((1, hd, tm, QK_PAD), out_map),
                   pl.BlockSpec((1, hd, tm, V_PAD), out_map)],
        out_shape=[jax.ShapeDtypeStruct((batch, hd, seq, QK_PAD), BF16),
                   jax.ShapeDtypeStruct((batch, hd, seq, QK_PAD), BF16),
                   jax.ShapeDtypeStruct((batch, hd, seq, V_PAD), BF16)],
        compiler_params=pltpu.CompilerParams(dimension_semantics=("arbitrary",),
                                             vmem_limit_bytes=VMEM_LIMIT),
        name="mlaproj",
    )(p2, p2, pos2, invf, gq, gkv, wq, wkv)


ATTN_ROW_PARTS = 4
ATTN_SM_ROWS = 32


def _attn_kernel(q_ref, k_ref, v_ref, o_ref, s_ref, p_ref, m_ref, a_ref, acc_ref, *, tq):
    i = pl.program_id(2)
    rp = tq // ATTN_ROW_PARTS
    m_ref[...] = jnp.full(m_ref.shape, -jnp.inf, F32)
    acc_ref[...] = jnp.zeros_like(acc_ref)

    def scores(start, part, nk):
        rows = slice(part * rp, (part + 1) * rp)
        s_ref[rows, 0:nk] = lax.dot_general(
            q_ref[0, 0, rows, :], k_ref[0, 0, pl.ds(start, nk), :],
            (((1,), (1,)), ((), ())), preferred_element_type=F32)

    def block(start, next_start, on_diagonal):
        def keys(part):
            return (part + 1) * rp if on_diagonal else tq

        def softmax(part):
            nk = keys(part)
            for c in range(rp // ATTN_SM_ROWS):
                r0 = part * rp + c * ATTN_SM_ROWS
                rows = slice(r0, r0 + ATTN_SM_ROWS)
                sc = s_ref[rows, 0:nk]
                if on_diagonal:
                    row = (r0 + lax.broadcasted_iota(jnp.int32, sc.shape, 0)) // CHUNK
                    col = lax.broadcasted_iota(jnp.int32, sc.shape, 1) // CHUNK
                    sc = jnp.where(row >= col, sc, -jnp.inf)
                m_old = m_ref[rows, :]
                m_new = jnp.maximum(m_old, jnp.max(sc, axis=-1, keepdims=True))
                m_ref[rows, :] = m_new
                a_ref[rows, :] = jnp.exp2(m_old - m_new)
                p_ref[rows, 0:nk] = jnp.exp2(sc - m_new).astype(BF16)

        def weighted_values(part):
            rows = slice(part * rp, (part + 1) * rp)
            nk = keys(part)
            acc_ref[rows, :] = a_ref[rows, :] * acc_ref[rows, :] + jnp.dot(
                p_ref[rows, 0:nk], v_ref[0, 0, pl.ds(start, nk), :], preferred_element_type=F32)

        for part in range(ATTN_ROW_PARTS):
            if part + 1 < ATTN_ROW_PARTS:
                scores(start, part + 1, keys(part + 1))
            elif not on_diagonal:
                scores(next_start, 0, tq)
            softmax(part)
            weighted_values(part)

    def body(j, carry):
        block(pl.multiple_of(j * tq, tq), pl.multiple_of((j + 1) * tq, tq), False)
        return carry

    scores(0, 0, tq)
    lax.fori_loop(0, i, body, 0)
    block(pl.multiple_of(i * tq, tq), None, True)
    acc = acc_ref[...]
    o_ref[0] = (acc[:, 0:V_HEAD_DIM] / acc[:, V_HEAD_DIM:V_HEAD_DIM + 1]).astype(o_ref.dtype)


def _attn_call(q4, k4, v4, tq):
    b, h, s, _ = q4.shape
    return pl.pallas_call(
        functools.partial(_attn_kernel, tq=tq),
        grid=(b, h, s // tq),
        in_specs=[pl.BlockSpec((1, 1, tq, QK_PAD), lambda bi, hi, i: (bi, hi, i, 0)),
                  pl.BlockSpec((1, 1, s, QK_PAD), lambda bi, hi, i: (bi, hi, 0, 0)),
                  pl.BlockSpec((1, 1, s, V_PAD), lambda bi, hi, i: (bi, hi, 0, 0))],
        out_specs=pl.BlockSpec((1, tq, V_HEAD_DIM), lambda bi, hi, i: (bi, i, hi)),
        out_shape=jax.ShapeDtypeStruct((b, s, h * V_HEAD_DIM), BF16),
        scratch_shapes=[pltpu.VMEM((tq, tq), F32), pltpu.VMEM((tq, tq), BF16),
                        pltpu.VMEM((tq, 1), F32), pltpu.VMEM((tq, 1), F32),
                        pltpu.VMEM((tq, V_PAD), F32)],
        compiler_params=pltpu.CompilerParams(
            dimension_semantics=("arbitrary", "arbitrary", "arbitrary"),
            vmem_limit_bytes=VMEM_LIMIT),
        name="attn",
    )(q4, k4, v4)


NORM_ROWS = 32


def _outproj_kernel(yr_ref, ym_ref, x_ref, mod_ref, g_ref, gffn_ref, w_ref, o_ref, h_ref):
    half = RWKV_WIDTH
    o = (jnp.dot(yr_ref[...], w_ref[0:half, :], preferred_element_type=F32)
         + jnp.dot(ym_ref[...], w_ref[half:, :], preferred_element_type=F32))
    x_mid = x_ref[...] + mod_ref[0, 2:3, :] * _rms(o, g_ref[...])
    o_ref[...] = x_mid
    h = _rms(x_mid, gffn_ref[...]) * (1.0 + mod_ref[0, 4:5, :]) + mod_ref[0, 3:4, :]
    h_ref[...] = h.astype(h_ref.dtype)


def _outproj_call(yr, ym, x2, mod3, g_post, g_pre_ffn, w_out, seq, tm):
    t, d = x2.shape
    tps = seq // tm
    return pl.pallas_call(
        _outproj_kernel,
        grid=(t // tm,),
        in_specs=[pl.BlockSpec((tm, RWKV_WIDTH), lambda i: (i, 0)),
                  pl.BlockSpec((tm, MLA_WIDTH), lambda i: (i, 0)),
                  pl.BlockSpec((tm, d), lambda i: (i, 0)),
                  pl.BlockSpec((1, N_MOD, d), lambda i: (i // tps, 0, 0)),
                  pl.BlockSpec((1, d), lambda i: (0, 0)),
                  pl.BlockSpec((1, d), lambda i: (0, 0)),
                  pl.BlockSpec(w_out.shape, lambda i: (0, 0))],
        out_specs=[pl.BlockSpec((tm, d), lambda i: (i, 0)),
                   pl.BlockSpec((tm, d), lambda i: (i, 0))],
        out_shape=[jax.ShapeDtypeStruct((t, d), F32), jax.ShapeDtypeStruct((t, d), BF16)],
        compiler_params=pltpu.CompilerParams(dimension_semantics=("arbitrary",),
                                             vmem_limit_bytes=VMEM_LIMIT),
        name="outproj",
    )(yr, ym, x2, mod3, g_post, g_pre_ffn, w_out)


FFN_ROWS = 64
FFN_COL_SPLIT = 1


def _gelu_tanh(x):
    c = 0.7978845608028654
    return 0.5 * x * (1.0 + jnp.tanh(c * (x + 0.044715 * (x * x * x))))


def _ffn_kernel(h_ref, hh_ref, x_ref, mod_ref, wg_ref, wv_ref, cwg_ref, cwv_ref,
                cbg_ref, cbv_ref, wd_ref, gpost_ref, o_ref, hcat_ref, acc_ref, ug_ref, uv_ref, act_ref,
                *, tiles_per_seq):
    i = pl.program_id(0)
    j = pl.program_id(1)
    tm, tf = act_ref.shape
    w = tf // FFN_COL_SPLIT

    @pl.when(j == 0)
    def _prologue():
        keep = (i % tiles_per_seq) != 0
        hcat_ref[0:HALO, :] = jnp.where(keep, hh_ref[...], jnp.zeros_like(hh_ref))
        hcat_ref[HALO:, :] = h_ref[...]
        acc_ref[...] = jnp.zeros_like(acc_ref)

    hb = hcat_ref[...]

    def up(c):
        cs = slice(c * w, (c + 1) * w)
        ug_ref[:, cs] = jnp.dot(hb, wg_ref[:, cs], preferred_element_type=F32)
        uv_ref[:, cs] = jnp.dot(hb, wv_ref[:, cs], preferred_element_type=F32)

    def conv(u_ref, r0, cs, cw_ref, cb_ref):
        return (cb_ref[:, cs] + cw_ref[2:3, cs] * u_ref[pl.ds(r0, FFN_ROWS), cs]
                + cw_ref[1:2, cs] * u_ref[pl.ds(r0 - 1, FFN_ROWS), cs]
                + cw_ref[0:1, cs] * u_ref[pl.ds(r0 - 2, FFN_ROWS), cs])

    def activate(c):
        cs = slice(c * w, (c + 1) * w)
        for rc in range(tm // FFN_ROWS):
            r0 = HALO + rc * FFN_ROWS
            act = (_gelu_tanh(conv(ug_ref, r0, cs, cwg_ref, cbg_ref))
                   * conv(uv_ref, r0, cs, cwv_ref, cbv_ref))
            act_ref[rc * FFN_ROWS:(rc + 1) * FFN_ROWS, cs] = act.astype(BF16)

    def down(c):
        cs = slice(c * w, (c + 1) * w)
        return jnp.dot(act_ref[:, cs], wd_ref[cs, :], preferred_element_type=F32)

    up(0)
    total = None
    for c in range(FFN_COL_SPLIT):
        if c + 1 < FFN_COL_SPLIT:
            up(c + 1)
        activate(c)
        part = down(c)
        total = part if total is None else total + part
    acc_ref[...] += total

    @pl.when(j == pl.num_programs(1) - 1)
    def _epilogue():
        for c in range(tm // NORM_ROWS):
            rows = slice(c * NORM_ROWS, (c + 1) * NORM_ROWS)
            o_ref[rows, :] = x_ref[rows, :] + mod_ref[0, 5:6, :] * _rms(acc_ref[rows, :], gpost_ref[...])


def _ffn_call(h2, x2, mod3, w_up, conv_w, conv_b, w_down, g_post, seq, tm, tf):
    t, d = x2.shape
    tps = seq // tm
    hb = tm // HALO
    nf = D_FF // tf
    return pl.pallas_call(
        functools.partial(_ffn_kernel, tiles_per_seq=tps),
        grid=(t // tm, nf),
        in_specs=[pl.BlockSpec((tm, d), lambda i, j: (i, 0)),
                  pl.BlockSpec((HALO, d), lambda i, j: (jnp.maximum(i * hb - 1, 0), 0)),
                  pl.BlockSpec((tm, d), lambda i, j: (i, 0)),
                  pl.BlockSpec((1, N_MOD, d), lambda i, j: (i // tps, 0, 0)),
                  pl.BlockSpec((d, tf), lambda i, j: (0, j)),
                  pl.BlockSpec((d, tf), lambda i, j: (0, nf + j)),
                  pl.BlockSpec((3, tf), lambda i, j: (0, j)),
                  pl.BlockSpec((3, tf), lambda i, j: (0, nf + j)),
                  pl.BlockSpec((1, tf), lambda i, j: (0, j)),
                  pl.BlockSpec((1, tf), lambda i, j: (0, nf + j)),
                  pl.BlockSpec((tf, d), lambda i, j: (j, 0)),
                  pl.BlockSpec((1, d), lambda i, j: (0, 0))],
        out_specs=pl.BlockSpec((tm, d), lambda i, j: (i, 0)),
        out_shape=jax.ShapeDtypeStruct((t, d), F32),
        scratch_shapes=[pltpu.VMEM((tm + HALO, d), BF16), pltpu.VMEM((tm, d), F32),
                        pltpu.VMEM((tm + HALO, tf), F32), pltpu.VMEM((tm + HALO, tf), F32),
                        pltpu.VMEM((tm, tf), BF16)],
        compiler_params=pltpu.CompilerParams(dimension_semantics=("arbitrary", "arbitrary"),
                                             vmem_limit_bytes=VMEM_LIMIT),
        name="ffn",
    )(h2, h2, x2, mod3, w_up, w_up, conv_w, conv_w, conv_b, conv_b, w_down, g_post)


def _pad_cols(w, n):
    return jnp.pad(w, ((0, 0), (0, n - w.shape[1])))


def _layout_w_in(w_in, mu_shift):
    w3 = 3 * RWKV_WIDTH
    o_wd = w3
    o_ad = o_wd + DECAY_LORA
    o_gd = o_ad + ICLR_LORA
    o_q = o_gd + GATE_LORA
    o_kv = o_q + Q_LORA_RANK
    o_kr = o_kv + KV_LORA_RANK
    half = QK_ROPE_DIM // 2

    def lay(m):
        kr = m[:, o_kr:o_kr + QK_ROPE_DIM]
        return jnp.concatenate([
            m[:, 0:w3],
            _pad_cols(m[:, o_wd:o_ad], LANES),
            _pad_cols(m[:, o_ad:o_gd], LANES),
            m[:, o_gd:o_q],
            m[:, o_q:o_kv],
            m[:, o_kv:o_kr],
            _pad_cols(kr, LANES),
            _pad_cols(jnp.concatenate([kr[:, half:], kr[:, :half]], axis=1), LANES),
        ], axis=1)

    mu_full = jnp.concatenate([mu_shift, jnp.zeros((w_in.shape[1] - mu_shift.shape[0],), F32)])
    w_p = lay(w_in).astype(BF16)
    mu_p = lay(mu_full[None, :])
    return w_p, mu_p


def _layout_w_q(w_q_up):
    dn, dr = QK_NOPE_DIM, QK_ROPE_DIM
    half = dr // 2
    w = w_q_up.reshape(Q_LORA_RANK, MLA_HEADS, dn + dr)
    nope, u1, u2 = w[..., :dn], w[..., dn:dn + half], w[..., dn + half:]
    z = jnp.zeros(u1.shape[:-1] + (LANES - dr,), w.dtype)
    main = jnp.concatenate([nope, u1, u2, z], axis=-1).reshape(Q_LORA_RANK, MLA_HEADS * QK_PAD)
    swap = jnp.concatenate([u2, u1, z], axis=-1).reshape(Q_LORA_RANK, MLA_HEADS * LANES)
    return jnp.concatenate([main, swap], axis=1).astype(BF16)


def _layout_w_kv(w_kv_up):
    w = w_kv_up.reshape(KV_LORA_RANK, MLA_HEADS, QK_NOPE_DIM + V_HEAD_DIM)
    kn = w[..., :QK_NOPE_DIM].reshape(KV_LORA_RANK, MLA_HEADS * QK_NOPE_DIM)
    vv = w[..., QK_NOPE_DIM:].reshape(KV_LORA_RANK, MLA_HEADS * V_HEAD_DIM)
    return jnp.concatenate([kn, vv], axis=1).astype(BF16)


def _pad_rows(w, n):
    return jnp.pad(w, ((0, n - w.shape[0]), (0, 0)))


def _block(x, c, positions, w_mod, b_mod, g_pre_mix, w_in, mu_shift, w0, w_decay_up, a0,
           w_iclr_up, w_gate_up, k_k, k_a, r_k, ln_x_g, ln_x_b, q_norm_g, w_q_up, kv_norm_g,
           w_kv_up, w_out, g_post_mix, g_pre_ffn, w_ffn_up, conv_w, conv_b, w_ffn_down,
           g_post_ffn, *, tm_in, tn_in, nb_rwkv, tm_mla, tq, tm_out, tm_ffn, tf):
    b, s, d = x.shape
    t = b * s
    x2 = x.reshape(t, d)

    c_pad = jnp.pad(c, ((0, 8 - b % 8 if b % 8 else 0), (0, 0)))
    mod = _mod_call(c_pad, w_mod, b_mod[None, :])[:b]
    mod3 = mod.reshape(b, N_MOD, d)

    w_in_p, mu_p = _layout_w_in(w_in, mu_shift)
    p2 = _inproj_call(x2, mod3, g_pre_mix[None, :], w_in_p, mu_p, s, tm_in, tn_in)

    vecs = jnp.stack([w0, a0, k_k, k_a, r_k.reshape(-1), ln_x_g, ln_x_b, jnp.zeros_like(w0)])
    y_rwkv = _rwkv_call(p2.reshape(b, s, IN_COLS_PAD), vecs,
                        _pad_rows(w_decay_up, LANES).astype(BF16),
                        _pad_rows(w_iclr_up, LANES).astype(BF16),
                        w_gate_up.astype(BF16), nb_rwkv)

    half = QK_ROPE_DIM // 2
    inv_freq = ROPE_THETA ** (-jnp.arange(0, QK_ROPE_DIM, 2, dtype=F32) / QK_ROPE_DIM)
    invf = jnp.concatenate([inv_freq, inv_freq, jnp.zeros((LANES - 2 * half,), F32)])[None, :]
    q4, k4, v4 = _mlaproj_call(p2, positions.reshape(t, 1), invf, q_norm_g[None, :],
                               kv_norm_g[None, :], _layout_w_q(w_q_up), _layout_w_kv(w_kv_up),
                               b, s, tm_mla)
    y_mla = _attn_call(q4, k4, v4, tq)

    x_mid, h_ffn = _outproj_call(y_rwkv.reshape(t, RWKV_WIDTH), y_mla.reshape(t, MLA_WIDTH), x2,
                                 mod3, g_post_mix[None, :], g_pre_ffn[None, :],
                                 w_out.astype(BF16), s, tm_out)
    out = _ffn_call(h_ffn, x_mid, mod3, w_ffn_up.astype(BF16), conv_w, conv_b[None, :],
                    w_ffn_down.astype(BF16), g_post_ffn[None, :], s, tm_ffn, tf)
    return out.reshape(b, s, d)


def kernel(x, c, positions, w_mod, b_mod, g_pre_mix, w_in, mu_shift, w0, w_decay_up, a0, w_iclr_up, w_gate_up, k_k, k_a, r_k, ln_x_g, ln_x_b, q_norm_g, w_q_up, kv_norm_g, w_kv_up, w_out, g_post_mix, g_pre_ffn, w_ffn_up, conv_w, conv_b, w_ffn_down, g_post_ffn):
    args = (x, c, positions, w_mod, b_mod, g_pre_mix, w_in, mu_shift, w0, w_decay_up, a0,
            w_iclr_up, w_gate_up, k_k, k_a, r_k, ln_x_g, ln_x_b, q_norm_g, w_q_up, kv_norm_g,
            w_kv_up, w_out, g_post_mix, g_pre_ffn, w_ffn_up, conv_w, conv_b, w_ffn_down,
            g_post_ffn)
    for l in range(w_mod.shape[0]):
        layer = [a[l] for a in args[3:]]
        x = _block(x, c, positions, *layer, tm_in=512, tn_in=1536, nb_rwkv=2, tm_mla=512, tq=1024,
                   tm_out=512, tm_ffn=512, tf=512)
    return x
```

```python
import functools

import numpy as np
import jax
import jax.numpy as jnp
from jax import lax
from jax.experimental import pallas as pl
from jax.experimental.pallas import tpu as pltpu

F32 = jnp.float32
BF16 = jnp.bfloat16

D_MODEL = 2048
CHUNK = 64
RWKV_HEAD_DIM = 64
RWKV_WIDTH = 1024
RWKV_HEADS = 16
DECAY_LORA = 96
ICLR_LORA = 96
GATE_LORA = 256
GN_EPS = 64e-5
QK_NOPE_DIM = 128
QK_ROPE_DIM = 64
V_HEAD_DIM = 128
MLA_WIDTH = 1024
MLA_HEADS = 8
Q_LORA_RANK = 512
KV_LORA_RANK = 256
ROPE_THETA = 10000.0
D_FF = 5632
NORM_EPS = 1e-6
N_MOD = 6

LANES = 128
HALO = 16
GROUP = 256
HEADS_PER_GROUP = GROUP // RWKV_HEAD_DIM
N_GROUPS = RWKV_WIDTH // GROUP
QK_PAD = 256
V_PAD = 256
IN_COLS_PAD = 4608
VMEM_LIMIT = 56 * 1024 * 1024


def _bdot(a, b):
    return jnp.dot(a.astype(BF16), b.astype(BF16), preferred_element_type=F32)


def _split2(x):
    hi = x.astype(BF16)
    lo = (x - hi.astype(F32)).astype(BF16)
    return hi, lo


def _sigmoid(x):
    return 1.0 / (1.0 + jnp.exp(-x))


def _rms(xv, g):
    ms = jnp.mean(xv * xv, axis=-1, keepdims=True)
    return xv * lax.rsqrt(ms + NORM_EPS) * g


def _mod_kernel(c_ref, w_ref, b_ref, o_ref):
    c = c_ref[...]
    s = c * _sigmoid(c)
    o_ref[...] = _bdot(s, w_ref[...]) + b_ref[...]


def _mod_call(c_pad, w_mod, b_mod, tn=1024):
    rows, d = c_pad.shape
    n = w_mod.shape[1]
    return pl.pallas_call(
        _mod_kernel,
        grid=(n // tn,),
        in_specs=[pl.BlockSpec((rows, d), lambda j: (0, 0)),
                  pl.BlockSpec((d, tn), lambda j: (0, j)),
                  pl.BlockSpec((1, tn), lambda j: (0, j))],
        out_specs=pl.BlockSpec((rows, tn), lambda j: (0, j)),
        out_shape=jax.ShapeDtypeStruct((rows, n), F32),
        compiler_params=pltpu.CompilerParams(dimension_semantics=("arbitrary",),
                                             vmem_limit_bytes=VMEM_LIMIT),
        name="mod",
    )(c_pad, w_mod, b_mod)


def _inproj_kernel(x_ref, xh_ref, mod_ref, g_ref, w_ref, mu_ref, o_ref, *, tiles_per_seq):
    i = pl.program_id(1)
    first = (i % tiles_per_seq) == 0
    shift = mod_ref[0, 0:1, :]
    scale = mod_ref[0, 1:2, :]
    g = g_ref[...]
    h = _rms(x_ref[...], g) * (1.0 + scale) + shift
    hh = _rms(xh_ref[...], g) * (1.0 + scale) + shift
    hh = jnp.where(first, 0.0, hh)
    hcat = jnp.concatenate([hh, h], axis=0).astype(BF16)
    p = jnp.dot(hcat, w_ref[...], preferred_element_type=F32)
    prev = pltpu.roll(p, 1, axis=0)
    out = p + (prev - p) * mu_ref[...]
    o_ref[...] = out[HALO:, :]


def _inproj_call(x2, mod3, g_pre, w_in_p, mu_p, seq, tm, tn):
    t, d = x2.shape
    n = w_in_p.shape[1]
    tps = seq // tm
    hb = tm // HALO
    return pl.pallas_call(
        functools.partial(_inproj_kernel, tiles_per_seq=tps),
        grid=(n // tn, t // tm),
        in_specs=[pl.BlockSpec((tm, d), lambda j, i: (i, 0)),
                  pl.BlockSpec((HALO, d), lambda j, i: (jnp.maximum(i * hb - 1, 0), 0)),
                  pl.BlockSpec((1, N_MOD, d), lambda j, i: (i // tps, 0, 0)),
                  pl.BlockSpec((1, d), lambda j, i: (0, 0)),
                  pl.BlockSpec((d, tn), lambda j, i: (0, j)),
                  pl.BlockSpec((1, tn), lambda j, i: (0, j))],
        out_specs=pl.BlockSpec((tm, tn), lambda j, i: (i, j)),
        out_shape=jax.ShapeDtypeStruct((t, n), F32),
        compiler_params=pltpu.CompilerParams(dimension_semantics=("arbitrary", "arbitrary"),
                                             vmem_limit_bytes=VMEM_LIMIT),
        name="inproj",
    )(x2, x2, mod3, g_pre, w_in_p, mu_p)


def _rwkv_masks(nb):
    i = np.arange(GROUP)[:, None]
    j = np.arange(GROUP)[None, :]
    n = RWKV_HEAD_DIM
    bd = (i // n) == (j // n)
    mk = bd.astype(np.float32)[None]
    t = np.arange(CHUNK)[:, None]
    sj = j % n
    mc = np.stack([
        sj < t,
        sj <= t,
        sj == t,
        (sj < t) & ((t // 8) == (sj // 8)),
        ((t // 16) == (sj // 16)) & ((t // 8) > (sj // 8)),
        ((t // 32) == (sj // 32)) & ((t // 16) > (sj // 16)),
        (t // 32) > (sj // 32),
    ]).astype(np.float32)
    ti = np.arange(nb * CHUNK)[:, None]
    tj = np.arange(nb * CHUNK)[None, :]
    tri = ((ti // CHUNK == tj // CHUNK) & (tj <= ti)).astype(np.float32)
    return mk, mc, tri


def _rwkv_kernel(r_ref, k_ref, v_ref, lo_ref, vec_ref, wd_ref, wa_ref, wg_ref,
                 mk_ref, mc_ref, tri_ref, e_ref, o_ref, st_ref, *, nb):
    @pl.when(pl.program_id(1) == 0)
    def _reset_state():
        st_ref[...] = jnp.zeros_like(st_ref)

    def rows(ref):
        return jnp.concatenate([ref[bb] for bb in range(nb)], axis=0)

    r = rows(r_ref)
    k = rows(k_ref)
    v = rows(v_ref)
    lo = rows(lo_ref)
    w0 = vec_ref[0:1, :]
    a0 = vec_ref[1:2, :]
    k_k = vec_ref[2:3, :]
    k_a = vec_ref[3:4, :]
    r_k = vec_ref[4:5, :]
    ln_g = vec_ref[5:6, :]
    ln_b = vec_ref[6:7, :]

    z = -(w0 + _bdot(jnp.tanh(lo[:, 0:LANES]), wd_ref[...]))
    softplus = jnp.maximum(z, 0.0) + jnp.log(1.0 + jnp.exp(-jnp.abs(z)))
    lw = -jnp.exp(-softplus - 0.5)
    a = _sigmoid(a0 + _bdot(lo[:, LANES:2 * LANES], wa_ref[...]))
    gate = _bdot(_sigmoid(lo[:, 2 * LANES:4 * LANES]), wg_ref[...])
    kk = k * k_k
    kp = k * (1.0 + (a - 1.0) * k_a)

    tri = tri_ref[...]
    l1 = lw.astype(BF16)
    rem = lw - l1.astype(F32)
    l2 = rem.astype(BF16)
    l3 = (rem - l2.astype(F32)).astype(BF16)
    cum = (jnp.dot(tri, l1, preferred_element_type=F32)
           + jnp.dot(tri, l2, preferred_element_type=F32)
           + jnp.dot(tri, l3, preferred_element_type=F32))
    cum_last = [cum[(bb + 1) * CHUNK - 1:(bb + 1) * CHUNK, :] for bb in range(nb)]
    cum_last_rows = jnp.concatenate(
        [jnp.broadcast_to(cl, (CHUNK, cl.shape[1])) for cl in cum_last], axis=0)
    e_pos = jnp.exp(cum)
    e_pos_x = jnp.exp(cum - lw)
    e_neg = jnp.exp(-cum)
    e_end = jnp.exp(cum_last_rows - cum)

    m_bd = mk_ref[0]
    c_strict = mc_ref[0]
    c_incl = mc_ref[1]
    c_eye = mc_ref[2]
    c_b8 = mc_ref[3].astype(BF16)
    c_levels = [mc_ref[lv].astype(BF16) for lv in (4, 5, 6)]
    ones_bd = e_ref[...]

    def head_sum(x):
        return jnp.dot(x.astype(BF16), ones_bd, preferred_element_type=F32)

    def head_sum_pair(x, y):
        s2 = head_sum(jnp.concatenate([x, y], axis=0))
        return s2[0:CHUNK], s2[CHUNK:]

    def head_sum2(x):
        hi, lo_ = _split2(x)
        s2 = jnp.dot(jnp.concatenate([hi, lo_], axis=0), ones_bd, preferred_element_type=F32)
        return s2[0:CHUNK] + s2[CHUNK:]

    def bd(x):
        return jnp.concatenate([x.astype(BF16)] * HEADS_PER_GROUP, axis=0) * ones_bd

    def fold4(x):
        return (x[0:CHUNK] + x[CHUNK:2 * CHUNK] + x[2 * CHUNK:3 * CHUNK] + x[3 * CHUNK:4 * CHUNK])

    chains = [(bb, gi) for bb in range(nb) for gi in range(N_GROUPS)]
    rsl = [(slice(bb * CHUNK, (bb + 1) * CHUNK), slice(gi * GROUP, (gi + 1) * GROUP))
           for bb, gi in chains]

    def each(fn, *lists):
        return [fn(*args) for args in zip(*lists)]

    def cut(x):
        return [x[rs, sl] for rs, sl in rsl]

    r_g, v_g, kp_g, a_g, kk_g = cut(r), cut(v), cut(kp), cut(a), cut(kk)
    sums = each(lambda x, rg, kg, rsl_: head_sum_pair(x * x, rg * kg * r_k[:, rsl_[1]]),
                kk_g, r_g, kp_g, rsl)
    kkn = each(lambda x, s2: x / jnp.maximum(jnp.sqrt(s2[0]), 1e-12), kk_g, sums)
    ka = each(lambda x, y: x * y, kkn, a_g)
    a_bar = each(lambda x, e: -x * e, kkn, cut(e_pos_x))
    r_bar = each(lambda x, e: x * e, r_g, cut(e_pos))
    b_til = each(lambda x, e: x * e, ka, cut(e_neg))
    k_til = each(lambda x, e: x * e, kp_g, cut(e_neg))
    b_hat = each(lambda x, e: x * e, ka, cut(e_end))
    k_hat = each(lambda x, e: x * e, kp_g, cut(e_end))

    def score_fn(ab, rb, bt, kt):
        lhs = jnp.concatenate([ab, rb], axis=0)
        rhs = jnp.concatenate([bd(bt), bd(kt)], axis=0)
        return lax.dot_general(lhs.astype(BF16), rhs,
                               (((1,), (1,)), ((), ())), preferred_element_type=F32)

    scores = each(score_fn, a_bar, r_bar, b_til, k_til)
    a_ab = each(lambda x: (x[0:CHUNK, 0:GROUP] * c_strict).astype(BF16), scores)
    a_ak = each(lambda x: (x[0:CHUNK, GROUP:] * c_strict).astype(BF16), scores)
    a_rb = each(lambda x: (x[CHUNK:, 0:GROUP] * c_incl).astype(BF16), scores)
    a_rk = each(lambda x: (x[CHUNK:, GROUP:] * c_incl).astype(BF16), scores)

    n1 = each(lambda x: x * c_b8, a_ab)
    n2 = each(lambda x: _bdot(x, bd(x)).astype(BF16), n1)
    sq = each(lambda x, y: _bdot(jnp.concatenate([y, c_eye + x], axis=0), bd(y)), n1, n2)
    inv = each(lambda x, s2: c_eye + x + s2[CHUNK:], n1, sq)
    inv = each(lambda x, s2: x + _bdot(x, bd(s2[0:CHUNK])), inv, sq)
    for c_off in c_levels:
        tmp = each(lambda x, a: _bdot(x, bd(a * c_off)).astype(BF16), inv, a_ab)
        inv = each(lambda x, t: x + _bdot(t, bd(x)), inv, tmp)

    v_bd = each(bd, v_g)
    av = each(lambda ak, rk, vb: _bdot(jnp.concatenate([ak, rk], axis=0), vb),
              a_ak, a_rk, v_bd)
    wu = each(lambda t, ab, x: _bdot(t, jnp.concatenate([bd(ab), bd(x[0:CHUNK])], axis=1)),
              inv, a_bar, av)
    ry = each(lambda a, x: _bdot(a, jnp.concatenate([bd(x[:, 0:GROUP]), bd(x[:, GROUP:])], axis=1)),
              a_rb, wu)
    r_hat = each(lambda x, y: x + y[:, 0:GROUP], r_bar, ry)
    y_hat = each(lambda y, x: y[:, GROUP:] + x[CHUNK:], ry, av)

    state = [st_ref[bb, gi] for bb, gi in chains]
    y = each(lambda rh, st, yh: lax.dot_general(
        rh.astype(BF16), bd(st), (((1,), (1,)), ((), ())),
        preferred_element_type=F32) + yh, r_hat, state, y_hat)

    def trans_fn(bh, kh, x, vg):
        vb = vg.astype(BF16)
        lhs_t = jnp.concatenate(
            [x.astype(BF16), jnp.concatenate([jnp.zeros_like(vb), vb], axis=1)], axis=0)
        rhs_t = jnp.concatenate([bh, kh], axis=0)
        return lax.dot_general(lhs_t, rhs_t.astype(BF16),
                               (((0,), (0,)), ((), ())), preferred_element_type=F32)

    mc = each(trans_fn, b_hat, k_hat, wu, v_g)
    for (bb, gi), (_, sl), mci, st in zip(chains, rsl, mc, state):
        st_ref[bb, gi] = (st * jnp.exp(cum_last[bb][:, sl])
                          + _bdot(st, mci[0:GROUP, :] * m_bd) + fold4(mci[GROUP:, :] * m_bd))

    inv_n = 1.0 / RWKV_HEAD_DIM
    mean = each(lambda x: head_sum2(x) * inv_n, y)
    dlt = each(lambda x, m: x - m, y, mean)
    var = each(lambda x: head_sum(x * x) * inv_n, dlt)
    bonus = each(lambda s2, vg: s2[1] * vg, sums, v_g)
    for (bb, gi), (rs, sl), d, vr, bo in zip(chains, rsl, dlt, var, bonus):
        yn = d * lax.rsqrt(vr + GN_EPS) * ln_g[:, sl] + ln_b[:, sl]
        o_ref[bb, :, sl] = ((yn + bo) * gate[rs, sl]).astype(o_ref.dtype)


def _rwkv_call(p3, vecs, wd, wa, wg, nb):
    b, s, _ = p3.shape
    w = RWKV_WIDTH
    mk, mc, tri = _rwkv_masks(nb)
    mk = jnp.asarray(mk)
    mc = jnp.asarray(mc)
    tri = jnp.asarray(tri, dtype=BF16)
    ones_bd = mk[0].astype(BF16)
    const2 = lambda bi, ci: (0, 0)
    const3 = lambda bi, ci: (0, 0, 0)
    return pl.pallas_call(
        functools.partial(_rwkv_kernel, nb=nb),
        grid=(b // nb, s // CHUNK),
        in_specs=[pl.BlockSpec((nb, CHUNK, w), lambda bi, ci: (bi, ci, 0)),
                  pl.BlockSpec((nb, CHUNK, w), lambda bi, ci: (bi, ci, 1)),
                  pl.BlockSpec((nb, CHUNK, w), lambda bi, ci: (bi, ci, 2)),
                  pl.BlockSpec((nb, CHUNK, 4 * LANES), lambda bi, ci: (bi, ci, 3 * w // (4 * LANES))),
                  pl.BlockSpec(vecs.shape, const2),
                  pl.BlockSpec(wd.shape, const2),
                  pl.BlockSpec(wa.shape, const2),
                  pl.BlockSpec(wg.shape, const2),
                  pl.BlockSpec(mk.shape, const3),
                  pl.BlockSpec(mc.shape, const3),
                  pl.BlockSpec(tri.shape, const2),
                  pl.BlockSpec(ones_bd.shape, const2)],
        out_specs=pl.BlockSpec((nb, CHUNK, w), lambda bi, ci: (bi, ci, 0)),
        out_shape=jax.ShapeDtypeStruct((b, s, w), BF16),
        scratch_shapes=[pltpu.VMEM((nb, N_GROUPS, CHUNK, GROUP), F32)],
        compiler_params=pltpu.CompilerParams(dimension_semantics=("arbitrary", "arbitrary"),
                                             vmem_limit_bytes=VMEM_LIMIT),
        name="rwkv",
    )(p3, p3, p3, p3, vecs, wd, wa, wg, mk, mc, tri, ones_bd)


def _mlaproj_kernel(pq_ref, pkv_ref, pos_ref, invf_ref, gq_ref, gkv_ref, wq_ref, wkv_ref,
                    q_ref, k_ref, v_ref):
    q = _bdot(_rms(pq_ref[...], gq_ref[...]), wq_ref[...])
    pkv = pkv_ref[...]
    kv = _bdot(_rms(pkv[:, 0:KV_LORA_RANK], gkv_ref[...]), wkv_ref[...])
    ang = pos_ref[...].astype(F32) * invf_ref[...]
    lane = lax.broadcasted_iota(jnp.int32, ang.shape, 1)
    half = QK_ROPE_DIM // 2
    cos_f = jnp.where(lane < QK_ROPE_DIM, jnp.cos(ang), 0.0)
    sin = jnp.sin(ang)
    sin_f = jnp.where(lane < half, -sin, jnp.where(lane < QK_ROPE_DIM, sin, 0.0))
    k_rot = pkv[:, KV_LORA_RANK:KV_LORA_RANK + LANES] * cos_f + pkv[:, KV_LORA_RANK + LANES:] * sin_f
    scale = (QK_NOPE_DIM + QK_ROPE_DIM) ** -0.5 * 1.4426950408889634
    swap0 = MLA_HEADS * QK_PAD
    ones_col = (lane == 0).astype(F32)
    for h in range(MLA_HEADS):
        q_nope = q[:, h * QK_PAD:h * QK_PAD + LANES]
        q_rot = (q[:, h * QK_PAD + LANES:(h + 1) * QK_PAD] * cos_f
                 + q[:, swap0 + h * LANES:swap0 + (h + 1) * LANES] * sin_f)
        q_ref[0, h] = (jnp.concatenate([q_nope, q_rot], axis=1) * scale).astype(q_ref.dtype)
        k_ref[0, h] = jnp.concatenate([kv[:, h * LANES:(h + 1) * LANES], k_rot], axis=1).astype(k_ref.dtype)
        v_ref[0, h] = jnp.concatenate(
            [kv[:, (MLA_HEADS + h) * LANES:(MLA_HEADS + h + 1) * LANES], ones_col],
            axis=1).astype(v_ref.dtype)


def _mlaproj_call(p2, pos2, invf, gq, gkv, wq, wkv, batch, seq, tm):
    t = p2.shape[0]
    tps = seq // tm
    blk = 4 * LANES
    q_blk = (IN_COLS_PAD - 2 * blk) // blk
    hd = MLA_HEADS
    out_map = lambda i: (i // tps, 0, i % tps, 0)
    const = lambda i: (0, 0)
    return pl.pallas_call(
        _mlaproj_kernel,
        grid=(t // tm,),
        in_specs=[pl.BlockSpec((tm, blk), lambda i: (i, q_blk)),
                  pl.BlockSpec((tm, blk), lambda i: (i, q_blk + 1)),
                  pl.BlockSpec((tm, 1), lambda i: (i, 0)),
                  pl.BlockSpec(invf.shape, const),
                  pl.BlockSpec(gq.shape, const),
                  pl.BlockSpec(gkv.shape, const),
                  pl.BlockSpec(wq.shape, const),
                  pl.BlockSpec(wkv.shape, const)],
        out_specs=[pl.BlockSpec((1, hd, tm, QK_PAD), out_map),
                   pl.BlockSpec((1, hd, tm, QK_PAD), out_map),
                   pl.BlockSpec((1, hd, tm, V_PAD), out_map)],
        out_shape=[jax.ShapeDtypeStruct((batch, hd, seq, QK_PAD), BF16),
                   jax.ShapeDtypeStruct((batch, hd, seq, QK_PAD), BF16),
                   jax.ShapeDtypeStruct((batch, hd, seq, V_PAD), BF16)],
        compiler_params=pltpu.CompilerParams(dimension_semantics=("arbitrary",),
                                             vmem_limit_bytes=VMEM_LIMIT),
        name="mlaproj",
    )(p2, p2, pos2, invf, gq, gkv, wq, wkv)


ATTN_ROW_PARTS = 4
ATTN_SM_ROWS = 32


def _attn_kernel(q_ref, k_ref, v_ref, o_ref, s_ref, p_ref, m_ref, a_ref, acc_ref, *, tq):
    i = pl.program_id(2)
    rp = tq // ATTN_ROW_PARTS
    m_ref[...] = jnp.full(m_ref.shape, -jnp.inf, F32)
    acc_ref[...] = jnp.zeros_like(acc_ref)

    def scores(start, part, nk):
        rows = slice(part * rp, (part + 1) * rp)
        s_ref[rows, 0:nk] = lax.dot_general(
            q_ref[0, 0, rows, :], k_ref[0, 0, pl.ds(start, nk), :],
            (((1,), (1,)), ((), ())), preferred_element_type=F32)

    def block(start, next_start, on_diagonal):
        def keys(part):
            return (part + 1) * rp if on_diagonal else tq

        def softmax(part):
            nk = keys(part)
            for c in range(rp // ATTN_SM_ROWS):
                r0 = part * rp + c * ATTN_SM_ROWS
                rows = slice(r0, r0 + ATTN_SM_ROWS)
                sc = s_ref[rows, 0:nk]
                if on_diagonal:
                    row = (r0 + lax.broadcasted_iota(jnp.int32, sc.shape, 0)) // CHUNK
                    col = lax.broadcasted_iota(jnp.int32, sc.shape, 1) // CHUNK
                    sc = jnp.where(row >= col, sc, -jnp.inf)
                m_old = m_ref[rows, :]
                m_new = jnp.maximum(m_old, jnp.max(sc, axis=-1, keepdims=True))
                m_ref[rows, :] = m_new
                a_ref[rows, :] = jnp.exp2(m_old - m_new)
                p_ref[rows, 0:nk] = jnp.exp2(sc - m_new).astype(BF16)

        def weighted_values(part):
            rows = slice(part * rp, (part + 1) * rp)
            nk = keys(part)
            acc_ref[rows, :] = a_ref[rows, :] * acc_ref[rows, :] + jnp.dot(
                p_ref[rows, 0:nk], v_ref[0, 0, pl.ds(start, nk), :], preferred_element_type=F32)

        for part in range(ATTN_ROW_PARTS):
            if part + 1 < ATTN_ROW_PARTS:
                scores(start, part + 1, keys(part + 1))
            elif not on_diagonal:
                scores(next_start, 0, tq)
            softmax(part)
            weighted_values(part)

    def body(j, carry):
        block(pl.multiple_of(j * tq, tq), pl.multiple_of((j + 1) * tq, tq), False)
        return carry

    scores(0, 0, tq)
    lax.fori_loop(0, i, body, 0)
    block(pl.multiple_of(i * tq, tq), None, True)
    acc = acc_ref[...]
    o_ref[0] = (acc[:, 0:V_HEAD_DIM] / acc[:, V_HEAD_DIM:V_HEAD_DIM + 1]).astype(o_ref.dtype)


def _attn_call(q4, k4, v4, tq):
    b, h, s, _ = q4.shape
    return pl.pallas_call(
        functools.partial(_attn_kernel, tq=tq),
        grid=(b, h, s // tq),
        in_specs=[pl.BlockSpec((1, 1, tq, QK_PAD), lambda bi, hi, i: (bi, hi, i, 0)),
                  pl.BlockSpec((1, 1, s, QK_PAD), lambda bi, hi, i: (bi, hi, 0, 0)),
                  pl.BlockSpec((1, 1, s, V_PAD), lambda bi, hi, i: (bi, hi, 0, 0))],
        out_specs=pl.BlockSpec((1, tq, V_HEAD_DIM), lambda bi, hi, i: (bi, i, hi)),
        out_shape=jax.ShapeDtypeStruct((b, s, h * V_HEAD_DIM), BF16),
        scratch_shapes=[pltpu.VMEM((tq, tq), F32), pltpu.VMEM((tq, tq), BF16),
                        pltpu.VMEM((tq, 1), F32), pltpu.VMEM((tq, 1), F32),
                        pltpu.VMEM((tq, V_PAD), F32)],
        compiler_params=pltpu.CompilerParams(
            dimension_semantics=("arbitrary", "arbitrary", "arbitrary"),
            vmem_limit_bytes=VMEM_LIMIT),
        name="attn",
    )(q4, k4, v4)


NORM_ROWS = 32


def _outproj_kernel(yr_ref, ym_ref, x_ref, mod_ref, g_ref, gffn_ref, w_ref, o_ref, h_ref):
    half = RWKV_WIDTH
    o = (jnp.dot(yr_ref[...], w_ref[0:half, :], preferred_element_type=F32)
         + jnp.dot(ym_ref[...], w_ref[half:, :], preferred_element_type=F32))
    x_mid = x_ref[...] + mod_ref[0, 2:3, :] * _rms(o, g_ref[...])
    o_ref[...] = x_mid
    h = _rms(x_mid, gffn_ref[...]) * (1.0 + mod_ref[0, 4:5, :]) + mod_ref[0, 3:4, :]
    h_ref[...] = h.astype(h_ref.dtype)


def _outproj_call(yr, ym, x2, mod3, g_post, g_pre_ffn, w_out, seq, tm):
    t, d = x2.shape
    tps = seq // tm
    return pl.pallas_call(
        _outproj_kernel,
        grid=(t // tm,),
        in_specs=[pl.BlockSpec((tm, RWKV_WIDTH), lambda i: (i, 0)),
                  pl.BlockSpec((tm, MLA_WIDTH), lambda i: (i, 0)),
                  pl.BlockSpec((tm, d), lambda i: (i, 0)),
                  pl.BlockSpec((1, N_MOD, d), lambda i: (i // tps, 0, 0)),
                  pl.BlockSpec((1, d), lambda i: (0, 0)),
                  pl.BlockSpec((1, d), lambda i: (0, 0)),
                  pl.BlockSpec(w_out.shape, lambda i: (0, 0))],
        out_specs=[pl.BlockSpec((tm, d), lambda i: (i, 0)),
                   pl.BlockSpec((tm, d), lambda i: (i, 0))],
        out_shape=[jax.ShapeDtypeStruct((t, d), F32), jax.ShapeDtypeStruct((t, d), BF16)],
        compiler_params=pltpu.CompilerParams(dimension_semantics=("arbitrary",),
                                             vmem_limit_bytes=VMEM_LIMIT),
        name="outproj",
    )(yr, ym, x2, mod3, g_post, g_pre_ffn, w_out)


FFN_ROWS = 128
FFN_COL_SPLIT = 1


def _gelu_tanh(x):
    c = 0.7978845608028654
    return 0.5 * x * (1.0 + jnp.tanh(c * (x + 0.044715 * (x * x * x))))


def _ffn_kernel(h_ref, hh_ref, x_ref, mod_ref, wg_ref, wv_ref, cwg_ref, cwv_ref,
                cbg_ref, cbv_ref, wd_ref, gpost_ref, o_ref, hcat_ref, acc_ref, ug_ref, uv_ref, act_ref,
                *, tiles_per_seq):
    i = pl.program_id(0)
    j = pl.program_id(1)
    tm, tf = act_ref.shape
    w = tf // FFN_COL_SPLIT

    @pl.when(j == 0)
    def _prologue():
        keep = (i % tiles_per_seq) != 0
        hcat_ref[0:HALO, :] = jnp.where(keep, hh_ref[...], jnp.zeros_like(hh_ref))
        hcat_ref[HALO:, :] = h_ref[...]
        acc_ref[...] = jnp.zeros_like(acc_ref)

    hb = hcat_ref[...]

    def up(c):
        cs = slice(c * w, (c + 1) * w)
        ug_ref[:, cs] = jnp.dot(hb, wg_ref[:, cs], preferred_element_type=F32)
        uv_ref[:, cs] = jnp.dot(hb, wv_ref[:, cs], preferred_element_type=F32)

    def conv(u_ref, r0, cs, cw_ref, cb_ref):
        return (cb_ref[:, cs] + cw_ref[2:3, cs] * u_ref[pl.ds(r0, FFN_ROWS), cs]
                + cw_ref[1:2, cs] * u_ref[pl.ds(r0 - 1, FFN_ROWS), cs]
                + cw_ref[0:1, cs] * u_ref[pl.ds(r0 - 2, FFN_ROWS), cs])

    def activate(c):
        cs = slice(c * w, (c + 1) * w)
        for rc in range(tm // FFN_ROWS):
            r0 = HALO + rc * FFN_ROWS
            act = (_gelu_tanh(conv(ug_ref, r0, cs, cwg_ref, cbg_ref))
                   * conv(uv_ref, r0, cs, cwv_ref, cbv_ref))
            act_ref[rc * FFN_ROWS:(rc + 1) * FFN_ROWS, cs] = act.astype(BF16)

    def down(c):
        cs = slice(c * w, (c + 1) * w)
        return jnp.dot(act_ref[:, cs], wd_ref[cs, :], preferred_element_type=F32)

    up(0)
    total = None
    for c in range(FFN_COL_SPLIT):
        if c + 1 < FFN_COL_SPLIT:
            up(c + 1)
        activate(c)
        part = down(c)
        total = part if total is None else total + part
    acc_ref[...] += total

    @pl.when(j == pl.num_programs(1) - 1)
    def _epilogue():
        for c in range(tm // NORM_ROWS):
            rows = slice(c * NORM_ROWS, (c + 1) * NORM_ROWS)
            o_ref[rows, :] = x_ref[rows, :] + mod_ref[0, 5:6, :] * _rms(acc_ref[rows, :], gpost_ref[...])


def _ffn_call(h2, x2, mod3, w_up, conv_w, conv_b, w_down, g_post, seq, tm, tf):
    t, d = x2.shape
    tps = seq // tm
    hb = tm // HALO
    nf = D_FF // tf
    return pl.pallas_call(
        functools.partial(_ffn_kernel, tiles_per_seq=tps),
        grid=(t // tm, nf),
        in_specs=[pl.BlockSpec((tm, d), lambda i, j: (i, 0)),
                  pl.BlockSpec((HALO, d), lambda i, j: (jnp.maximum(i * hb - 1, 0), 0)),
                  pl.BlockSpec((tm, d), lambda i, j: (i, 0)),
                  pl.BlockSpec((1, N_MOD, d), lambda i, j: (i // tps, 0, 0)),
                  pl.BlockSpec((d, tf), lambda i, j: (0, j)),
                  pl.BlockSpec((d, tf), lambda i, j: (0, nf + j)),
                  pl.BlockSpec((3, tf), lambda i, j: (0, j)),
                  pl.BlockSpec((3, tf), lambda i, j: (0, nf + j)),
                  pl.BlockSpec((1, tf), lambda i, j: (0, j)),
                  pl.BlockSpec((1, tf), lambda i, j: (0, nf + j)),
                  pl.BlockSpec((tf, d), lambda i, j: (j, 0)),
                  pl.BlockSpec((1, d), lambda i, j: (0, 0))],
        out_specs=pl.BlockSpec((tm, d), lambda i, j: (i, 0)),
        out_shape=jax.ShapeDtypeStruct((t, d), F32),
        scratch_shapes=[pltpu.VMEM((tm + HALO, d), BF16), pltpu.VMEM((tm, d), F32),
                        pltpu.VMEM((tm + HALO, tf), F32), pltpu.VMEM((tm + HALO, tf), F32),
                        pltpu.VMEM((tm, tf), BF16)],
        compiler_params=pltpu.CompilerParams(dimension_semantics=("arbitrary", "arbitrary"),
                                             vmem_limit_bytes=VMEM_LIMIT),
        name="ffn",
    )(h2, h2, x2, mod3, w_up, w_up, conv_w, conv_w, conv_b, conv_b, w_down, g_post)


def _pad_cols(w, n):
    return jnp.pad(w, ((0, 0), (0, n - w.shape[1])))


def _layout_w_in(w_in, mu_shift):
    w3 = 3 * RWKV_WIDTH
    o_wd = w3
    o_ad = o_wd + DECAY_LORA
    o_gd = o_ad + ICLR_LORA
    o_q = o_gd + GATE_LORA
    o_kv = o_q + Q_LORA_RANK
    o_kr = o_kv + KV_LORA_RANK
    half = QK_ROPE_DIM // 2

    def lay(m):
        kr = m[:, o_kr:o_kr + QK_ROPE_DIM]
        return jnp.concatenate([
            m[:, 0:w3],
            _pad_cols(m[:, o_wd:o_ad], LANES),
            _pad_cols(m[:, o_ad:o_gd], LANES),
            m[:, o_gd:o_q],
            m[:, o_q:o_kv],
            m[:, o_kv:o_kr],
            _pad_cols(kr, LANES),
            _pad_cols(jnp.concatenate([kr[:, half:], kr[:, :half]], axis=1), LANES),
        ], axis=1)

    mu_full = jnp.concatenate([mu_shift, jnp.zeros((w_in.shape[1] - mu_shift.shape[0],), F32)])
    w_p = lay(w_in).astype(BF16)
    mu_p = lay(mu_full[None, :])
    return w_p, mu_p


def _layout_w_q(w_q_up):
    dn, dr = QK_NOPE_DIM, QK_ROPE_DIM
    half = dr // 2
    w = w_q_up.reshape(Q_LORA_RANK, MLA_HEADS, dn + dr)
    nope, u1, u2 = w[..., :dn], w[..., dn:dn + half], w[..., dn + half:]
    z = jnp.zeros(u1.shape[:-1] + (LANES - dr,), w.dtype)
    main = jnp.concatenate([nope, u1, u2, z], axis=-1).reshape(Q_LORA_RANK, MLA_HEADS * QK_PAD)
    swap = jnp.concatenate([u2, u1, z], axis=-1).reshape(Q_LORA_RANK, MLA_HEADS * LANES)
    return jnp.concatenate([main, swap], axis=1).astype(BF16)


def _layout_w_kv(w_kv_up):
    w = w_kv_up.reshape(KV_LORA_RANK, MLA_HEADS, QK_NOPE_DIM + V_HEAD_DIM)
    kn = w[..., :QK_NOPE_DIM].reshape(KV_LORA_RANK, MLA_HEADS * QK_NOPE_DIM)
    vv = w[..., QK_NOPE_DIM:].reshape(KV_LORA_RANK, MLA_HEADS * V_HEAD_DIM)
    return jnp.concatenate([kn, vv], axis=1).astype(BF16)


def _pad_rows(w, n):
    return jnp.pad(w, ((0, n - w.shape[0]), (0, 0)))


def _block(x, c, positions, w_mod, b_mod, g_pre_mix, w_in, mu_shift, w0, w_decay_up, a0,
           w_iclr_up, w_gate_up, k_k, k_a, r_k, ln_x_g, ln_x_b, q_norm_g, w_q_up, kv_norm_g,
           w_kv_up, w_out, g_post_mix, g_pre_ffn, w_ffn_up, conv_w, conv_b, w_ffn_down,
           g_post_ffn, *, tm_in, tn_in, nb_rwkv, tm_mla, tq, tm_out, tm_ffn, tf):
    b, s, d = x.shape
    t = b * s
    x2 = x.reshape(t, d)

    c_pad = jnp.pad(c, ((0, 8 - b % 8 if b % 8 else 0), (0, 0)))
    mod = _mod_call(c_pad, w_mod, b_mod[None, :])[:b]
    mod3 = mod.reshape(b, N_MOD, d)

    w_in_p, mu_p = _layout_w_in(w_in, mu_shift)
    p2 = _inproj_call(x2, mod3, g_pre_mix[None, :], w_in_p, mu_p, s, tm_in, tn_in)

    vecs = jnp.stack([w0, a0, k_k, k_a, r_k.reshape(-1), ln_x_g, ln_x_b, jnp.zeros_like(w0)])
    y_rwkv = _rwkv_call(p2.reshape(b, s, IN_COLS_PAD), vecs,
                        _pad_rows(w_decay_up, LANES).astype(BF16),
                        _pad_rows(w_iclr_up, LANES).astype(BF16),
                        w_gate_up.astype(BF16), nb_rwkv)

    half = QK_ROPE_DIM // 2
    inv_freq = ROPE_THETA ** (-jnp.arange(0, QK_ROPE_DIM, 2, dtype=F32) / QK_ROPE_DIM)
    invf = jnp.concatenate([inv_freq, inv_freq, jnp.zeros((LANES - 2 * half,), F32)])[None, :]
    q4, k4, v4 = _mlaproj_call(p2, positions.reshape(t, 1), invf, q_norm_g[None, :],
                               kv_norm_g[None, :], _layout_w_q(w_q_up), _layout_w_kv(w_kv_up),
                               b, s, tm_mla)
    y_mla = _attn_call(q4, k4, v4, tq)

    x_mid, h_ffn = _outproj_call(y_rwkv.reshape(t, RWKV_WIDTH), y_mla.reshape(t, MLA_WIDTH), x2,
                                 mod3, g_post_mix[None, :], g_pre_ffn[None, :],
                                 w_out.astype(BF16), s, tm_out)
    out = _ffn_call(h_ffn, x_mid, mod3, w_ffn_up.astype(BF16), conv_w, conv_b[None, :],
                    w_ffn_down.astype(BF16), g_post_ffn[None, :], s, tm_ffn, tf)
    return out.reshape(b, s, d)


def kernel(x, c, positions, w_mod, b_mod, g_pre_mix, w_in, mu_shift, w0, w_decay_up, a0, w_iclr_up, w_gate_up, k_k, k_a, r_k, ln_x_g, ln_x_b, q_norm_g, w_q_up, kv_norm_g, w_kv_up, w_out, g_post_mix, g_pre_ffn, w_ffn_up, conv_w, conv_b, w_ffn_down, g_post_ffn):
    args = (x, c, positions, w_mod, b_mod, g_pre_mix, w_in, mu_shift, w0, w_decay_up, a0,
            w_iclr_up, w_gate_up, k_k, k_a, r_k, ln_x_g, ln_x_b, q_norm_g, w_q_up, kv_norm_g,
            w_kv_up, w_out, g_post_mix, g_pre_ffn, w_ffn_up, conv_w, conv_b, w_ffn_down,
            g_post_ffn)
    for l in range(w_mod.shape[0]):
        layer = [a[l] for a in args[3:]]
        x = _block(x, c, positions, *layer, tm_in=512, tn_in=1536, nb_rwkv=2, tm_mla=512, tq=1024,
                   tm_out=512, tm_ffn=512, tf=512)
    return x
```

```python
import functools

import numpy as np
import jax
import jax.numpy as jnp
from jax import lax
from jax.experimental import pallas as pl
from jax.experimental.pallas import tpu as pltpu

F32 = jnp.float32
BF16 = jnp.bfloat16

D_MODEL = 2048
CHUNK = 64
RWKV_HEAD_DIM = 64
RWKV_WIDTH = 1024
RWKV_HEADS = 16
DECAY_LORA = 96
ICLR_LORA = 96
GATE_LORA = 256
GN_EPS = 64e-5
QK_NOPE_DIM = 128
QK_ROPE_DIM = 64
V_HEAD_DIM = 128
MLA_WIDTH = 1024
MLA_HEADS = 8
Q_LORA_RANK = 512
KV_LORA_RANK = 256
ROPE_THETA = 10000.0
D_FF = 5632
NORM_EPS = 1e-6
N_MOD = 6

LANES = 128
HALO = 16
GROUP = 256
HEADS_PER_GROUP = GROUP // RWKV_HEAD_DIM
N_GROUPS = RWKV_WIDTH // GROUP
QK_PAD = 256
V_PAD = 256
IN_COLS_PAD = 4608
VMEM_LIMIT = 56 * 1024 * 1024


def _bdot(a, b):
    return jnp.dot(a.astype(BF16), b.astype(BF16), preferred_element_type=F32)


def _split2(x):
    hi = x.astype(BF16)
    lo = (x - hi.astype(F32)).astype(BF16)
    return hi, lo


def _sigmoid(x):
    return 1.0 / (1.0 + jnp.exp(-x))


def _rms(xv, g):
    ms = jnp.mean(xv * xv, axis=-1, keepdims=True)
    return xv * lax.rsqrt(ms + NORM_EPS) * g


def _mod_kernel(c_ref, w_ref, b_ref, o_ref):
    c = c_ref[...]
    s = c * _sigmoid(c)
    o_ref[...] = _bdot(s, w_ref[...]) + b_ref[...]


def _mod_call(c_pad, w_mod, b_mod, tn=1024):
    rows, d = c_pad.shape
    n = w_mod.shape[1]
    return pl.pallas_call(
        _mod_kernel,
        grid=(n // tn,),
        in_specs=[pl.BlockSpec((rows, d), lambda j: (0, 0)),
                  pl.BlockSpec((d, tn), lambda j: (0, j)),
                  pl.BlockSpec((1, tn), lambda j: (0, j))],
        out_specs=pl.BlockSpec((rows, tn), lambda j: (0, j)),
        out_shape=jax.ShapeDtypeStruct((rows, n), F32),
        compiler_params=pltpu.CompilerParams(dimension_semantics=("arbitrary",),
                                             vmem_limit_bytes=VMEM_LIMIT),
        name="mod",
    )(c_pad, w_mod, b_mod)


def _inproj_kernel(x_ref, xh_ref, mod_ref, g_ref, w_ref, mu_ref, o_ref, *, tiles_per_seq):
    i = pl.program_id(1)
    first = (i % tiles_per_seq) == 0
    shift = mod_ref[0, 0:1, :]
    scale = mod_ref[0, 1:2, :]
    g = g_ref[...]
    h = _rms(x_ref[...], g) * (1.0 + scale) + shift
    hh = _rms(xh_ref[...], g) * (1.0 + scale) + shift
    hh = jnp.where(first, 0.0, hh)
    hcat = jnp.concatenate([hh, h], axis=0).astype(BF16)
    p = jnp.dot(hcat, w_ref[...], preferred_element_type=F32)
    prev = pltpu.roll(p, 1, axis=0)
    out = p + (prev - p) * mu_ref[...]
    o_ref[...] = out[HALO:, :]


def _inproj_call(x2, mod3, g_pre, w_in_p, mu_p, seq, tm, tn):
    t, d = x2.shape
    n = w_in_p.shape[1]
    tps = seq // tm
    hb = tm // HALO
    return pl.pallas_call(
        functools.partial(_inproj_kernel, tiles_per_seq=tps),
        grid=(n // tn, t // tm),
        in_specs=[pl.BlockSpec((tm, d), lambda j, i: (i, 0)),
                  pl.BlockSpec((HALO, d), lambda j, i: (jnp.maximum(i * hb - 1, 0), 0)),
                  pl.BlockSpec((1, N_MOD, d), lambda j, i: (i // tps, 0, 0)),
                  pl.BlockSpec((1, d), lambda j, i: (0, 0)),
                  pl.BlockSpec((d, tn), lambda j, i: (0, j)),
                  pl.BlockSpec((1, tn), lambda j, i: (0, j))],
        out_specs=pl.BlockSpec((tm, tn), lambda j, i: (i, j)),
        out_shape=jax.ShapeDtypeStruct((t, n), F32),
        compiler_params=pltpu.CompilerParams(dimension_semantics=("arbitrary", "arbitrary"),
                                             vmem_limit_bytes=VMEM_LIMIT),
        name="inproj",
    )(x2, x2, mod3, g_pre, w_in_p, mu_p)


def _rwkv_masks(nb):
    i = np.arange(GROUP)[:, None]
    j = np.arange(GROUP)[None, :]
    n = RWKV_HEAD_DIM
    bd = (i // n) == (j // n)
    mk = bd.astype(np.float32)[None]
    t = np.arange(CHUNK)[:, None]
    sj = j % n
    mc = np.stack([
        sj < t,
        sj <= t,
        sj == t,
        (sj < t) & ((t // 8) == (sj // 8)),
        ((t // 16) == (sj // 16)) & ((t // 8) > (sj // 8)),
        ((t // 32) == (sj // 32)) & ((t // 16) > (sj // 16)),
        (t // 32) > (sj // 32),
    ]).astype(np.float32)
    ti = np.arange(nb * CHUNK)[:, None]
    tj = np.arange(nb * CHUNK)[None, :]
    tri = ((ti // CHUNK == tj // CHUNK) & (tj <= ti)).astype(np.float32)
    return mk, mc, tri


def _rwkv_kernel(r_ref, k_ref, v_ref, lo_ref, vec_ref, wd_ref, wa_ref, wg_ref,
                 mk_ref, mc_ref, tri_ref, e_ref, o_ref, st_ref, *, nb):
    @pl.when(pl.program_id(1) == 0)
    def _reset_state():
        st_ref[...] = jnp.zeros_like(st_ref)

    def rows(ref):
        return jnp.concatenate([ref[bb] for bb in range(nb)], axis=0)

    r = rows(r_ref)
    k = rows(k_ref)
    v = rows(v_ref)
    lo = rows(lo_ref)
    w0 = vec_ref[0:1, :]
    a0 = vec_ref[1:2, :]
    k_k = vec_ref[2:3, :]
    k_a = vec_ref[3:4, :]
    r_k = vec_ref[4:5, :]
    ln_g = vec_ref[5:6, :]
    ln_b = vec_ref[6:7, :]

    z = -(w0 + _bdot(jnp.tanh(lo[:, 0:LANES]), wd_ref[...]))
    softplus = jnp.maximum(z, 0.0) + jnp.log(1.0 + jnp.exp(-jnp.abs(z)))
    lw = -jnp.exp(-softplus - 0.5)
    a = _sigmoid(a0 + _bdot(lo[:, LANES:2 * LANES], wa_ref[...]))
    gate = _bdot(_sigmoid(lo[:, 2 * LANES:4 * LANES]), wg_ref[...])
    kk = k * k_k
    kp = k * (1.0 + (a - 1.0) * k_a)

    tri = tri_ref[...]
    l1 = lw.astype(BF16)
    rem = lw - l1.astype(F32)
    l2 = rem.astype(BF16)
    l3 = (rem - l2.astype(F32)).astype(BF16)
    cum = (jnp.dot(tri, l1, preferred_element_type=F32)
           + jnp.dot(tri, l2, preferred_element_type=F32)
           + jnp.dot(tri, l3, preferred_element_type=F32))
    cum_last = [cum[(bb + 1) * CHUNK - 1:(bb + 1) * CHUNK, :] for bb in range(nb)]
    cum_last_rows = jnp.concatenate(
        [jnp.broadcast_to(cl, (CHUNK, cl.shape[1])) for cl in cum_last], axis=0)
    e_pos = jnp.exp(cum)
    e_pos_x = jnp.exp(cum - lw)
    e_neg = jnp.exp(-cum)
    e_end = jnp.exp(cum_last_rows - cum)

    m_bd = mk_ref[0]
    c_strict = mc_ref[0]
    c_incl = mc_ref[1]
    c_eye = mc_ref[2]
    c_b8 = mc_ref[3].astype(BF16)
    c_levels = [mc_ref[lv].astype(BF16) for lv in (4, 5, 6)]
    ones_bd = e_ref[...]

    def head_sum(x):
        return jnp.dot(x.astype(BF16), ones_bd, preferred_element_type=F32)

    def head_sum_pair(x, y):
        s2 = head_sum(jnp.concatenate([x, y], axis=0))
        return s2[0:CHUNK], s2[CHUNK:]

    def head_sum2(x):
        hi, lo_ = _split2(x)
        s2 = jnp.dot(jnp.concatenate([hi, lo_], axis=0), ones_bd, preferred_element_type=F32)
        return s2[0:CHUNK] + s2[CHUNK:]

    def bd(x):
        return jnp.concatenate([x.astype(BF16)] * HEADS_PER_GROUP, axis=0) * ones_bd

    def fold4(x):
        return (x[0:CHUNK] + x[CHUNK:2 * CHUNK] + x[2 * CHUNK:3 * CHUNK] + x[3 * CHUNK:4 * CHUNK])

    chains = [(bb, gi) for bb in range(nb) for gi in range(N_GROUPS)]
    rsl = [(slice(bb * CHUNK, (bb + 1) * CHUNK), slice(gi * GROUP, (gi + 1) * GROUP))
           for bb, gi in chains]

    def each(fn, *lists):
        return [fn(*args) for args in zip(*lists)]

    def cut(x):
        return [x[rs, sl] for rs, sl in rsl]

    r_g, v_g, kp_g, a_g, kk_g = cut(r), cut(v), cut(kp), cut(a), cut(kk)
    sums = each(lambda x, rg, kg, rsl_: head_sum_pair(x * x, rg * kg * r_k[:, rsl_[1]]),
                kk_g, r_g, kp_g, rsl)
    kkn = each(lambda x, s2: x / jnp.maximum(jnp.sqrt(s2[0]), 1e-12), kk_g, sums)
    ka = each(lambda x, y: x * y, kkn, a_g)
    a_bar = each(lambda x, e: -x * e, kkn, cut(e_pos_x))
    r_bar = each(lambda x, e: x * e, r_g, cut(e_pos))
    b_til = each(lambda x, e: x * e, ka, cut(e_neg))
    k_til = each(lambda x, e: x * e, kp_g, cut(e_neg))
    b_hat = each(lambda x, e: x * e, ka, cut(e_end))
    k_hat = each(lambda x, e: x * e, kp_g, cut(e_end))

    def score_fn(ab, rb, bt, kt):
        lhs = jnp.concatenate([ab, rb], axis=0)
        rhs = jnp.concatenate([bd(bt), bd(kt)], axis=0)
        return lax.dot_general(lhs.astype(BF16), rhs,
                               (((1,), (1,)), ((), ())), preferred_element_type=F32)

    scores = each(score_fn, a_bar, r_bar, b_til, k_til)
    a_ab = each(lambda x: (x[0:CHUNK, 0:GROUP] * c_strict).astype(BF16), scores)
    a_ak = each(lambda x: (x[0:CHUNK, GROUP:] * c_strict).astype(BF16), scores)
    a_rb = each(lambda x: (x[CHUNK:, 0:GROUP] * c_incl).astype(BF16), scores)
    a_rk = each(lambda x: (x[CHUNK:, GROUP:] * c_incl).astype(BF16), scores)

    n1 = each(lambda x: x * c_b8, a_ab)
    n2 = each(lambda x: _bdot(x, bd(x)).astype(BF16), n1)
    sq = each(lambda x, y: _bdot(jnp.concatenate([y, c_eye + x], axis=0), bd(y)), n1, n2)
    inv = each(lambda x, s2: c_eye + x + s2[CHUNK:], n1, sq)
    inv = each(lambda x, s2: x + _bdot(x, bd(s2[0:CHUNK])), inv, sq)
    for c_off in c_levels:
        tmp = each(lambda x, a: _bdot(x, bd(a * c_off)).astype(BF16), inv, a_ab)
        inv = each(lambda x, t: x + _bdot(t, bd(x)), inv, tmp)

    v_bd = each(bd, v_g)
    av = each(lambda ak, rk, vb: _bdot(jnp.concatenate([ak, rk], axis=0), vb),
              a_ak, a_rk, v_bd)
    wu = each(lambda t, ab, x: _bdot(t, jnp.concatenate([bd(ab), bd(x[0:CHUNK])], axis=1)),
              inv, a_bar, av)
    ry = each(lambda a, x: _bdot(a, jnp.concatenate([bd(x[:, 0:GROUP]), bd(x[:, GROUP:])], axis=1)),
              a_rb, wu)
    r_hat = each(lambda x, y: x + y[:, 0:GROUP], r_bar, ry)
    y_hat = each(lambda y, x: y[:, GROUP:] + x[CHUNK:], ry, av)

    state = [st_ref[bb, gi] for bb, gi in chains]
    y = each(lambda rh, st, yh: lax.dot_general(
        rh.astype(BF16), bd(st), (((1,), (1,)), ((), ())),
        preferred_element_type=F32) + yh, r_hat, state, y_hat)

    def trans_fn(bh, kh, x, vg):
        vb = vg.astype(BF16)
        lhs_t = jnp.concatenate(
            [x.astype(BF16), jnp.concatenate([jnp.zeros_like(vb), vb], axis=1)], axis=0)
        rhs_t = jnp.concatenate([bh, kh], axis=0)
        return lax.dot_general(lhs_t, rhs_t.astype(BF16),
                               (((0,), (0,)), ((), ())), preferred_element_type=F32)

    mc = each(trans_fn, b_hat, k_hat, wu, v_g)
    for (bb, gi), (_, sl), mci, st in zip(chains, rsl, mc, state):
        st_ref[bb, gi] = (st * jnp.exp(cum_last[bb][:, sl])
                          + _bdot(st, mci[0:GROUP, :] * m_bd) + fold4(mci[GROUP:, :] * m_bd))

    inv_n = 1.0 / RWKV_HEAD_DIM
    mean = each(lambda x: head_sum2(x) * inv_n, y)
    dlt = each(lambda x, m: x - m, y, mean)
    var = each(lambda x: head_sum(x * x) * inv_n, dlt)
    bonus = each(lambda s2, vg: s2[1] * vg, sums, v_g)
    for (bb, gi), (rs, sl), d, vr, bo in zip(chains, rsl, dlt, var, bonus):
        yn = d * lax.rsqrt(vr + GN_EPS) * ln_g[:, sl] + ln_b[:, sl]
        o_ref[bb, :, sl] = ((yn + bo) * gate[rs, sl]).astype(o_ref.dtype)


def _rwkv_call(p3, vecs, wd, wa, wg, nb):
    b, s, _ = p3.shape
    w = RWKV_WIDTH
    mk, mc, tri = _rwkv_masks(nb)
    mk = jnp.asarray(mk)
    mc = jnp.asarray(mc)
    tri = jnp.asarray(tri, dtype=BF16)
    ones_bd = mk[0].astype(BF16)
    const2 = lambda bi, ci: (0, 0)
    const3 = lambda bi, ci: (0, 0, 0)
    return pl.pallas_call(
        functools.partial(_rwkv_kernel, nb=nb),
        grid=(b // nb, s // CHUNK),
        in_specs=[pl.BlockSpec((nb, CHUNK, w), lambda bi, ci: (bi, ci, 0)),
                  pl.BlockSpec((nb, CHUNK, w), lambda bi, ci: (bi, ci, 1)),
                  pl.BlockSpec((nb, CHUNK, w), lambda bi, ci: (bi, ci, 2)),
                  pl.BlockSpec((nb, CHUNK, 4 * LANES), lambda bi, ci: (bi, ci, 3 * w // (4 * LANES))),
                  pl.BlockSpec(vecs.shape, const2),
                  pl.BlockSpec(wd.shape, const2),
                  pl.BlockSpec(wa.shape, const2),
                  pl.BlockSpec(wg.shape, const2),
                  pl.BlockSpec(mk.shape, const3),
                  pl.BlockSpec(mc.shape, const3),
                  pl.BlockSpec(tri.shape, const2),
                  pl.BlockSpec(ones_bd.shape, const2)],
        out_specs=pl.BlockSpec((nb, CHUNK, w), lambda bi, ci: (bi, ci, 0)),
        out_shape=jax.ShapeDtypeStruct((b, s, w), BF16),
        scratch_shapes=[pltpu.VMEM((nb, N_GROUPS, CHUNK, GROUP), F32)],
        compiler_params=pltpu.CompilerParams(dimension_semantics=("arbitrary", "arbitrary"),
                                             vmem_limit_bytes=VMEM_LIMIT),
        name="rwkv",
    )(p3, p3, p3, p3, vecs, wd, wa, wg, mk, mc, tri, ones_bd)


def _mlaproj_kernel(pq_ref, pkv_ref, pos_ref, invf_ref, gq_ref, gkv_ref, wq_ref, wkv_ref,
                    q_ref, k_ref, v_ref):
    q = _bdot(_rms(pq_ref[...], gq_ref[...]), wq_ref[...])
    pkv = pkv_ref[...]
    kv = _bdot(_rms(pkv[:, 0:KV_LORA_RANK], gkv_ref[...]), wkv_ref[...])
    ang = pos_ref[...].astype(F32) * invf_ref[...]
    lane = lax.broadcasted_iota(jnp.int32, ang.shape, 1)
    half = QK_ROPE_DIM // 2
    cos_f = jnp.where(lane < QK_ROPE_DIM, jnp.cos(ang), 0.0)
    sin = jnp.sin(ang)
    sin_f = jnp.where(lane < half, -sin, jnp.where(lane < QK_ROPE_DIM, sin, 0.0))
    k_rot = pkv[:, KV_LORA_RANK:KV_LORA_RANK + LANES] * cos_f + pkv[:, KV_LORA_RANK + LANES:] * sin_f
    scale = (QK_NOPE_DIM + QK_ROPE_DIM) ** -0.5 * 1.4426950408889634
    swap0 = MLA_HEADS * QK_PAD
    ones_col = (lane == 0).astype(F32)
    for h in range(MLA_HEADS):
        q_nope = q[:, h * QK_PAD:h * QK_PAD + LANES]
        q_rot = (q[:, h * QK_PAD + LANES:(h + 1) * QK_PAD] * cos_f
                 + q[:, swap0 + h * LANES:swap0 + (h + 1) * LANES] * sin_f)
        q_ref[0, h] = (jnp.concatenate([q_nope, q_rot], axis=1) * scale).astype(q_ref.dtype)
        k_ref[0, h] = jnp.concatenate([kv[:, h * LANES:(h + 1) * LANES], k_rot], axis=1).astype(k_ref.dtype)
        v_ref[0, h] = jnp.concatenate(
            [kv[:, (MLA_HEADS + h) * LANES:(MLA_HEADS + h + 1) * LANES], ones_col],
            axis=1).astype(v_ref.dtype)


def _mlaproj_call(p2, pos2, invf, gq, gkv, wq, wkv, batch, seq, tm):
    t = p2.shape[0]
    tps = seq // tm
    blk = 4 * LANES
    q_blk = (IN_COLS_PAD - 2 * blk) // blk
    hd = MLA_HEADS
    out_map = lambda i: (i // tps, 0, i % tps, 0)
    const = lambda i: (0, 0)
    return pl.pallas_call(
        _mlaproj_kernel,
        grid=(t // tm,),
        in_specs=[pl.BlockSpec((tm, blk), lambda i: (i, q_blk)),
                  pl.BlockSpec((tm, blk), lambda i: (i, q_blk + 1)),
                  pl.BlockSpec((tm, 1), lambda i: (i, 0)),
                  pl.BlockSpec(invf.shape, const),
                  pl.BlockSpec(gq.shape, const),
                  pl.BlockSpec(gkv.shape, const),
                  pl.BlockSpec(wq.shape, const),
                  pl.BlockSpec(wkv.shape, const)],
        out_specs=[pl.BlockSpec((1, hd, tm, QK_PAD), out_map),
                   pl.BlockSpec((1, hd, tm, QK_PAD), out_map),
                   pl.BlockSpec((1, hd, tm, V_PAD), out_map)],
        out_shape=[jax.ShapeDtypeStruct((batch, hd, seq, QK_PAD), BF16),
                   jax.ShapeDtypeStruct((batch, hd, seq, QK_PAD), BF16),
                   jax.ShapeDtypeStruct((batch, hd, seq, V_PAD), BF16)],
        compiler_params=pltpu.CompilerParams(dimension_semantics=("arbitrary",),
                                             vmem_limit_bytes=VMEM_LIMIT),
        name="mlaproj",
    )(p2, p2, pos2, invf, gq, gkv, wq, wkv)


ATTN_ROW_PARTS = 4
ATTN_SM_ROWS = 32
ATTN_HEADS_PER_STEP = 2


def _attn_kernel(q_ref, k_ref, v_ref, o_ref, s_ref, p_ref, m_ref, a_ref, acc_ref, *, tq):
    i = pl.program_id(2)
    rp = tq // ATTN_ROW_PARTS
    heads = range(ATTN_HEADS_PER_STEP)
    m_ref[...] = jnp.full(m_ref.shape, -jnp.inf, F32)
    acc_ref[...] = jnp.zeros_like(acc_ref)

    def scores(hh, start, part, nk):
        rows = slice(part * rp, (part + 1) * rp)
        s_ref[hh, rows, 0:nk] = lax.dot_general(
            q_ref[0, hh, rows, :], k_ref[0, hh, pl.ds(start, nk), :],
            (((1,), (1,)), ((), ())), preferred_element_type=F32)

    def block(start, next_start, on_diagonal):
        def keys(part):
            return (part + 1) * rp if on_diagonal else tq

        def softmax(hh, part):
            nk = keys(part)
            for c in range(rp // ATTN_SM_ROWS):
                r0 = part * rp + c * ATTN_SM_ROWS
                rows = slice(r0, r0 + ATTN_SM_ROWS)
                sc = s_ref[hh, rows, 0:nk]
                if on_diagonal:
                    row = (r0 + lax.broadcasted_iota(jnp.int32, sc.shape, 0)) // CHUNK
                    col = lax.broadcasted_iota(jnp.int32, sc.shape, 1) // CHUNK
                    sc = jnp.where(row >= col, sc, -jnp.inf)
                m_old = m_ref[hh, rows, :]
                m_new = jnp.maximum(m_old, jnp.max(sc, axis=-1, keepdims=True))
                m_ref[hh, rows, :] = m_new
                a_ref[hh, rows, :] = jnp.exp2(m_old - m_new)
                p_ref[hh, rows, 0:nk] = jnp.exp2(sc - m_new).astype(BF16)

        def weighted_values(hh, part):
            rows = slice(part * rp, (part + 1) * rp)
            nk = keys(part)
            acc_ref[hh, rows, :] = a_ref[hh, rows, :] * acc_ref[hh, rows, :] + jnp.dot(
                p_ref[hh, rows, 0:nk], v_ref[0, hh, pl.ds(start, nk), :], preferred_element_type=F32)

        for part in range(ATTN_ROW_PARTS):
            for hh in heads:
                if part + 1 < ATTN_ROW_PARTS:
                    scores(hh, start, part + 1, keys(part + 1))
                elif not on_diagonal:
                    scores(hh, next_start, 0, tq)
            for hh in heads:
                softmax(hh, part)
                weighted_values(hh, part)

    def body(j, carry):
        block(pl.multiple_of(j * tq, tq), pl.multiple_of((j + 1) * tq, tq), False)
        return carry

    for hh in heads:
        scores(hh, 0, 0, tq)
    lax.fori_loop(0, i, body, 0)
    block(pl.multiple_of(i * tq, tq), None, True)
    for hh in heads:
        acc = acc_ref[hh]
        o_ref[0, :, hh * V_HEAD_DIM:(hh + 1) * V_HEAD_DIM] = (
            acc[:, 0:V_HEAD_DIM] / acc[:, V_HEAD_DIM:V_HEAD_DIM + 1]).astype(o_ref.dtype)


def _attn_call(q4, k4, v4, tq):
    b, h, s, _ = q4.shape
    hp = ATTN_HEADS_PER_STEP
    return pl.pallas_call(
        functools.partial(_attn_kernel, tq=tq),
        grid=(b, h // hp, s // tq),
        in_specs=[pl.BlockSpec((1, hp, tq, QK_PAD), lambda bi, hi, i: (bi, hi, i, 0)),
                  pl.BlockSpec((1, hp, s, QK_PAD), lambda bi, hi, i: (bi, hi, 0, 0)),
                  pl.BlockSpec((1, hp, s, V_PAD), lambda bi, hi, i: (bi, hi, 0, 0))],
        out_specs=pl.BlockSpec((1, tq, hp * V_HEAD_DIM), lambda bi, hi, i: (bi, i, hi)),
        out_shape=jax.ShapeDtypeStruct((b, s, h * V_HEAD_DIM), BF16),
        scratch_shapes=[pltpu.VMEM((hp, tq, tq), F32), pltpu.VMEM((hp, tq, tq), BF16),
                        pltpu.VMEM((hp, tq, 1), F32), pltpu.VMEM((hp, tq, 1), F32),
                        pltpu.VMEM((hp, tq, V_PAD), F32)],
        compiler_params=pltpu.CompilerParams(
            dimension_semantics=("arbitrary", "arbitrary", "arbitrary"),
            vmem_limit_bytes=VMEM_LIMIT),
        name="attn",
    )(q4, k4, v4)


NORM_ROWS = 32


def _outproj_kernel(yr_ref, ym_ref, x_ref, mod_ref, g_ref, gffn_ref, w_ref, o_ref, h_ref):
    half = RWKV_WIDTH
    o = (jnp.dot(yr_ref[...], w_ref[0:half, :], preferred_element_type=F32)
         + jnp.dot(ym_ref[...], w_ref[half:, :], preferred_element_type=F32))
    x_mid = x_ref[...] + mod_ref[0, 2:3, :] * _rms(o, g_ref[...])
    o_ref[...] = x_mid
    h = _rms(x_mid, gffn_ref[...]) * (1.0 + mod_ref[0, 4:5, :]) + mod_ref[0, 3:4, :]
    h_ref[...] = h.astype(h_ref.dtype)


def _outproj_call(yr, ym, x2, mod3, g_post, g_pre_ffn, w_out, seq, tm):
    t, d = x2.shape
    tps = seq // tm
    return pl.pallas_call(
        _outproj_kernel,
        grid=(t // tm,),
        in_specs=[pl.BlockSpec((tm, RWKV_WIDTH), lambda i: (i, 0)),
                  pl.BlockSpec((tm, MLA_WIDTH), lambda i: (i, 0)),
                  pl.BlockSpec((tm, d), lambda i: (i, 0)),
                  pl.BlockSpec((1, N_MOD, d), lambda i: (i // tps, 0, 0)),
                  pl.BlockSpec((1, d), lambda i: (0, 0)),
                  pl.BlockSpec((1, d), lambda i: (0, 0)),
                  pl.BlockSpec(w_out.shape, lambda i: (0, 0))],
        out_specs=[pl.BlockSpec((tm, d), lambda i: (i, 0)),
                   pl.BlockSpec((tm, d), lambda i: (i, 0))],
        out_shape=[jax.ShapeDtypeStruct((t, d), F32), jax.ShapeDtypeStruct((t, d), BF16)],
        compiler_params=pltpu.CompilerParams(dimension_semantics=("arbitrary",),
                                             vmem_limit_bytes=VMEM_LIMIT),
        name="outproj",
    )(yr, ym, x2, mod3, g_post, g_pre_ffn, w_out)


FFN_ROWS = 128
FFN_COL_SPLIT = 1


def _gelu_tanh(x):
    c = 0.7978845608028654
    return 0.5 * x * (1.0 + jnp.tanh(c * (x + 0.044715 * (x * x * x))))


def _ffn_kernel(h_ref, hh_ref, x_ref, mod_ref, wg_ref, wv_ref, cwg_ref, cwv_ref,
                cbg_ref, cbv_ref, wd_ref, gpost_ref, o_ref, hcat_ref, acc_ref, ug_ref, uv_ref, act_ref,
                *, tiles_per_seq):
    i = pl.program_id(0)
    j = pl.program_id(1)
    tm, tf = act_ref.shape
    w = tf // FFN_COL_SPLIT

    @pl.when(j == 0)
    def _prologue():
        keep = (i % tiles_per_seq) != 0
        hcat_ref[0:HALO, :] = jnp.where(keep, hh_ref[...], jnp.zeros_like(hh_ref))
        hcat_ref[HALO:, :] = h_ref[...]
        acc_ref[...] = jnp.zeros_like(acc_ref)

    hb = hcat_ref[...]

    def up(c):
        cs = slice(c * w, (c + 1) * w)
        ug_ref[:, cs] = jnp.dot(hb, wg_ref[:, cs], preferred_element_type=F32)
        uv_ref[:, cs] = jnp.dot(hb, wv_ref[:, cs], preferred_element_type=F32)

    def conv(u_ref, r0, cs, cw_ref, cb_ref):
        return (cb_ref[:, cs] + cw_ref[2:3, cs] * u_ref[pl.ds(r0, FFN_ROWS), cs]
                + cw_ref[1:2, cs] * u_ref[pl.ds(r0 - 1, FFN_ROWS), cs]
                + cw_ref[0:1, cs] * u_ref[pl.ds(r0 - 2, FFN_ROWS), cs])

    def activate(c):
        cs = slice(c * w, (c + 1) * w)
        for rc in range(tm // FFN_ROWS):
            r0 = HALO + rc * FFN_ROWS
            act = (_gelu_tanh(conv(ug_ref, r0, cs, cwg_ref, cbg_ref))
                   * conv(uv_ref, r0, cs, cwv_ref, cbv_ref))
            act_ref[rc * FFN_ROWS:(rc + 1) * FFN_ROWS, cs] = act.astype(BF16)

    def down(c):
        cs = slice(c * w, (c + 1) * w)
        return jnp.dot(act_ref[:, cs], wd_ref[cs, :], preferred_element_type=F32)

    up(0)
    total = None
    for c in range(FFN_COL_SPLIT):
        if c + 1 < FFN_COL_SPLIT:
            up(c + 1)
        activate(c)
        part = down(c)
        total = part if total is None else total + part
    acc_ref[...] += total

    @pl.when(j == pl.num_programs(1) - 1)
    def _epilogue():
        for c in range(tm // NORM_ROWS):
            rows = slice(c * NORM_ROWS, (c + 1) * NORM_ROWS)
            o_ref[rows, :] = x_ref[rows, :] + mod_ref[0, 5:6, :] * _rms(acc_ref[rows, :], gpost_ref[...])


def _ffn_call(h2, x2, mod3, w_up, conv_w, conv_b, w_down, g_post, seq, tm, tf):
    t, d = x2.shape
    tps = seq // tm
    hb = tm // HALO
    nf = D_FF // tf
    return pl.pallas_call(
        functools.partial(_ffn_kernel, tiles_per_seq=tps),
        grid=(t // tm, nf),
        in_specs=[pl.BlockSpec((tm, d), lambda i, j: (i, 0)),
                  pl.BlockSpec((HALO, d), lambda i, j: (jnp.maximum(i * hb - 1, 0), 0)),
                  pl.BlockSpec((tm, d), lambda i, j: (i, 0)),
                  pl.BlockSpec((1, N_MOD, d), lambda i, j: (i // tps, 0, 0)),
                  pl.BlockSpec((d, tf), lambda i, j: (0, j)),
                  pl.BlockSpec((d, tf), lambda i, j: (0, nf + j)),
                  pl.BlockSpec((3, tf), lambda i, j: (0, j)),
                  pl.BlockSpec((3, tf), lambda i, j: (0, nf + j)),
                  pl.BlockSpec((1, tf), lambda i, j: (0, j)),
                  pl.BlockSpec((1, tf), lambda i, j: (0, nf + j)),
                  pl.BlockSpec((tf, d), lambda i, j: (j, 0)),
                  pl.BlockSpec((1, d), lambda i, j: (0, 0))],
        out_specs=pl.BlockSpec((tm, d), lambda i, j: (i, 0)),
        out_shape=jax.ShapeDtypeStruct((t, d), F32),
        scratch_shapes=[pltpu.VMEM((tm + HALO, d), BF16), pltpu.VMEM((tm, d), F32),
                        pltpu.VMEM((tm + HALO, tf), F32), pltpu.VMEM((tm + HALO, tf), F32),
                        pltpu.VMEM((tm, tf), BF16)],
        compiler_params=pltpu.CompilerParams(dimension_semantics=("arbitrary", "arbitrary"),
                                             vmem_limit_bytes=VMEM_LIMIT),
        name="ffn",
    )(h2, h2, x2, mod3, w_up, w_up, conv_w, conv_w, conv_b, conv_b, w_down, g_post)


def _pad_cols(w, n):
    return jnp.pad(w, ((0, 0), (0, n - w.shape[1])))


def _layout_w_in(w_in, mu_shift):
    w3 = 3 * RWKV_WIDTH
    o_wd = w3
    o_ad = o_wd + DECAY_LORA
    o_gd = o_ad + ICLR_LORA
    o_q = o_gd + GATE_LORA
    o_kv = o_q + Q_LORA_RANK
    o_kr = o_kv + KV_LORA_RANK
    half = QK_ROPE_DIM // 2

    def lay(m):
        kr = m[:, o_kr:o_kr + QK_ROPE_DIM]
        return jnp.concatenate([
            m[:, 0:w3],
            _pad_cols(m[:, o_wd:o_ad], LANES),
            _pad_cols(m[:, o_ad:o_gd], LANES),
            m[:, o_gd:o_q],
            m[:, o_q:o_kv],
            m[:, o_kv:o_kr],
            _pad_cols(kr, LANES),
            _pad_cols(jnp.concatenate([kr[:, half:], kr[:, :half]], axis=1), LANES),
        ], axis=1)

    mu_full = jnp.concatenate([mu_shift, jnp.zeros((w_in.shape[1] - mu_shift.shape[0],), F32)])
    w_p = lay(w_in).astype(BF16)
    mu_p = lay(mu_full[None, :])
    return w_p, mu_p


def _layout_w_q(w_q_up):
    dn, dr = QK_NOPE_DIM, QK_ROPE_DIM
    half = dr // 2
    w = w_q_up.reshape(Q_LORA_RANK, MLA_HEADS, dn + dr)
    nope, u1, u2 = w[..., :dn], w[..., dn:dn + half], w[..., dn + half:]
    z = jnp.zeros(u1.shape[:-1] + (LANES - dr,), w.dtype)
    main = jnp.concatenate([nope, u1, u2, z], axis=-1).reshape(Q_LORA_RANK, MLA_HEADS * QK_PAD)
    swap = jnp.concatenate([u2, u1, z], axis=-1).reshape(Q_LORA_RANK, MLA_HEADS * LANES)
    return jnp.concatenate([main, swap], axis=1).astype(BF16)


def _layout_w_kv(w_kv_up):
    w = w_kv_up.reshape(KV_LORA_RANK, MLA_HEADS, QK_NOPE_DIM + V_HEAD_DIM)
    kn = w[..., :QK_NOPE_DIM].reshape(KV_LORA_RANK, MLA_HEADS * QK_NOPE_DIM)
    vv = w[..., QK_NOPE_DIM:].reshape(KV_LORA_RANK, MLA_HEADS * V_HEAD_DIM)
    return jnp.concatenate([kn, vv], axis=1).astype(BF16)


def _pad_rows(w, n):
    return jnp.pad(w, ((0, n - w.shape[0]), (0, 0)))


def _block(x, c, positions, w_mod, b_mod, g_pre_mix, w_in, mu_shift, w0, w_decay_up, a0,
           w_iclr_up, w_gate_up, k_k, k_a, r_k, ln_x_g, ln_x_b, q_norm_g, w_q_up, kv_norm_g,
           w_kv_up, w_out, g_post_mix, g_pre_ffn, w_ffn_up, conv_w, conv_b, w_ffn_down,
           g_post_ffn, *, tm_in, tn_in, nb_rwkv, tm_mla, tq, tm_out, tm_ffn, tf):
    b, s, d = x.shape
    t = b * s
    x2 = x.reshape(t, d)

    c_pad = jnp.pad(c, ((0, 8 - b % 8 if b % 8 else 0), (0, 0)))
    mod = _mod_call(c_pad, w_mod, b_mod[None, :])[:b]
    mod3 = mod.reshape(b, N_MOD, d)

    w_in_p, mu_p = _layout_w_in(w_in, mu_shift)
    p2 = _inproj_call(x2, mod3, g_pre_mix[None, :], w_in_p, mu_p, s, tm_in, tn_in)

    vecs = jnp.stack([w0, a0, k_k, k_a, r_k.reshape(-1), ln_x_g, ln_x_b, jnp.zeros_like(w0)])
    y_rwkv = _rwkv_call(p2.reshape(b, s, IN_COLS_PAD), vecs,
                        _pad_rows(w_decay_up, LANES).astype(BF16),
                        _pad_rows(w_iclr_up, LANES).astype(BF16),
                        w_gate_up.astype(BF16), nb_rwkv)

    half = QK_ROPE_DIM // 2
    inv_freq = ROPE_THETA ** (-jnp.arange(0, QK_ROPE_DIM, 2, dtype=F32) / QK_ROPE_DIM)
    invf = jnp.concatenate([inv_freq, inv_freq, jnp.zeros((LANES - 2 * half,), F32)])[None, :]
    q4, k4, v4 = _mlaproj_call(p2, positions.reshape(t, 1), invf, q_norm_g[None, :],
                               kv_norm_g[None, :], _layout_w_q(w_q_up), _layout_w_kv(w_kv_up),
                               b, s, tm_mla)
    y_mla = _attn_call(q4, k4, v4, tq)

    x_mid, h_ffn = _outproj_call(y_rwkv.reshape(t, RWKV_WIDTH), y_mla.reshape(t, MLA_WIDTH), x2,
                                 mod3, g_post_mix[None, :], g_pre_ffn[None, :],
                                 w_out.astype(BF16), s, tm_out)
    out = _ffn_call(h_ffn, x_mid, mod3, w_ffn_up.astype(BF16), conv_w, conv_b[None, :],
                    w_ffn_down.astype(BF16), g_post_ffn[None, :], s, tm_ffn, tf)
    return out.reshape(b, s, d)


def kernel(x, c, positions, w_mod, b_mod, g_pre_mix, w_in, mu_shift, w0, w_decay_up, a0, w_iclr_up, w_gate_up, k_k, k_a, r_k, ln_x_g, ln_x_b, q_norm_g, w_q_up, kv_norm_g, w_kv_up, w_out, g_post_mix, g_pre_ffn, w_ffn_up, conv_w, conv_b, w_ffn_down, g_post_ffn):
    args = (x, c, positions, w_mod, b_mod, g_pre_mix, w_in, mu_shift, w0, w_decay_up, a0,
            w_iclr_up, w_gate_up, k_k, k_a, r_k, ln_x_g, ln_x_b, q_norm_g, w_q_up, kv_norm_g,
            w_kv_up, w_out, g_post_mix, g_pre_ffn, w_ffn_up, conv_w, conv_b, w_ffn_down,
            g_post_ffn)
    for l in range(w_mod.shape[0]):
        layer = [a[l] for a in args[3:]]
        x = _block(x, c, positions, *layer, tm_in=512, tn_in=1536, nb_rwkv=2, tm_mla=512, tq=1024,
                   tm_out=512, tm_ffn=512, tf=512)
    return x
```

```python
import functools

import numpy as np
import jax
import jax.numpy as jnp
from jax import lax
from jax.experimental import pallas as pl
from jax.experimental.pallas import tpu as pltpu

F32 = jnp.float32
BF16 = jnp.bfloat16

D_MODEL = 2048
CHUNK = 64
RWKV_HEAD_DIM = 64
RWKV_WIDTH = 1024
RWKV_HEADS = 16
DECAY_LORA = 96
ICLR_LORA = 96
GATE_LORA = 256
GN_EPS = 64e-5
QK_NOPE_DIM = 128
QK_ROPE_DIM = 64
V_HEAD_DIM = 128
MLA_WIDTH = 1024
MLA_HEADS = 8
Q_LORA_RANK = 512
KV_LORA_RANK = 256
ROPE_THETA = 10000.0
D_FF = 5632
NORM_EPS = 1e-6
N_MOD = 6

LANES = 128
HALO = 16
GROUP = 256
HEADS_PER_GROUP = GROUP // RWKV_HEAD_DIM
N_GROUPS = RWKV_WIDTH // GROUP
QK_PAD = 256
V_PAD = 256
IN_COLS_PAD = 4608
VMEM_LIMIT = 56 * 1024 * 1024


def _bdot(a, b):
    return jnp.dot(a.astype(BF16), b.astype(BF16), preferred_element_type=F32)


def _split2(x):
    hi = x.astype(BF16)
    lo = (x - hi.astype(F32)).astype(BF16)
    return hi, lo


def _sigmoid(x):
    return 1.0 / (1.0 + jnp.exp(-x))


def _rms(xv, g):
    ms = jnp.mean(xv * xv, axis=-1, keepdims=True)
    return xv * lax.rsqrt(ms + NORM_EPS) * g


def _mod_kernel(c_ref, w_ref, b_ref, o_ref):
    c = c_ref[...]
    s = c * _sigmoid(c)
    o_ref[...] = _bdot(s, w_ref[...]) + b_ref[...]


def _mod_call(c_pad, w_mod, b_mod, tn=1024):
    rows, d = c_pad.shape
    n = w_mod.shape[1]
    return pl.pallas_call(
        _mod_kernel,
        grid=(n // tn,),
        in_specs=[pl.BlockSpec((rows, d), lambda j: (0, 0)),
                  pl.BlockSpec((d, tn), lambda j: (0, j)),
                  pl.BlockSpec((1, tn), lambda j: (0, j))],
        out_specs=pl.BlockSpec((rows, tn), lambda j: (0, j)),
        out_shape=jax.ShapeDtypeStruct((rows, n), F32),
        compiler_params=pltpu.CompilerParams(dimension_semantics=("arbitrary",),
                                             vmem_limit_bytes=VMEM_LIMIT),
        name="mod",
    )(c_pad, w_mod, b_mod)


def _inproj_kernel(x_ref, xh_ref, mod_ref, g_ref, w_ref, mu_ref, o_ref, *, tiles_per_seq):
    i = pl.program_id(1)
    first = (i % tiles_per_seq) == 0
    shift = mod_ref[0, 0:1, :]
    scale = mod_ref[0, 1:2, :]
    g = g_ref[...]
    h = _rms(x_ref[...], g) * (1.0 + scale) + shift
    hh = _rms(xh_ref[...], g) * (1.0 + scale) + shift
    hh = jnp.where(first, 0.0, hh)
    hcat = jnp.concatenate([hh, h], axis=0).astype(BF16)
    p = jnp.dot(hcat, w_ref[...], preferred_element_type=F32)
    prev = pltpu.roll(p, 1, axis=0)
    out = p + (prev - p) * mu_ref[...]
    o_ref[...] = out[HALO:, :]


def _inproj_call(x2, mod3, g_pre, w_in_p, mu_p, seq, tm, tn):
    t, d = x2.shape
    n = w_in_p.shape[1]
    tps = seq // tm
    hb = tm // HALO
    return pl.pallas_call(
        functools.partial(_inproj_kernel, tiles_per_seq=tps),
        grid=(n // tn, t // tm),
        in_specs=[pl.BlockSpec((tm, d), lambda j, i: (i, 0)),
                  pl.BlockSpec((HALO, d), lambda j, i: (jnp.maximum(i * hb - 1, 0), 0)),
                  pl.BlockSpec((1, N_MOD, d), lambda j, i: (i // tps, 0, 0)),
                  pl.BlockSpec((1, d), lambda j, i: (0, 0)),
                  pl.BlockSpec((d, tn), lambda j, i: (0, j)),
                  pl.BlockSpec((1, tn), lambda j, i: (0, j))],
        out_specs=pl.BlockSpec((tm, tn), lambda j, i: (i, j)),
        out_shape=jax.ShapeDtypeStruct((t, n), F32),
        compiler_params=pltpu.CompilerParams(dimension_semantics=("arbitrary", "arbitrary"),
                                             vmem_limit_bytes=VMEM_LIMIT),
        name="inproj",
    )(x2, x2, mod3, g_pre, w_in_p, mu_p)


def _rwkv_masks(nb):
    i = np.arange(GROUP)[:, None]
    j = np.arange(GROUP)[None, :]
    n = RWKV_HEAD_DIM
    bd = (i // n) == (j // n)
    mk = bd.astype(np.float32)[None]
    t = np.arange(CHUNK)[:, None]
    sj = j % n
    mc = np.stack([
        sj < t,
        sj <= t,
        sj == t,
        (sj < t) & ((t // 8) == (sj // 8)),
        ((t // 16) == (sj // 16)) & ((t // 8) > (sj // 8)),
        ((t // 32) == (sj // 32)) & ((t // 16) > (sj // 16)),
        (t // 32) > (sj // 32),
    ]).astype(np.float32)
    ti = np.arange(nb * CHUNK)[:, None]
    tj = np.arange(nb * CHUNK)[None, :]
    tri = ((ti // CHUNK == tj // CHUNK) & (tj <= ti)).astype(np.float32)
    return mk, mc, tri


def _rwkv_kernel(r_ref, k_ref, v_ref, lo_ref, vec_ref, wd_ref, wa_ref, wg_ref,
                 mk_ref, mc_ref, tri_ref, e_ref, o_ref, st_ref, *, nb):
    @pl.when(pl.program_id(1) == 0)
    def _reset_state():
        st_ref[...] = jnp.zeros_like(st_ref)

    def rows(ref):
        return jnp.concatenate([ref[bb] for bb in range(nb)], axis=0)

    r = rows(r_ref)
    k = rows(k_ref)
    v = rows(v_ref)
    lo = rows(lo_ref)
    w0 = vec_ref[0:1, :]
    a0 = vec_ref[1:2, :]
    k_k = vec_ref[2:3, :]
    k_a = vec_ref[3:4, :]
    r_k = vec_ref[4:5, :]
    ln_g = vec_ref[5:6, :]
    ln_b = vec_ref[6:7, :]

    z = -(w0 + _bdot(jnp.tanh(lo[:, 0:LANES]), wd_ref[...]))
    softplus = jnp.maximum(z, 0.0) + jnp.log(1.0 + jnp.exp(-jnp.abs(z)))
    lw = -jnp.exp(-softplus - 0.5)
    a = _sigmoid(a0 + _bdot(lo[:, LANES:2 * LANES], wa_ref[...]))
    gate = _bdot(_sigmoid(lo[:, 2 * LANES:4 * LANES]), wg_ref[...])
    kk = k * k_k
    kp = k * (1.0 + (a - 1.0) * k_a)

    tri = tri_ref[...]
    l1 = lw.astype(BF16)
    rem = lw - l1.astype(F32)
    l2 = rem.astype(BF16)
    l3 = (rem - l2.astype(F32)).astype(BF16)
    cum = (jnp.dot(tri, l1, preferred_element_type=F32)
           + jnp.dot(tri, l2, preferred_element_type=F32)
           + jnp.dot(tri, l3, preferred_element_type=F32))
    cum_last = [cum[(bb + 1) * CHUNK - 1:(bb + 1) * CHUNK, :] for bb in range(nb)]
    cum_last_rows = jnp.concatenate(
        [jnp.broadcast_to(cl, (CHUNK, cl.shape[1])) for cl in cum_last], axis=0)
    e_pos = jnp.exp(cum)
    e_pos_x = jnp.exp(cum - lw)
    e_neg = jnp.exp(-cum)
    e_end = jnp.exp(cum_last_rows - cum)

    m_bd = mk_ref[0]
    c_strict = mc_ref[0]
    c_incl = mc_ref[1]
    c_eye = mc_ref[2]
    c_b8 = mc_ref[3].astype(BF16)
    c_levels = [mc_ref[lv].astype(BF16) for lv in (4, 5, 6)]
    ones_bd = e_ref[...]

    def head_sum(x):
        return jnp.dot(x.astype(BF16), ones_bd, preferred_element_type=F32)

    def head_sum_pair(x, y):
        s2 = head_sum(jnp.concatenate([x, y], axis=0))
        return s2[0:CHUNK], s2[CHUNK:]

    def head_sum2(x):
        hi, lo_ = _split2(x)
        s2 = jnp.dot(jnp.concatenate([hi, lo_], axis=0), ones_bd, preferred_element_type=F32)
        return s2[0:CHUNK] + s2[CHUNK:]

    def bd(x):
        return jnp.concatenate([x.astype(BF16)] * HEADS_PER_GROUP, axis=0) * ones_bd

    def fold4(x):
        return (x[0:CHUNK] + x[CHUNK:2 * CHUNK] + x[2 * CHUNK:3 * CHUNK] + x[3 * CHUNK:4 * CHUNK])

    chains = [(bb, gi) for bb in range(nb) for gi in range(N_GROUPS)]
    rsl = [(slice(bb * CHUNK, (bb + 1) * CHUNK), slice(gi * GROUP, (gi + 1) * GROUP))
           for bb, gi in chains]

    def each(fn, *lists):
        return [fn(*args) for args in zip(*lists)]

    def cut(x):
        return [x[rs, sl] for rs, sl in rsl]

    r_g, v_g, kp_g, a_g, kk_g = cut(r), cut(v), cut(kp), cut(a), cut(kk)
    sums = each(lambda x, rg, kg, rsl_: head_sum_pair(x * x, rg * kg * r_k[:, rsl_[1]]),
                kk_g, r_g, kp_g, rsl)
    kkn = each(lambda x, s2: x / jnp.maximum(jnp.sqrt(s2[0]), 1e-12), kk_g, sums)
    ka = each(lambda x, y: x * y, kkn, a_g)
    a_bar = each(lambda x, e: -x * e, kkn, cut(e_pos_x))
    r_bar = each(lambda x, e: x * e, r_g, cut(e_pos))
    b_til = each(lambda x, e: x * e, ka, cut(e_neg))
    k_til = each(lambda x, e: x * e, kp_g, cut(e_neg))
    b_hat = each(lambda x, e: x * e, ka, cut(e_end))
    k_hat = each(lambda x, e: x * e, kp_g, cut(e_end))

    def score_fn(ab, rb, bt, kt):
        lhs = jnp.concatenate([ab, rb], axis=0)
        rhs = jnp.concatenate([bd(bt), bd(kt)], axis=0)
        return lax.dot_general(lhs.astype(BF16), rhs,
                               (((1,), (1,)), ((), ())), preferred_element_type=F32)

    scores = each(score_fn, a_bar, r_bar, b_til, k_til)
    a_ab = each(lambda x: (x[0:CHUNK, 0:GROUP] * c_strict).astype(BF16), scores)
    a_ak = each(lambda x: (x[0:CHUNK, GROUP:] * c_strict).astype(BF16), scores)
    a_rb = each(lambda x: (x[CHUNK:, 0:GROUP] * c_incl).astype(BF16), scores)
    a_rk = each(lambda x: (x[CHUNK:, GROUP:] * c_incl).astype(BF16), scores)

    n1 = each(lambda x: x * c_b8, a_ab)
    n2 = each(lambda x: _bdot(x, bd(x)).astype(BF16), n1)
    sq = each(lambda x, y: _bdot(jnp.concatenate([y, c_eye + x], axis=0), bd(y)), n1, n2)
    inv = each(lambda x, s2: c_eye + x + s2[CHUNK:], n1, sq)
    inv = each(lambda x, s2: x + _bdot(x, bd(s2[0:CHUNK])), inv, sq)
    for c_off in c_levels:
        tmp = each(lambda x, a: _bdot(x, bd(a * c_off)).astype(BF16), inv, a_ab)
        inv = each(lambda x, t: x + _bdot(t, bd(x)), inv, tmp)

    v_bd = each(bd, v_g)
    av = each(lambda ak, rk, vb: _bdot(jnp.concatenate([ak, rk], axis=0), vb),
              a_ak, a_rk, v_bd)
    wu = each(lambda t, ab, x: _bdot(t, jnp.concatenate([bd(ab), bd(x[0:CHUNK])], axis=1)),
              inv, a_bar, av)
    ry = each(lambda a, x: _bdot(a, jnp.concatenate([bd(x[:, 0:GROUP]), bd(x[:, GROUP:])], axis=1)),
              a_rb, wu)
    r_hat = each(lambda x, y: x + y[:, 0:GROUP], r_bar, ry)
    y_hat = each(lambda y, x: y[:, GROUP:] + x[CHUNK:], ry, av)

    state = [st_ref[bb, gi] for bb, gi in chains]
    y = each(lambda rh, st, yh: lax.dot_general(
        rh.astype(BF16), bd(st), (((1,), (1,)), ((), ())),
        preferred_element_type=F32) + yh, r_hat, state, y_hat)

    def trans_fn(bh, kh, x, vg):
        vb = vg.astype(BF16)
        lhs_t = jnp.concatenate(
            [x.astype(BF16), jnp.concatenate([jnp.zeros_like(vb), vb], axis=1)], axis=0)
        rhs_t = jnp.concatenate([bh, kh], axis=0)
        return lax.dot_general(lhs_t, rhs_t.astype(BF16),
                               (((0,), (0,)), ((), ())), preferred_element_type=F32)

    mc = each(trans_fn, b_hat, k_hat, wu, v_g)
    for (bb, gi), (_, sl), mci, st in zip(chains, rsl, mc, state):
        st_ref[bb, gi] = (st * jnp.exp(cum_last[bb][:, sl])
                          + _bdot(st, mci[0:GROUP, :] * m_bd) + fold4(mci[GROUP:, :] * m_bd))

    inv_n = 1.0 / RWKV_HEAD_DIM
    mean = each(lambda x: head_sum2(x) * inv_n, y)
    dlt = each(lambda x, m: x - m, y, mean)
    var = each(lambda x: head_sum(x * x) * inv_n, dlt)
    bonus = each(lambda s2, vg: s2[1] * vg, sums, v_g)
    for (bb, gi), (rs, sl), d, vr, bo in zip(chains, rsl, dlt, var, bonus):
        yn = d * lax.rsqrt(vr + GN_EPS) * ln_g[:, sl] + ln_b[:, sl]
        o_ref[bb, :, sl] = ((yn + bo) * gate[rs, sl]).astype(o_ref.dtype)


def _rwkv_call(p3, vecs, wd, wa, wg, nb):
    b, s, _ = p3.shape
    w = RWKV_WIDTH
    mk, mc, tri = _rwkv_masks(nb)
    mk = jnp.asarray(mk)
    mc = jnp.asarray(mc)
    tri = jnp.asarray(tri, dtype=BF16)
    ones_bd = mk[0].astype(BF16)
    const2 = lambda bi, ci: (0, 0)
    const3 = lambda bi, ci: (0, 0, 0)
    return pl.pallas_call(
        functools.partial(_rwkv_kernel, nb=nb),
        grid=(b // nb, s // CHUNK),
        in_specs=[pl.BlockSpec((nb, CHUNK, w), lambda bi, ci: (bi, ci, 0)),
                  pl.BlockSpec((nb, CHUNK, w), lambda bi, ci: (bi, ci, 1)),
                  pl.BlockSpec((nb, CHUNK, w), lambda bi, ci: (bi, ci, 2)),
                  pl.BlockSpec((nb, CHUNK, 4 * LANES), lambda bi, ci: (bi, ci, 3 * w // (4 * LANES))),
                  pl.BlockSpec(vecs.shape, const2),
                  pl.BlockSpec(wd.shape, const2),
                  pl.BlockSpec(wa.shape, const2),
                  pl.BlockSpec(wg.shape, const2),
                  pl.BlockSpec(mk.shape, const3),
                  pl.BlockSpec(mc.shape, const3),
                  pl.BlockSpec(tri.shape, const2),
                  pl.BlockSpec(ones_bd.shape, const2)],
        out_specs=pl.BlockSpec((nb, CHUNK, w), lambda bi, ci: (bi, ci, 0)),
        out_shape=jax.ShapeDtypeStruct((b, s, w), BF16),
        scratch_shapes=[pltpu.VMEM((nb, N_GROUPS, CHUNK, GROUP), F32)],
        compiler_params=pltpu.CompilerParams(dimension_semantics=("arbitrary", "arbitrary"),
                                             vmem_limit_bytes=VMEM_LIMIT),
        name="rwkv",
    )(p3, p3, p3, p3, vecs, wd, wa, wg, mk, mc, tri, ones_bd)


def _mlaproj_kernel(pq_ref, pkv_ref, pos_ref, invf_ref, gq_ref, gkv_ref, wq_ref, wkv_ref,
                    q_ref, k_ref, v_ref):
    q = _bdot(_rms(pq_ref[...], gq_ref[...]), wq_ref[...])
    pkv = pkv_ref[...]
    kv = _bdot(_rms(pkv[:, 0:KV_LORA_RANK], gkv_ref[...]), wkv_ref[...])
    ang = pos_ref[...].astype(F32) * invf_ref[...]
    lane = lax.broadcasted_iota(jnp.int32, ang.shape, 1)
    half = QK_ROPE_DIM // 2
    cos_f = jnp.where(lane < QK_ROPE_DIM, jnp.cos(ang), 0.0)
    sin = jnp.sin(ang)
    sin_f = jnp.where(lane < half, -sin, jnp.where(lane < QK_ROPE_DIM, sin, 0.0))
    k_rot = pkv[:, KV_LORA_RANK:KV_LORA_RANK + LANES] * cos_f + pkv[:, KV_LORA_RANK + LANES:] * sin_f
    scale = (QK_NOPE_DIM + QK_ROPE_DIM) ** -0.5 * 1.4426950408889634
    swap0 = MLA_HEADS * QK_PAD
    ones_col = (lane == 0).astype(F32)
    for h in range(MLA_HEADS):
        q_nope = q[:, h * QK_PAD:h * QK_PAD + LANES]
        q_rot = (q[:, h * QK_PAD + LANES:(h + 1) * QK_PAD] * cos_f
                 + q[:, swap0 + h * LANES:swap0 + (h + 1) * LANES] * sin_f)
        q_ref[0, h] = (jnp.concatenate([q_nope, q_rot], axis=1) * scale).astype(q_ref.dtype)
        k_ref[0, h] = jnp.concatenate([kv[:, h * LANES:(h + 1) * LANES], k_rot], axis=1).astype(k_ref.dtype)
        v_ref[0, h] = jnp.concatenate(
            [kv[:, (MLA_HEADS + h) * LANES:(MLA_HEADS + h + 1) * LANES], ones_col],
            axis=1).astype(v_ref.dtype)


def _mlaproj_call(p2, pos2, invf, gq, gkv, wq, wkv, batch, seq, tm):
    t = p2.shape[0]
    tps = seq // tm
    blk = 4 * LANES
    q_blk = (IN_COLS_PAD - 2 * blk) // blk
    hd = MLA_HEADS
    out_map = lambda i: (i // tps, 0, i % tps, 0)
    const = lambda i: (0, 0)
    return pl.pallas_call(
        _mlaproj_kernel,
        grid=(t // tm,),
        in_specs=[pl.BlockSpec((tm, blk), lambda i: (i, q_blk)),
                  pl.BlockSpec((tm, blk), lambda i: (i, q_blk + 1)),
                  pl.BlockSpec((tm, 1), lambda i: (i, 0)),
                  pl.BlockSpec(invf.shape, const),
                  pl.BlockSpec(gq.shape, const),
                  pl.BlockSpec(gkv.shape, const),
                  pl.BlockSpec(wq.shape, const),
                  pl.BlockSpec(wkv.shape, const)],
        out_specs=[pl.BlockSpec((1, hd, tm, QK_PAD), out_map),
                   pl.BlockSpec((1, hd, tm, QK_PAD), out_map),
                   pl.BlockSpec((1, hd, tm, V_PAD), out_map)],
        out_shape=[jax.ShapeDtypeStruct((batch, hd, seq, QK_PAD), BF16),
                   jax.ShapeDtypeStruct((batch, hd, seq, QK_PAD), BF16),
                   jax.ShapeDtypeStruct((batch, hd, seq, V_PAD), BF16)],
        compiler_params=pltpu.CompilerParams(dimension_semantics=("arbitrary",),
                                             vmem_limit_bytes=VMEM_LIMIT),
        name="mlaproj",
    )(p2, p2, pos2, invf, gq, gkv, wq, wkv)


ATTN_ROW_PARTS = 4
ATTN_SM_ROWS = 32
ATTN_HEADS_PER_STEP = 2


def _attn_kernel(q_ref, k_ref, v_ref, o_ref, s_ref, p_ref, m_ref, a_ref, acc_ref, *, tq):
    i = pl.program_id(2)
    rp = tq // ATTN_ROW_PARTS
    heads = range(ATTN_HEADS_PER_STEP)
    m_ref[...] = jnp.full(m_ref.shape, -jnp.inf, F32)
    acc_ref[...] = jnp.zeros_like(acc_ref)

    def scores(hh, start, part, nk):
        rows = slice(part * rp, (part + 1) * rp)
        s_ref[hh, rows, 0:nk] = lax.dot_general(
            q_ref[0, hh, rows, :], k_ref[0, hh, pl.ds(start, nk), :],
            (((1,), (1,)), ((), ())), preferred_element_type=F32)

    def block(start, next_start, on_diagonal):
        def keys(part):
            return (part + 1) * rp if on_diagonal else tq

        def softmax(hh, part):
            nk = keys(part)
            for c in range(rp // ATTN_SM_ROWS):
                r0 = part * rp + c * ATTN_SM_ROWS
                rows = slice(r0, r0 + ATTN_SM_ROWS)
                sc = s_ref[hh, rows, 0:nk]
                if on_diagonal:
                    row = (r0 + lax.broadcasted_iota(jnp.int32, sc.shape, 0)) // CHUNK
                    col = lax.broadcasted_iota(jnp.int32, sc.shape, 1) // CHUNK
                    sc = jnp.where(row >= col, sc, -jnp.inf)
                m_old = m_ref[hh, rows, :]
                m_new = jnp.maximum(m_old, jnp.max(sc, axis=-1, keepdims=True))
                m_ref[hh, rows, :] = m_new
                a_ref[hh, rows, :] = jnp.exp2(m_old - m_new)
                p_ref[hh, rows, 0:nk] = jnp.exp2(sc - m_new).astype(BF16)

        def weighted_values(hh, part):
            rows = slice(part * rp, (part + 1) * rp)
            nk = keys(part)
            acc_ref[hh, rows, :] = a_ref[hh, rows, :] * acc_ref[hh, rows, :] + jnp.dot(
                p_ref[hh, rows, 0:nk], v_ref[0, hh, pl.ds(start, nk), :], preferred_element_type=F32)

        for part in range(ATTN_ROW_PARTS):
            for hh in heads:
                if part + 1 < ATTN_ROW_PARTS:
                    scores(hh, start, part + 1, keys(part + 1))
                elif not on_diagonal:
                    scores(hh, next_start, 0, tq)
            for hh in heads:
                softmax(hh, part)
                weighted_values(hh, part)

    def body(j, carry):
        block(pl.multiple_of(j * tq, tq), pl.multiple_of((j + 1) * tq, tq), False)
        return carry

    for hh in heads:
        scores(hh, 0, 0, tq)
    lax.fori_loop(0, i, body, 0)
    block(pl.multiple_of(i * tq, tq), None, True)
    for hh in heads:
        acc = acc_ref[hh]
        o_ref[0, :, hh * V_HEAD_DIM:(hh + 1) * V_HEAD_DIM] = (
            acc[:, 0:V_HEAD_DIM] / acc[:, V_HEAD_DIM:V_HEAD_DIM + 1]).astype(o_ref.dtype)


def _attn_call(q4, k4, v4, tq):
    b, h, s, _ = q4.shape
    hp = ATTN_HEADS_PER_STEP
    return pl.pallas_call(
        functools.partial(_attn_kernel, tq=tq),
        grid=(b, h // hp, s // tq),
        in_specs=[pl.BlockSpec((1, hp, tq, QK_PAD), lambda bi, hi, i: (bi, hi, i, 0)),
                  pl.BlockSpec((1, hp, s, QK_PAD), lambda bi, hi, i: (bi, hi, 0, 0)),
                  pl.BlockSpec((1, hp, s, V_PAD), lambda bi, hi, i: (bi, hi, 0, 0))],
        out_specs=pl.BlockSpec((1, tq, hp * V_HEAD_DIM), lambda bi, hi, i: (bi, i, hi)),
        out_shape=jax.ShapeDtypeStruct((b, s, h * V_HEAD_DIM), BF16),
        scratch_shapes=[pltpu.VMEM((hp, tq, tq), F32), pltpu.VMEM((hp, tq, tq), BF16),
                        pltpu.VMEM((hp, tq, 1), F32), pltpu.VMEM((hp, tq, 1), F32),
                        pltpu.VMEM((hp, tq, V_PAD), F32)],
        compiler_params=pltpu.CompilerParams(
            dimension_semantics=("arbitrary", "arbitrary", "arbitrary"),
            vmem_limit_bytes=VMEM_LIMIT),
        name="attn",
    )(q4, k4, v4)


NORM_ROWS = 32


def _outproj_kernel(yr_ref, ym_ref, x_ref, mod_ref, g_ref, gffn_ref, w_ref, o_ref, h_ref):
    half = RWKV_WIDTH
    o = (jnp.dot(yr_ref[...], w_ref[0:half, :], preferred_element_type=F32)
         + jnp.dot(ym_ref[...], w_ref[half:, :], preferred_element_type=F32))
    x_mid = x_ref[...] + mod_ref[0, 2:3, :] * _rms(o, g_ref[...])
    o_ref[...] = x_mid
    h = _rms(x_mid, gffn_ref[...]) * (1.0 + mod_ref[0, 4:5, :]) + mod_ref[0, 3:4, :]
    h_ref[...] = h.astype(h_ref.dtype)


def _outproj_call(yr, ym, x2, mod3, g_post, g_pre_ffn, w_out, seq, tm):
    t, d = x2.shape
    tps = seq // tm
    return pl.pallas_call(
        _outproj_kernel,
        grid=(t // tm,),
        in_specs=[pl.BlockSpec((tm, RWKV_WIDTH), lambda i: (i, 0)),
                  pl.BlockSpec((tm, MLA_WIDTH), lambda i: (i, 0)),
                  pl.BlockSpec((tm, d), lambda i: (i, 0)),
                  pl.BlockSpec((1, N_MOD, d), lambda i: (i // tps, 0, 0)),
                  pl.BlockSpec((1, d), lambda i: (0, 0)),
                  pl.BlockSpec((1, d), lambda i: (0, 0)),
                  pl.BlockSpec(w_out.shape, lambda i: (0, 0))],
        out_specs=[pl.BlockSpec((tm, d), lambda i: (i, 0)),
                   pl.BlockSpec((tm, d), lambda i: (i, 0))],
        out_shape=[jax.ShapeDtypeStruct((t, d), F32), jax.ShapeDtypeStruct((t, d), BF16)],
        compiler_params=pltpu.CompilerParams(dimension_semantics=("arbitrary",),
                                             vmem_limit_bytes=VMEM_LIMIT),
        name="outproj",
    )(yr, ym, x2, mod3, g_post, g_pre_ffn, w_out)


FFN_ROWS = 128


def _gelu_tanh(x):
    c = 0.7978845608028654
    return 0.5 * x * (1.0 + jnp.tanh(c * (x + 0.044715 * (x * x * x))))


def _ffn_kernel(h_ref, hh_ref, x_ref, mod_ref, wg_ref, wv_ref, cwg_ref, cwv_ref,
                cbg_ref, cbv_ref, wd_ref, gpost_ref, o_ref, hcat_ref, acc_ref, ug_ref, uv_ref, act_ref,
                *, tiles_per_seq):
    i = pl.program_id(0)
    j = pl.program_id(1)
    tm = act_ref.shape[0]

    @pl.when(j == 0)
    def _prologue():
        keep = (i % tiles_per_seq) != 0
        hcat_ref[0:HALO, :] = jnp.where(keep, hh_ref[...], jnp.zeros_like(hh_ref))
        hcat_ref[HALO:, :] = h_ref[...]
        acc_ref[...] = jnp.zeros_like(acc_ref)

    hb = hcat_ref[...]
    ug_ref[...] = jnp.dot(hb, wg_ref[...], preferred_element_type=F32)
    uv_ref[...] = jnp.dot(hb, wv_ref[...], preferred_element_type=F32)

    def conv(u_ref, r0, cw_ref, cb_ref):
        return (cb_ref[...] + cw_ref[2:3, :] * u_ref[pl.ds(r0, FFN_ROWS), :]
                + cw_ref[1:2, :] * u_ref[pl.ds(r0 - 1, FFN_ROWS), :]
                + cw_ref[0:1, :] * u_ref[pl.ds(r0 - 2, FFN_ROWS), :])

    for rc in range(tm // FFN_ROWS):
        r0 = HALO + rc * FFN_ROWS
        act = _gelu_tanh(conv(ug_ref, r0, cwg_ref, cbg_ref)) * conv(uv_ref, r0, cwv_ref, cbv_ref)
        act_ref[rc * FFN_ROWS:(rc + 1) * FFN_ROWS, :] = act.astype(BF16)
    acc_ref[...] += jnp.dot(act_ref[...], wd_ref[...], preferred_element_type=F32)

    @pl.when(j == pl.num_programs(1) - 1)
    def _epilogue():
        for c in range(tm // NORM_ROWS):
            rows = slice(c * NORM_ROWS, (c + 1) * NORM_ROWS)
            o_ref[rows, :] = x_ref[rows, :] + mod_ref[0, 5:6, :] * _rms(acc_ref[rows, :], gpost_ref[...])


def _ffn_call(h2, x2, mod3, w_up, conv_w, conv_b, w_down, g_post, seq, tm, tf):
    t, d = x2.shape
    tps = seq // tm
    hb = tm // HALO
    nf = D_FF // tf
    return pl.pallas_call(
        functools.partial(_ffn_kernel, tiles_per_seq=tps),
        grid=(t // tm, nf),
        in_specs=[pl.BlockSpec((tm, d), lambda i, j: (i, 0)),
                  pl.BlockSpec((HALO, d), lambda i, j: (jnp.maximum(i * hb - 1, 0), 0)),
                  pl.BlockSpec((tm, d), lambda i, j: (i, 0)),
                  pl.BlockSpec((1, N_MOD, d), lambda i, j: (i // tps, 0, 0)),
                  pl.BlockSpec((d, tf), lambda i, j: (0, j)),
                  pl.BlockSpec((d, tf), lambda i, j: (0, nf + j)),
                  pl.BlockSpec((3, tf), lambda i, j: (0, j)),
                  pl.BlockSpec((3, tf), lambda i, j: (0, nf + j)),
                  pl.BlockSpec((1, tf), lambda i, j: (0, j)),
                  pl.BlockSpec((1, tf), lambda i, j: (0, nf + j)),
                  pl.BlockSpec((tf, d), lambda i, j: (j, 0)),
                  pl.BlockSpec((1, d), lambda i, j: (0, 0))],
        out_specs=pl.BlockSpec((tm, d), lambda i, j: (i, 0)),
        out_shape=jax.ShapeDtypeStruct((t, d), F32),
        scratch_shapes=[pltpu.VMEM((tm + HALO, d), BF16), pltpu.VMEM((tm, d), F32),
                        pltpu.VMEM((tm + HALO, tf), F32), pltpu.VMEM((tm + HALO, tf), F32),
                        pltpu.VMEM((tm, tf), BF16)],
        compiler_params=pltpu.CompilerParams(dimension_semantics=("arbitrary", "arbitrary"),
                                             vmem_limit_bytes=VMEM_LIMIT),
        name="ffn",
    )(h2, h2, x2, mod3, w_up, w_up, conv_w, conv_w, conv_b, conv_b, w_down, g_post)


def _pad_cols(w, n):
    return jnp.pad(w, ((0, 0), (0, n - w.shape[1])))


def _layout_w_in(w_in, mu_shift):
    w3 = 3 * RWKV_WIDTH
    o_wd = w3
    o_ad = o_wd + DECAY_LORA
    o_gd = o_ad + ICLR_LORA
    o_q = o_gd + GATE_LORA
    o_kv = o_q + Q_LORA_RANK
    o_kr = o_kv + KV_LORA_RANK
    half = QK_ROPE_DIM // 2

    def lay(m):
        kr = m[:, o_kr:o_kr + QK_ROPE_DIM]
        return jnp.concatenate([
            m[:, 0:w3],
            _pad_cols(m[:, o_wd:o_ad], LANES),
            _pad_cols(m[:, o_ad:o_gd], LANES),
            m[:, o_gd:o_q],
            m[:, o_q:o_kv],
            m[:, o_kv:o_kr],
            _pad_cols(kr, LANES),
            _pad_cols(jnp.concatenate([kr[:, half:], kr[:, :half]], axis=1), LANES),
        ], axis=1)

    mu_full = jnp.concatenate([mu_shift, jnp.zeros((w_in.shape[1] - mu_shift.shape[0],), F32)])
    w_p = lay(w_in).astype(BF16)
    mu_p = lay(mu_full[None, :])
    return w_p, mu_p


def _layout_w_q(w_q_up):
    dn, dr = QK_NOPE_DIM, QK_ROPE_DIM
    half = dr // 2
    w = w_q_up.reshape(Q_LORA_RANK, MLA_HEADS, dn + dr)
    nope, u1, u2 = w[..., :dn], w[..., dn:dn + half], w[..., dn + half:]
    z = jnp.zeros(u1.shape[:-1] + (LANES - dr,), w.dtype)
    main = jnp.concatenate([nope, u1, u2, z], axis=-1).reshape(Q_LORA_RANK, MLA_HEADS * QK_PAD)
    swap = jnp.concatenate([u2, u1, z], axis=-1).reshape(Q_LORA_RANK, MLA_HEADS * LANES)
    return jnp.concatenate([main, swap], axis=1).astype(BF16)


def _layout_w_kv(w_kv_up):
    w = w_kv_up.reshape(KV_LORA_RANK, MLA_HEADS, QK_NOPE_DIM + V_HEAD_DIM)
    kn = w[..., :QK_NOPE_DIM].reshape(KV_LORA_RANK, MLA_HEADS * QK_NOPE_DIM)
    vv = w[..., QK_NOPE_DIM:].reshape(KV_LORA_RANK, MLA_HEADS * V_HEAD_DIM)
    return jnp.concatenate([kn, vv], axis=1).astype(BF16)


def _pad_rows(w, n):
    return jnp.pad(w, ((0, n - w.shape[0]), (0, 0)))


def _block(x, c, positions, w_mod, b_mod, g_pre_mix, w_in, mu_shift, w0, w_decay_up, a0,
           w_iclr_up, w_gate_up, k_k, k_a, r_k, ln_x_g, ln_x_b, q_norm_g, w_q_up, kv_norm_g,
           w_kv_up, w_out, g_post_mix, g_pre_ffn, w_ffn_up, conv_w, conv_b, w_ffn_down,
           g_post_ffn, *, tm_in, tn_in, nb_rwkv, tm_mla, tq, tm_out, tm_ffn, tf):
    b, s, d = x.shape
    t = b * s
    x2 = x.reshape(t, d)

    c_pad = jnp.pad(c, ((0, 8 - b % 8 if b % 8 else 0), (0, 0)))
    mod = _mod_call(c_pad, w_mod, b_mod[None, :])[:b]
    mod3 = mod.reshape(b, N_MOD, d)

    w_in_p, mu_p = _layout_w_in(w_in, mu_shift)
    p2 = _inproj_call(x2, mod3, g_pre_mix[None, :], w_in_p, mu_p, s, tm_in, tn_in)

    vecs = jnp.stack([w0, a0, k_k, k_a, r_k.reshape(-1), ln_x_g, ln_x_b, jnp.zeros_like(w0)])
    y_rwkv = _rwkv_call(p2.reshape(b, s, IN_COLS_PAD), vecs,
                        _pad_rows(w_decay_up, LANES).astype(BF16),
                        _pad_rows(w_iclr_up, LANES).astype(BF16),
                        w_gate_up.astype(BF16), nb_rwkv)

    half = QK_ROPE_DIM // 2
    inv_freq = ROPE_THETA ** (-jnp.arange(0, QK_ROPE_DIM, 2, dtype=F32) / QK_ROPE_DIM)
    invf = jnp.concatenate([inv_freq, inv_freq, jnp.zeros((LANES - 2 * half,), F32)])[None, :]
    q4, k4, v4 = _mlaproj_call(p2, positions.reshape(t, 1), invf, q_norm_g[None, :],
                               kv_norm_g[None, :], _layout_w_q(w_q_up), _layout_w_kv(w_kv_up),
                               b, s, tm_mla)
    y_mla = _attn_call(q4, k4, v4, tq)

    x_mid, h_ffn = _outproj_call(y_rwkv.reshape(t, RWKV_WIDTH), y_mla.reshape(t, MLA_WIDTH), x2,
                                 mod3, g_post_mix[None, :], g_pre_ffn[None, :],
                                 w_out.astype(BF16), s, tm_out)
    out = _ffn_call(h_ffn, x_mid, mod3, w_ffn_up.astype(BF16), conv_w, conv_b[None, :],
                    w_ffn_down.astype(BF16), g_post_ffn[None, :], s, tm_ffn, tf)
    return out.reshape(b, s, d)


def kernel(x, c, positions, w_mod, b_mod, g_pre_mix, w_in, mu_shift, w0, w_decay_up, a0, w_iclr_up, w_gate_up, k_k, k_a, r_k, ln_x_g, ln_x_b, q_norm_g, w_q_up, kv_norm_g, w_kv_up, w_out, g_post_mix, g_pre_ffn, w_ffn_up, conv_w, conv_b, w_ffn_down, g_post_ffn):
    args = (x, c, positions, w_mod, b_mod, g_pre_mix, w_in, mu_shift, w0, w_decay_up, a0,
            w_iclr_up, w_gate_up, k_k, k_a, r_k, ln_x_g, ln_x_b, q_norm_g, w_q_up, kv_norm_g,
            w_kv_up, w_out, g_post_mix, g_pre_ffn, w_ffn_up, conv_w, conv_b, w_ffn_down,
            g_post_ffn)
    for l in range(w_mod.shape[0]):
        layer = [a[l] for a in args[3:]]
        x = _block(x, c, positions, *layer, tm_in=512, tn_in=1536, nb_rwkv=2, tm_mla=512, tq=1024,
                   tm_out=512, tm_ffn=512, tf=512)
    return x
```

```python
import functools

import numpy as np
import jax
import jax.numpy as jnp
from jax import lax
from jax.experimental import pallas as pl
from jax.experimental.pallas import tpu as pltpu

F32 = jnp.float32
BF16 = jnp.bfloat16

D_MODEL = 2048
CHUNK = 64
RWKV_HEAD_DIM = 64
RWKV_WIDTH = 1024
RWKV_HEADS = 16
DECAY_LORA = 96
ICLR_LORA = 96
GATE_LORA = 256
GN_EPS = 64e-5
QK_NOPE_DIM = 128
QK_ROPE_DIM = 64
V_HEAD_DIM = 128
MLA_WIDTH = 1024
MLA_HEADS = 8
Q_LORA_RANK = 512
KV_LORA_RANK = 256
ROPE_THETA = 10000.0
D_FF = 5632
NORM_EPS = 1e-6
N_MOD = 6

LANES = 128
HALO = 16
GROUP = 256
HEADS_PER_GROUP = GROUP // RWKV_HEAD_DIM
N_GROUPS = RWKV_WIDTH // GROUP
QK_PAD = 256
V_PAD = 256
IN_COLS_PAD = 4608
VMEM_LIMIT = 56 * 1024 * 1024


def _bdot(a, b):
    return jnp.dot(a.astype(BF16), b.astype(BF16), preferred_element_type=F32)


def _split2(x):
    hi = x.astype(BF16)
    lo = (x - hi.astype(F32)).astype(BF16)
    return hi, lo


def _sigmoid(x):
    return 1.0 / (1.0 + jnp.exp(-x))


def _rms(xv, g):
    ms = jnp.mean(xv * xv, axis=-1, keepdims=True)
    return xv * lax.rsqrt(ms + NORM_EPS) * g


def _mod_kernel(c_ref, w_ref, b_ref, o_ref):
    c = c_ref[...]
    s = c * _sigmoid(c)
    o_ref[...] = _bdot(s, w_ref[...]) + b_ref[...]


def _mod_call(c_pad, w_mod, b_mod, tn=1024):
    rows, d = c_pad.shape
    n = w_mod.shape[1]
    return pl.pallas_call(
        _mod_kernel,
        grid=(n // tn,),
        in_specs=[pl.BlockSpec((rows, d), lambda j: (0, 0)),
                  pl.BlockSpec((d, tn), lambda j: (0, j)),
                  pl.BlockSpec((1, tn), lambda j: (0, j))],
        out_specs=pl.BlockSpec((rows, tn), lambda j: (0, j)),
        out_shape=jax.ShapeDtypeStruct((rows, n), F32),
        compiler_params=pltpu.CompilerParams(dimension_semantics=("arbitrary",),
                                             vmem_limit_bytes=VMEM_LIMIT),
        name="mod",
    )(c_pad, w_mod, b_mod)


def _inproj_kernel(x_ref, xh_ref, mod_ref, g_ref, w_ref, mu_ref, o_ref, *, tiles_per_seq):
    i = pl.program_id(1)
    first = (i % tiles_per_seq) == 0
    shift = mod_ref[0, 0:1, :]
    scale = mod_ref[0, 1:2, :]
    g = g_ref[...]
    h = _rms(x_ref[...], g) * (1.0 + scale) + shift
    hh = _rms(xh_ref[...], g) * (1.0 + scale) + shift
    hh = jnp.where(first, 0.0, hh)
    hcat = jnp.concatenate([hh, h], axis=0).astype(BF16)
    p = jnp.dot(hcat, w_ref[...], preferred_element_type=F32)
    prev = pltpu.roll(p, 1, axis=0)
    out = p + (prev - p) * mu_ref[...]
    o_ref[...] = out[HALO:, :]


def _inproj_call(x2, mod3, g_pre, w_in_p, mu_p, seq, tm, tn):
    t, d = x2.shape
    n = w_in_p.shape[1]
    tps = seq // tm
    hb = tm // HALO
    return pl.pallas_call(
        functools.partial(_inproj_kernel, tiles_per_seq=tps),
        grid=(n // tn, t // tm),
        in_specs=[pl.BlockSpec((tm, d), lambda j, i: (i, 0)),
                  pl.BlockSpec((HALO, d), lambda j, i: (jnp.maximum(i * hb - 1, 0), 0)),
                  pl.BlockSpec((1, N_MOD, d), lambda j, i: (i // tps, 0, 0)),
                  pl.BlockSpec((1, d), lambda j, i: (0, 0)),
                  pl.BlockSpec((d, tn), lambda j, i: (0, j)),
                  pl.BlockSpec((1, tn), lambda j, i: (0, j))],
        out_specs=pl.BlockSpec((tm, tn), lambda j, i: (i, j)),
        out_shape=jax.ShapeDtypeStruct((t, n), F32),
        compiler_params=pltpu.CompilerParams(dimension_semantics=("arbitrary", "arbitrary"),
                                             vmem_limit_bytes=VMEM_LIMIT),
        name="inproj",
    )(x2, x2, mod3, g_pre, w_in_p, mu_p)


def _rwkv_masks(nb):
    i = np.arange(GROUP)[:, None]
    j = np.arange(GROUP)[None, :]
    n = RWKV_HEAD_DIM
    bd = (i // n) == (j // n)
    mk = bd.astype(np.float32)[None]
    t = np.arange(CHUNK)[:, None]
    sj = j % n
    mc = np.stack([
        sj < t,
        sj <= t,
        sj == t,
        (sj < t) & ((t // 8) == (sj // 8)),
        ((t // 16) == (sj // 16)) & ((t // 8) > (sj // 8)),
        ((t // 32) == (sj // 32)) & ((t // 16) > (sj // 16)),
        (t // 32) > (sj // 32),
    ]).astype(np.float32)
    ti = np.arange(nb * CHUNK)[:, None]
    tj = np.arange(nb * CHUNK)[None, :]
    tri = ((ti // CHUNK == tj // CHUNK) & (tj <= ti)).astype(np.float32)
    return mk, mc, tri


def _rwkv_kernel(r_ref, k_ref, v_ref, lo_ref, vec_ref, wd_ref, wa_ref, wg_ref,
                 mk_ref, mc_ref, tri_ref, e_ref, o_ref, st_ref, *, nb):
    @pl.when(pl.program_id(1) == 0)
    def _reset_state():
        st_ref[...] = jnp.zeros_like(st_ref)

    def rows(ref):
        return jnp.concatenate([ref[bb] for bb in range(nb)], axis=0)

    r = rows(r_ref)
    k = rows(k_ref)
    v = rows(v_ref)
    lo = rows(lo_ref)
    w0 = vec_ref[0:1, :]
    a0 = vec_ref[1:2, :]
    k_k = vec_ref[2:3, :]
    k_a = vec_ref[3:4, :]
    r_k = vec_ref[4:5, :]
    ln_g = vec_ref[5:6, :]
    ln_b = vec_ref[6:7, :]

    z = -(w0 + _bdot(jnp.tanh(lo[:, 0:LANES]), wd_ref[...]))
    softplus = jnp.maximum(z, 0.0) + jnp.log(1.0 + jnp.exp(-jnp.abs(z)))
    lw = -jnp.exp(-softplus - 0.5)
    a = _sigmoid(a0 + _bdot(lo[:, LANES:2 * LANES], wa_ref[...]))
    gate = _bdot(_sigmoid(lo[:, 2 * LANES:4 * LANES]), wg_ref[...])
    kk = k * k_k
    kp = k * (1.0 + (a - 1.0) * k_a)

    tri = tri_ref[...]
    l1 = lw.astype(BF16)
    rem = lw - l1.astype(F32)
    l2 = rem.astype(BF16)
    l3 = (rem - l2.astype(F32)).astype(BF16)
    cum = (jnp.dot(tri, l1, preferred_element_type=F32)
           + jnp.dot(tri, l2, preferred_element_type=F32)
           + jnp.dot(tri, l3, preferred_element_type=F32))
    cum_last = [cum[(bb + 1) * CHUNK - 1:(bb + 1) * CHUNK, :] for bb in range(nb)]
    cum_last_rows = jnp.concatenate(
        [jnp.broadcast_to(cl, (CHUNK, cl.shape[1])) for cl in cum_last], axis=0)
    e_pos = jnp.exp(cum)
    e_pos_x = jnp.exp(cum - lw)
    e_neg = jnp.exp(-cum)
    e_end = jnp.exp(cum_last_rows - cum)

    m_bd = mk_ref[0]
    c_strict = mc_ref[0]
    c_incl = mc_ref[1]
    c_eye = mc_ref[2]
    c_b8 = mc_ref[3].astype(BF16)
    c_levels = [mc_ref[lv].astype(BF16) for lv in (4, 5, 6)]
    ones_bd = e_ref[...]

    def head_sum(x):
        return jnp.dot(x.astype(BF16), ones_bd, preferred_element_type=F32)

    def head_sum_pair(x, y):
        s2 = head_sum(jnp.concatenate([x, y], axis=0))
        return s2[0:CHUNK], s2[CHUNK:]

    def head_sum2(x):
        hi, lo_ = _split2(x)
        s2 = jnp.dot(jnp.concatenate([hi, lo_], axis=0), ones_bd, preferred_element_type=F32)
        return s2[0:CHUNK] + s2[CHUNK:]

    def bd(x):
        return jnp.concatenate([x.astype(BF16)] * HEADS_PER_GROUP, axis=0) * ones_bd

    def fold4(x):
        return (x[0:CHUNK] + x[CHUNK:2 * CHUNK] + x[2 * CHUNK:3 * CHUNK] + x[3 * CHUNK:4 * CHUNK])

    chains = [(bb, gi) for bb in range(nb) for gi in range(N_GROUPS)]
    rsl = [(slice(bb * CHUNK, (bb + 1) * CHUNK), slice(gi * GROUP, (gi + 1) * GROUP))
           for bb, gi in chains]

    def each(fn, *lists):
        return [fn(*args) for args in zip(*lists)]

    def cut(x):
        return [x[rs, sl] for rs, sl in rsl]

    r_g, v_g, kp_g, a_g, kk_g = cut(r), cut(v), cut(kp), cut(a), cut(kk)
    sums = each(lambda x, rg, kg, rsl_: head_sum_pair(x * x, rg * kg * r_k[:, rsl_[1]]),
                kk_g, r_g, kp_g, rsl)
    kkn = each(lambda x, s2: x / jnp.maximum(jnp.sqrt(s2[0]), 1e-12), kk_g, sums)
    ka = each(lambda x, y: x * y, kkn, a_g)
    a_bar = each(lambda x, e: -x * e, kkn, cut(e_pos_x))
    r_bar = each(lambda x, e: x * e, r_g, cut(e_pos))
    b_til = each(lambda x, e: x * e, ka, cut(e_neg))
    k_til = each(lambda x, e: x * e, kp_g, cut(e_neg))
    b_hat = each(lambda x, e: x * e, ka, cut(e_end))
    k_hat = each(lambda x, e: x * e, kp_g, cut(e_end))

    def score_fn(ab, rb, bt, kt):
        lhs = jnp.concatenate([ab, rb], axis=0)
        rhs = jnp.concatenate([bd(bt), bd(kt)], axis=0)
        return lax.dot_general(lhs.astype(BF16), rhs,
                               (((1,), (1,)), ((), ())), preferred_element_type=F32)

    scores = each(score_fn, a_bar, r_bar, b_til, k_til)
    a_ab = each(lambda x: (x[0:CHUNK, 0:GROUP] * c_strict).astype(BF16), scores)
    a_ak = each(lambda x: (x[0:CHUNK, GROUP:] * c_strict).astype(BF16), scores)
    a_rb = each(lambda x: (x[CHUNK:, 0:GROUP] * c_incl).astype(BF16), scores)
    a_rk = each(lambda x: (x[CHUNK:, GROUP:] * c_incl).astype(BF16), scores)

    n1 = each(lambda x: x * c_b8, a_ab)
    n2 = each(lambda x: _bdot(x, bd(x)).astype(BF16), n1)
    sq = each(lambda x, y: _bdot(jnp.concatenate([y, c_eye + x], axis=0), bd(y)), n1, n2)
    inv = each(lambda x, s2: c_eye + x + s2[CHUNK:], n1, sq)
    inv = each(lambda x, s2: x + _bdot(x, bd(s2[0:CHUNK])), inv, sq)
    for c_off in c_levels:
        tmp = each(lambda x, a: _bdot(x, bd(a * c_off)).astype(BF16), inv, a_ab)
        inv = each(lambda x, t: x + _bdot(t, bd(x)), inv, tmp)

    v_bd = each(bd, v_g)
    av = each(lambda ak, rk, vb: _bdot(jnp.concatenate([ak, rk], axis=0), vb),
              a_ak, a_rk, v_bd)
    wu = each(lambda t, ab, x: _bdot(t, jnp.concatenate([bd(ab), bd(x[0:CHUNK])], axis=1)),
              inv, a_bar, av)
    ry = each(lambda a, x: _bdot(a, jnp.concatenate([bd(x[:, 0:GROUP]), bd(x[:, GROUP:])], axis=1)),
              a_rb, wu)
    r_hat = each(lambda x, y: x + y[:, 0:GROUP], r_bar, ry)
    y_hat = each(lambda y, x: y[:, GROUP:] + x[CHUNK:], ry, av)

    state = [st_ref[bb, gi] for bb, gi in chains]
    y = each(lambda rh, st, yh: lax.dot_general(
        rh.astype(BF16), bd(st), (((1,), (1,)), ((), ())),
        preferred_element_type=F32) + yh, r_hat, state, y_hat)

    def trans_fn(bh, kh, x, vg):
        vb = vg.astype(BF16)
        lhs_t = jnp.concatenate(
            [x.astype(BF16), jnp.concatenate([jnp.zeros_like(vb), vb], axis=1)], axis=0)
        rhs_t = jnp.concatenate([bh, kh], axis=0)
        return lax.dot_general(lhs_t, rhs_t.astype(BF16),
                               (((0,), (0,)), ((), ())), preferred_element_type=F32)

    mc = each(trans_fn, b_hat, k_hat, wu, v_g)
    for (bb, gi), (_, sl), mci, st in zip(chains, rsl, mc, state):
        st_ref[bb, gi] = (st * jnp.exp(cum_last[bb][:, sl])
                          + _bdot(st, mci[0:GROUP, :] * m_bd) + fold4(mci[GROUP:, :] * m_bd))

    inv_n = 1.0 / RWKV_HEAD_DIM
    mean = each(lambda x: head_sum2(x) * inv_n, y)
    dlt = each(lambda x, m: x - m, y, mean)
    var = each(lambda x: head_sum(x * x) * inv_n, dlt)
    bonus = each(lambda s2, vg: s2[1] * vg, sums, v_g)
    for (bb, gi), (rs, sl), d, vr, bo in zip(chains, rsl, dlt, var, bonus):
        yn = d * lax.rsqrt(vr + GN_EPS) * ln_g[:, sl] + ln_b[:, sl]
        o_ref[bb, :, sl] = ((yn + bo) * gate[rs, sl]).astype(o_ref.dtype)


def _rwkv_call(p3, vecs, wd, wa, wg, nb):
    b, s, _ = p3.shape
    w = RWKV_WIDTH
    mk, mc, tri = _rwkv_masks(nb)
    mk = jnp.asarray(mk)
    mc = jnp.asarray(mc)
    tri = jnp.asarray(tri, dtype=BF16)
    ones_bd = mk[0].astype(BF16)
    const2 = lambda bi, ci: (0, 0)
    const3 = lambda bi, ci: (0, 0, 0)
    return pl.pallas_call(
        functools.partial(_rwkv_kernel, nb=nb),
        grid=(b // nb, s // CHUNK),
        in_specs=[pl.BlockSpec((nb, CHUNK, w), lambda bi, ci: (bi, ci, 0)),
                  pl.BlockSpec((nb, CHUNK, w), lambda bi, ci: (bi, ci, 1)),
                  pl.BlockSpec((nb, CHUNK, w), lambda bi, ci: (bi, ci, 2)),
                  pl.BlockSpec((nb, CHUNK, 4 * LANES), lambda bi, ci: (bi, ci, 3 * w // (4 * LANES))),
                  pl.BlockSpec(vecs.shape, const2),
                  pl.BlockSpec(wd.shape, const2),
                  pl.BlockSpec(wa.shape, const2),
                  pl.BlockSpec(wg.shape, const2),
                  pl.BlockSpec(mk.shape, const3),
                  pl.BlockSpec(mc.shape, const3),
                  pl.BlockSpec(tri.shape, const2),
                  pl.BlockSpec(ones_bd.shape, const2)],
        out_specs=pl.BlockSpec((nb, CHUNK, w), lambda bi, ci: (bi, ci, 0)),
        out_shape=jax.ShapeDtypeStruct((b, s, w), BF16),
        scratch_shapes=[pltpu.VMEM((nb, N_GROUPS, CHUNK, GROUP), F32)],
        compiler_params=pltpu.CompilerParams(dimension_semantics=("arbitrary", "arbitrary"),
                                             vmem_limit_bytes=VMEM_LIMIT),
        name="rwkv",
    )(p3, p3, p3, p3, vecs, wd, wa, wg, mk, mc, tri, ones_bd)


def _mlaproj_kernel(pq_ref, pkv_ref, pos_ref, invf_ref, gq_ref, gkv_ref, wq_ref, wkv_ref,
                    q_ref, k_ref, v_ref):
    q = _bdot(_rms(pq_ref[...], gq_ref[...]), wq_ref[...])
    pkv = pkv_ref[...]
    kv = _bdot(_rms(pkv[:, 0:KV_LORA_RANK], gkv_ref[...]), wkv_ref[...])
    ang = pos_ref[...].astype(F32) * invf_ref[...]
    lane = lax.broadcasted_iota(jnp.int32, ang.shape, 1)
    half = QK_ROPE_DIM // 2
    cos_f = jnp.where(lane < QK_ROPE_DIM, jnp.cos(ang), 0.0)
    sin = jnp.sin(ang)
    sin_f = jnp.where(lane < half, -sin, jnp.where(lane < QK_ROPE_DIM, sin, 0.0))
    k_rot = pkv[:, KV_LORA_RANK:KV_LORA_RANK + LANES] * cos_f + pkv[:, KV_LORA_RANK + LANES:] * sin_f
    scale = (QK_NOPE_DIM + QK_ROPE_DIM) ** -0.5 * 1.4426950408889634
    swap0 = MLA_HEADS * QK_PAD
    ones_col = (lane == 0).astype(F32)
    for h in range(MLA_HEADS):
        q_nope = q[:, h * QK_PAD:h * QK_PAD + LANES]
        q_rot = (q[:, h * QK_PAD + LANES:(h + 1) * QK_PAD] * cos_f
                 + q[:, swap0 + h * LANES:swap0 + (h + 1) * LANES] * sin_f)
        q_ref[0, h] = (jnp.concatenate([q_nope, q_rot], axis=1) * scale).astype(q_ref.dtype)
        k_ref[0, h] = jnp.concatenate([kv[:, h * LANES:(h + 1) * LANES], k_rot], axis=1).astype(k_ref.dtype)
        v_ref[0, h] = jnp.concatenate(
            [kv[:, (MLA_HEADS + h) * LANES:(MLA_HEADS + h + 1) * LANES], ones_col],
            axis=1).astype(v_ref.dtype)


def _mlaproj_call(p2, pos2, invf, gq, gkv, wq, wkv, batch, seq, tm):
    t = p2.shape[0]
    tps = seq // tm
    blk = 4 * LANES
    q_blk = (IN_COLS_PAD - 2 * blk) // blk
    hd = MLA_HEADS
    out_map = lambda i: (i // tps, 0, i % tps, 0)
    const = lambda i: (0, 0)
    return pl.pallas_call(
        _mlaproj_kernel,
        grid=(t // tm,),
        in_specs=[pl.BlockSpec((tm, blk), lambda i: (i, q_blk)),
                  pl.BlockSpec((tm, blk), lambda i: (i, q_blk + 1)),
                  pl.BlockSpec((tm, 1), lambda i: (i, 0)),
                  pl.BlockSpec(invf.shape, const),
                  pl.BlockSpec(gq.shape, const),
                  pl.BlockSpec(gkv.shape, const),
                  pl.BlockSpec(wq.shape, const),
                  pl.BlockSpec(wkv.shape, const)],
        out_specs=[pl.BlockSpec((1, hd, tm, QK_PAD), out_map),
                   pl.BlockSpec((1, hd, tm, QK_PAD), out_map),
                   pl.BlockSpec((1, hd, tm, V_PAD), out_map)],
        out_shape=[jax.ShapeDtypeStruct((batch, hd, seq, QK_PAD), BF16),
                   jax.ShapeDtypeStruct((batch, hd, seq, QK_PAD), BF16),
                   jax.ShapeDtypeStruct((batch, hd, seq, V_PAD), BF16)],
        compiler_params=pltpu.CompilerParams(dimension_semantics=("arbitrary",),
                                             vmem_limit_bytes=VMEM_LIMIT),
        name="mlaproj",
    )(p2, p2, pos2, invf, gq, gkv, wq, wkv)


ATTN_ROW_PARTS = 4
ATTN_SM_ROWS = 32
ATTN_HEADS_PER_STEP = 2


def _attn_kernel(q_ref, k_ref, v_ref, o_ref, s_ref, p_ref, m_ref, a_ref, acc_ref, *, tq):
    i = pl.program_id(2)
    rp = tq // ATTN_ROW_PARTS
    heads = range(ATTN_HEADS_PER_STEP)
    m_ref[...] = jnp.full(m_ref.shape, -jnp.inf, F32)
    acc_ref[...] = jnp.zeros_like(acc_ref)

    def scores(hh, start, part, nk):
        rows = slice(part * rp, (part + 1) * rp)
        s_ref[hh, rows, 0:nk] = lax.dot_general(
            q_ref[0, hh, rows, :], k_ref[0, hh, pl.ds(start, nk), :],
            (((1,), (1,)), ((), ())), preferred_element_type=F32)

    def block(start, next_start, on_diagonal):
        def keys(part):
            return (part + 1) * rp if on_diagonal else tq

        def softmax(hh, part):
            nk = keys(part)
            for c in range(rp // ATTN_SM_ROWS):
                r0 = part * rp + c * ATTN_SM_ROWS
                rows = slice(r0, r0 + ATTN_SM_ROWS)
                sc = s_ref[hh, rows, 0:nk]
                if on_diagonal:
                    row = (r0 + lax.broadcasted_iota(jnp.int32, sc.shape, 0)) // CHUNK
                    col = lax.broadcasted_iota(jnp.int32, sc.shape, 1) // CHUNK
                    sc = jnp.where(row >= col, sc, -jnp.inf)
                m_old = m_ref[hh, rows, :]
                m_new = jnp.maximum(m_old, jnp.max(sc, axis=-1, keepdims=True))
                m_ref[hh, rows, :] = m_new
                a_ref[hh, rows, :] = jnp.exp2(m_old - m_new)
                p_ref[hh, rows, 0:nk] = jnp.exp2(sc - m_new).astype(BF16)

        def weighted_values(hh, part):
            rows = slice(part * rp, (part + 1) * rp)
            nk = keys(part)
            acc_ref[hh, rows, :] = a_ref[hh, rows, :] * acc_ref[hh, rows, :] + jnp.dot(
                p_ref[hh, rows, 0:nk], v_ref[0, hh, pl.ds(start, nk), :], preferred_element_type=F32)

        for part in range(ATTN_ROW_PARTS):
            for hh in heads:
                if part + 1 < ATTN_ROW_PARTS:
                    scores(hh, start, part + 1, keys(part + 1))
                elif not on_diagonal:
                    scores(hh, next_start, 0, tq)
            for hh in heads:
                softmax(hh, part)
                weighted_values(hh, part)

    def body(j, carry):
        block(pl.multiple_of(j * tq, tq), pl.multiple_of((j + 1) * tq, tq), False)
        return carry

    for hh in heads:
        scores(hh, 0, 0, tq)
    lax.fori_loop(0, i, body, 0)
    block(pl.multiple_of(i * tq, tq), None, True)
    for hh in heads:
        acc = acc_ref[hh]
        o_ref[0, :, hh * V_HEAD_DIM:(hh + 1) * V_HEAD_DIM] = (
            acc[:, 0:V_HEAD_DIM] / acc[:, V_HEAD_DIM:V_HEAD_DIM + 1]).astype(o_ref.dtype)


def _attn_call(q4, k4, v4, tq):
    b, h, s, _ = q4.shape
    hp = ATTN_HEADS_PER_STEP
    return pl.pallas_call(
        functools.partial(_attn_kernel, tq=tq),
        grid=(b, h // hp, s // tq),
        in_specs=[pl.BlockSpec((1, hp, tq, QK_PAD), lambda bi, hi, i: (bi, hi, i, 0)),
                  pl.BlockSpec((1, hp, s, QK_PAD), lambda bi, hi, i: (bi, hi, 0, 0)),
                  pl.BlockSpec((1, hp, s, V_PAD), lambda bi, hi, i: (bi, hi, 0, 0))],
        out_specs=pl.BlockSpec((1, tq, hp * V_HEAD_DIM), lambda bi, hi, i: (bi, i, hi)),
        out_shape=jax.ShapeDtypeStruct((b, s, h * V_HEAD_DIM), BF16),
        scratch_shapes=[pltpu.VMEM((hp, tq, tq), F32), pltpu.VMEM((hp, tq, tq), BF16),
                        pltpu.VMEM((hp, tq, 1), F32), pltpu.VMEM((hp, tq, 1), F32),
                        pltpu.VMEM((hp, tq, V_PAD), F32)],
        compiler_params=pltpu.CompilerParams(
            dimension_semantics=("arbitrary", "arbitrary", "arbitrary"),
            vmem_limit_bytes=VMEM_LIMIT),
        name="attn",
    )(q4, k4, v4)


NORM_ROWS = 32


def _outproj_kernel(yr_ref, ym_ref, x_ref, mod_ref, g_ref, gffn_ref, w_ref, o_ref, h_ref):
    half = RWKV_WIDTH
    o = (jnp.dot(yr_ref[...], w_ref[0:half, :], preferred_element_type=F32)
         + jnp.dot(ym_ref[...], w_ref[half:, :], preferred_element_type=F32))
    x_mid = x_ref[...] + mod_ref[0, 2:3, :] * _rms(o, g_ref[...])
    o_ref[...] = x_mid
    h = _rms(x_mid, gffn_ref[...]) * (1.0 + mod_ref[0, 4:5, :]) + mod_ref[0, 3:4, :]
    h_ref[...] = h.astype(h_ref.dtype)


def _outproj_call(yr, ym, x2, mod3, g_post, g_pre_ffn, w_out, seq, tm):
    t, d = x2.shape
    tps = seq // tm
    return pl.pallas_call(
        _outproj_kernel,
        grid=(t // tm,),
        in_specs=[pl.BlockSpec((tm, RWKV_WIDTH), lambda i: (i, 0)),
                  pl.BlockSpec((tm, MLA_WIDTH), lambda i: (i, 0)),
                  pl.BlockSpec((tm, d), lambda i: (i, 0)),
                  pl.BlockSpec((1, N_MOD, d), lambda i: (i // tps, 0, 0)),
                  pl.BlockSpec((1, d), lambda i: (0, 0)),
                  pl.BlockSpec((1, d), lambda i: (0, 0)),
                  pl.BlockSpec(w_out.shape, lambda i: (0, 0))],
        out_specs=[pl.BlockSpec((tm, d), lambda i: (i, 0)),
                   pl.BlockSpec((tm, d), lambda i: (i, 0))],
        out_shape=[jax.ShapeDtypeStruct((t, d), F32), jax.ShapeDtypeStruct((t, d), BF16)],
        compiler_params=pltpu.CompilerParams(dimension_semantics=("arbitrary",),
                                             vmem_limit_bytes=VMEM_LIMIT),
        name="outproj",
    )(yr, ym, x2, mod3, g_post, g_pre_ffn, w_out)


FFN_ROWS = 128


def _gelu_tanh(x):
    c = 0.7978845608028654
    return 0.5 * x * (1.0 + jnp.tanh(c * (x + 0.044715 * (x * x * x))))


def _ffn_kernel(h_ref, hh_ref, x_ref, mod_ref, wg_ref, wv_ref, cwg_ref, cwv_ref,
                cbg_ref, cbv_ref, wd_ref, gpost_ref, o_ref, hcat_ref, acc_ref, ug_ref, uv_ref, act_ref,
                *, tiles_per_seq):
    i = pl.program_id(0)
    j = pl.program_id(1)
    tm = act_ref.shape[0]

    @pl.when(j == 0)
    def _prologue():
        keep = (i % tiles_per_seq) != 0
        hcat_ref[0:HALO, :] = jnp.where(keep, hh_ref[...], jnp.zeros_like(hh_ref))
        hcat_ref[HALO:, :] = h_ref[...]
        acc_ref[...] = jnp.zeros_like(acc_ref)

    hb = hcat_ref[...]
    ug_ref[...] = jnp.dot(hb, wg_ref[...], preferred_element_type=F32)
    uv_ref[...] = jnp.dot(hb, wv_ref[...], preferred_element_type=F32)

    def conv(u_ref, r0, cw_ref, cb_ref):
        return (cb_ref[...] + cw_ref[2:3, :] * u_ref[pl.ds(r0, FFN_ROWS), :]
                + cw_ref[1:2, :] * u_ref[pl.ds(r0 - 1, FFN_ROWS), :]
                + cw_ref[0:1, :] * u_ref[pl.ds(r0 - 2, FFN_ROWS), :])

    for rc in range(tm // FFN_ROWS):
        r0 = HALO + rc * FFN_ROWS
        act = _gelu_tanh(conv(ug_ref, r0, cwg_ref, cbg_ref)) * conv(uv_ref, r0, cwv_ref, cbv_ref)
        act_ref[rc * FFN_ROWS:(rc + 1) * FFN_ROWS, :] = act.astype(BF16)
    acc_ref[...] += jnp.dot(act_ref[...], wd_ref[...], preferred_element_type=F32)

    @pl.when(j == pl.num_programs(1) - 1)
    def _epilogue():
        for c in range(tm // NORM_ROWS):
            rows = slice(c * NORM_ROWS, (c + 1) * NORM_ROWS)
            o_ref[rows, :] = x_ref[rows, :] + mod_ref[0, 5:6, :] * _rms(acc_ref[rows, :], gpost_ref[...])


def _ffn_call(h2, x2, mod3, w_up, conv_w, conv_b, w_down, g_post, seq, tm, tf):
    t, d = x2.shape
    tps = seq // tm
    hb = tm // HALO
    nf = D_FF // tf
    return pl.pallas_call(
        functools.partial(_ffn_kernel, tiles_per_seq=tps),
        grid=(t // tm, nf),
        in_specs=[pl.BlockSpec((tm, d), lambda i, j: (i, 0)),
                  pl.BlockSpec((HALO, d), lambda i, j: (jnp.maximum(i * hb - 1, 0), 0)),
                  pl.BlockSpec((tm, d), lambda i, j: (i, 0)),
                  pl.BlockSpec((1, N_MOD, d), lambda i, j: (i // tps, 0, 0)),
                  pl.BlockSpec((d, tf), lambda i, j: (0, j)),
                  pl.BlockSpec((d, tf), lambda i, j: (0, nf + j)),
                  pl.BlockSpec((3, tf), lambda i, j: (0, j)),
                  pl.BlockSpec((3, tf), lambda i, j: (0, nf + j)),
                  pl.BlockSpec((1, tf), lambda i, j: (0, j)),
                  pl.BlockSpec((1, tf), lambda i, j: (0, nf + j)),
                  pl.BlockSpec((tf, d), lambda i, j: (j, 0)),
                  pl.BlockSpec((1, d), lambda i, j: (0, 0))],
        out_specs=pl.BlockSpec((tm, d), lambda i, j: (i, 0)),
        out_shape=jax.ShapeDtypeStruct((t, d), F32),
        scratch_shapes=[pltpu.VMEM((tm + HALO, d), BF16), pltpu.VMEM((tm, d), F32),
                        pltpu.VMEM((tm + HALO, tf), F32), pltpu.VMEM((tm + HALO, tf), F32),
                        pltpu.VMEM((tm, tf), BF16)],
        compiler_params=pltpu.CompilerParams(dimension_semantics=("arbitrary", "arbitrary"),
                                             vmem_limit_bytes=VMEM_LIMIT),
        name="ffn",
    )(h2, h2, x2, mod3, w_up, w_up, conv_w, conv_w, conv_b, conv_b, w_down, g_post)


def _pad_cols(w, n):
    return jnp.pad(w, ((0, 0), (0, n - w.shape[1])))


def _layout_w_in(w_in, mu_shift):
    w3 = 3 * RWKV_WIDTH
    o_wd = w3
    o_ad = o_wd + DECAY_LORA
    o_gd = o_ad + ICLR_LORA
    o_q = o_gd + GATE_LORA
    o_kv = o_q + Q_LORA_RANK
    o_kr = o_kv + KV_LORA_RANK
    half = QK_ROPE_DIM // 2

    def lay(m):
        kr = m[:, o_kr:o_kr + QK_ROPE_DIM]
        return jnp.concatenate([
            m[:, 0:w3],
            _pad_cols(m[:, o_wd:o_ad], LANES),
            _pad_cols(m[:, o_ad:o_gd], LANES),
            m[:, o_gd:o_q],
            m[:, o_q:o_kv],
            m[:, o_kv:o_kr],
            _pad_cols(kr, LANES),
            _pad_cols(jnp.concatenate([kr[:, half:], kr[:, :half]], axis=1), LANES),
        ], axis=1)

    mu_full = jnp.concatenate([mu_shift, jnp.zeros((w_in.shape[1] - mu_shift.shape[0],), F32)])
    w_p = lay(w_in).astype(BF16)
    mu_p = lay(mu_full[None, :])
    return w_p, mu_p


def _layout_w_q(w_q_up):
    dn, dr = QK_NOPE_DIM, QK_ROPE_DIM
    half = dr // 2
    w = w_q_up.reshape(Q_LORA_RANK, MLA_HEADS, dn + dr)
    nope, u1, u2 = w[..., :dn], w[..., dn:dn + half], w[..., dn + half:]
    z = jnp.zeros(u1.shape[:-1] + (LANES - dr,), w.dtype)
    main = jnp.concatenate([nope, u1, u2, z], axis=-1).reshape(Q_LORA_RANK, MLA_HEADS * QK_PAD)
    swap = jnp.concatenate([u2, u1, z], axis=-1).reshape(Q_LORA_RANK, MLA_HEADS * LANES)
    return jnp.concatenate([main, swap], axis=1).astype(BF16)


def _layout_w_kv(w_kv_up):
    w = w_kv_up.reshape(KV_LORA_RANK, MLA_HEADS, QK_NOPE_DIM + V_HEAD_DIM)
    kn = w[..., :QK_NOPE_DIM].reshape(KV_LORA_RANK, MLA_HEADS * QK_NOPE_DIM)
    vv = w[..., QK_NOPE_DIM:].reshape(KV_LORA_RANK, MLA_HEADS * V_HEAD_DIM)
    return jnp.concatenate([kn, vv], axis=1).astype(BF16)


def _pad_rows(w, n):
    return jnp.pad(w, ((0, n - w.shape[0]), (0, 0)))


def _block(x, c, positions, w_mod, b_mod, g_pre_mix, w_in, mu_shift, w0, w_decay_up, a0,
           w_iclr_up, w_gate_up, k_k, k_a, r_k, ln_x_g, ln_x_b, q_norm_g, w_q_up, kv_norm_g,
           w_kv_up, w_out, g_post_mix, g_pre_ffn, w_ffn_up, conv_w, conv_b, w_ffn_down,
           g_post_ffn, *, tm_in, tn_in, nb_rwkv, tm_mla, tq, tm_out, tm_ffn, tf):
    b, s, d = x.shape
    t = b * s
    x2 = x.reshape(t, d)

    c_pad = jnp.pad(c, ((0, 8 - b % 8 if b % 8 else 0), (0, 0)))
    mod = _mod_call(c_pad, w_mod, b_mod[None, :])[:b]
    mod3 = mod.reshape(b, N_MOD, d)

    w_in_p, mu_p = _layout_w_in(w_in, mu_shift)
    p2 = _inproj_call(x2, mod3, g_pre_mix[None, :], w_in_p, mu_p, s, tm_in, tn_in)

    vecs = jnp.stack([w0, a0, k_k, k_a, r_k.reshape(-1), ln_x_g, ln_x_b, jnp.zeros_like(w0)])
    y_rwkv = _rwkv_call(p2.reshape(b, s, IN_COLS_PAD), vecs,
                        _pad_rows(w_decay_up, LANES).astype(BF16),
                        _pad_rows(w_iclr_up, LANES).astype(BF16),
                        w_gate_up.astype(BF16), nb_rwkv)

    half = QK_ROPE_DIM // 2
    inv_freq = ROPE_THETA ** (-jnp.arange(0, QK_ROPE_DIM, 2, dtype=F32) / QK_ROPE_DIM)
    invf = jnp.concatenate([inv_freq, inv_freq, jnp.zeros((LANES - 2 * half,), F32)])[None, :]
    q4, k4, v4 = _mlaproj_call(p2, positions.reshape(t, 1), invf, q_norm_g[None, :],
                               kv_norm_g[None, :], _layout_w_q(w_q_up), _layout_w_kv(w_kv_up),
                               b, s, tm_mla)
    y_mla = _attn_call(q4, k4, v4, tq)

    x_mid, h_ffn = _outproj_call(y_rwkv.reshape(t, RWKV_WIDTH), y_mla.reshape(t, MLA_WIDTH), x2,
                                 mod3, g_post_mix[None, :], g_pre_ffn[None, :],
                                 w_out.astype(BF16), s, tm_out)
    out = _ffn_call(h_ffn, x_mid, mod3, w_ffn_up.astype(BF16), conv_w, conv_b[None, :],
                    w_ffn_down.astype(BF16), g_post_ffn[None, :], s, tm_ffn, tf)
    return out.reshape(b, s, d)


def kernel(x, c, positions, w_mod, b_mod, g_pre_mix, w_in, mu_shift, w0, w_decay_up, a0, w_iclr_up, w_gate_up, k_k, k_a, r_k, ln_x_g, ln_x_b, q_norm_g, w_q_up, kv_norm_g, w_kv_up, w_out, g_post_mix, g_pre_ffn, w_ffn_up, conv_w, conv_b, w_ffn_down, g_post_ffn):
    args = (x, c, positions, w_mod, b_mod, g_pre_mix, w_in, mu_shift, w0, w_decay_up, a0,
            w_iclr_up, w_gate_up, k_k, k_a, r_k, ln_x_g, ln_x_b, q_norm_g, w_q_up, kv_norm_g,
            w_kv_up, w_out, g_post_mix, g_pre_ffn, w_ffn_up, conv_w, conv_b, w_ffn_down,
            g_post_ffn)
    for l in range(w_mod.shape[0]):
        layer = [a[l] for a in args[3:]]
        x = _block(x, c, positions, *layer, tm_in=1024, tn_in=1536, nb_rwkv=2, tm_mla=512, tq=1024,
                   tm_out=512, tm_ffn=512, tf=512)
    return x
```

```python
import functools

import numpy as np
import jax
import jax.numpy as jnp
from jax import lax
from jax.experimental import pallas as pl
from jax.experimental.pallas import tpu as pltpu

F32 = jnp.float32
BF16 = jnp.bfloat16

D_MODEL = 2048
CHUNK = 64
RWKV_HEAD_DIM = 64
RWKV_WIDTH = 1024
RWKV_HEADS = 16
DECAY_LORA = 96
ICLR_LORA = 96
GATE_LORA = 256
GN_EPS = 64e-5
QK_NOPE_DIM = 128
QK_ROPE_DIM = 64
V_HEAD_DIM = 128
MLA_WIDTH = 1024
MLA_HEADS = 8
Q_LORA_RANK = 512
KV_LORA_RANK = 256
ROPE_THETA = 10000.0
D_FF = 5632
NORM_EPS = 1e-6
N_MOD = 6

LANES = 128
HALO = 16
GROUP = 256
HEADS_PER_GROUP = GROUP // RWKV_HEAD_DIM
N_GROUPS = RWKV_WIDTH // GROUP
QK_PAD = 256
V_PAD = 256
IN_COLS_PAD = 4608
VMEM_LIMIT = 56 * 1024 * 1024


def _bdot(a, b):
    return jnp.dot(a.astype(BF16), b.astype(BF16), preferred_element_type=F32)


def _split2(x):
    hi = x.astype(BF16)
    lo = (x - hi.astype(F32)).astype(BF16)
    return hi, lo


def _sigmoid(x):
    return 1.0 / (1.0 + jnp.exp(-x))


def _rms(xv, g):
    ms = jnp.mean(xv * xv, axis=-1, keepdims=True)
    return xv * lax.rsqrt(ms + NORM_EPS) * g


def _mod_kernel(c_ref, w_ref, b_ref, o_ref):
    c = c_ref[...]
    s = c * _sigmoid(c)
    o_ref[...] = _bdot(s, w_ref[...]) + b_ref[...]


def _mod_call(c_pad, w_mod, b_mod, tn=1024):
    rows, d = c_pad.shape
    n = w_mod.shape[1]
    return pl.pallas_call(
        _mod_kernel,
        grid=(n // tn,),
        in_specs=[pl.BlockSpec((rows, d), lambda j: (0, 0)),
                  pl.BlockSpec((d, tn), lambda j: (0, j)),
                  pl.BlockSpec((1, tn), lambda j: (0, j))],
        out_specs=pl.BlockSpec((rows, tn), lambda j: (0, j)),
        out_shape=jax.ShapeDtypeStruct((rows, n), F32),
        compiler_params=pltpu.CompilerParams(dimension_semantics=("arbitrary",),
                                             vmem_limit_bytes=VMEM_LIMIT),
        name="mod",
    )(c_pad, w_mod, b_mod)


def _inproj_kernel(x_ref, xh_ref, mod_ref, g_ref, w_ref, mu_ref, o_ref, *, tiles_per_seq):
    i = pl.program_id(1)
    first = (i % tiles_per_seq) == 0
    shift = mod_ref[0, 0:1, :]
    scale = mod_ref[0, 1:2, :]
    g = g_ref[...]
    h = _rms(x_ref[...], g) * (1.0 + scale) + shift
    hh = _rms(xh_ref[...], g) * (1.0 + scale) + shift
    hh = jnp.where(first, 0.0, hh)
    hcat = jnp.concatenate([hh, h], axis=0).astype(BF16)
    p = jnp.dot(hcat, w_ref[...], preferred_element_type=F32)
    prev = pltpu.roll(p, 1, axis=0)
    out = p + (prev - p) * mu_ref[...]
    o_ref[...] = out[HALO:, :]


def _inproj_call(x2, mod3, g_pre, w_in_p, mu_p, seq, tm, tn):
    t, d = x2.shape
    n = w_in_p.shape[1]
    tps = seq // tm
    hb = tm // HALO
    return pl.pallas_call(
        functools.partial(_inproj_kernel, tiles_per_seq=tps),
        grid=(n // tn, t // tm),
        in_specs=[pl.BlockSpec((tm, d), lambda j, i: (i, 0)),
                  pl.BlockSpec((HALO, d), lambda j, i: (jnp.maximum(i * hb - 1, 0), 0)),
                  pl.BlockSpec((1, N_MOD, d), lambda j, i: (i // tps, 0, 0)),
                  pl.BlockSpec((1, d), lambda j, i: (0, 0)),
                  pl.BlockSpec((d, tn), lambda j, i: (0, j)),
                  pl.BlockSpec((1, tn), lambda j, i: (0, j))],
        out_specs=pl.BlockSpec((tm, tn), lambda j, i: (i, j)),
        out_shape=jax.ShapeDtypeStruct((t, n), F32),
        compiler_params=pltpu.CompilerParams(dimension_semantics=("arbitrary", "arbitrary"),
                                             vmem_limit_bytes=VMEM_LIMIT),
        name="inproj",
    )(x2, x2, mod3, g_pre, w_in_p, mu_p)


def _rwkv_masks(nb):
    i = np.arange(GROUP)[:, None]
    j = np.arange(GROUP)[None, :]
    n = RWKV_HEAD_DIM
    bd = (i // n) == (j // n)
    mk = bd.astype(np.float32)[None]
    t = np.arange(CHUNK)[:, None]
    sj = j % n
    mc = np.stack([
        sj < t,
        sj <= t,
        sj == t,
        (sj < t) & ((t // 8) == (sj // 8)),
        ((t // 16) == (sj // 16)) & ((t // 8) > (sj // 8)),
        ((t // 32) == (sj // 32)) & ((t // 16) > (sj // 16)),
        (t // 32) > (sj // 32),
    ]).astype(np.float32)
    ti = np.arange(nb * CHUNK)[:, None]
    tj = np.arange(nb * CHUNK)[None, :]
    tri = ((ti // CHUNK == tj // CHUNK) & (tj <= ti)).astype(np.float32)
    return mk, mc, tri


def _rwkv_kernel(r_ref, k_ref, v_ref, lo_ref, vec_ref, wd_ref, wa_ref, wg_ref,
                 mk_ref, mc_ref, tri_ref, e_ref, o_ref, st_ref, *, nb):
    @pl.when(pl.program_id(1) == 0)
    def _reset_state():
        st_ref[...] = jnp.zeros_like(st_ref)

    def rows(ref):
        return jnp.concatenate([ref[bb] for bb in range(nb)], axis=0)

    r = rows(r_ref)
    k = rows(k_ref)
    v = rows(v_ref)
    lo = rows(lo_ref)
    w0 = vec_ref[0:1, :]
    a0 = vec_ref[1:2, :]
    k_k = vec_ref[2:3, :]
    k_a = vec_ref[3:4, :]
    r_k = vec_ref[4:5, :]
    ln_g = vec_ref[5:6, :]
    ln_b = vec_ref[6:7, :]

    z = -(w0 + _bdot(jnp.tanh(lo[:, 0:LANES]), wd_ref[...]))
    softplus = jnp.maximum(z, 0.0) + jnp.log(1.0 + jnp.exp(-jnp.abs(z)))
    lw = -jnp.exp(-softplus - 0.5)
    a = _sigmoid(a0 + _bdot(lo[:, LANES:2 * LANES], wa_ref[...]))
    gate = _bdot(_sigmoid(lo[:, 2 * LANES:4 * LANES]), wg_ref[...])
    kk = k * k_k
    kp = k * (1.0 + (a - 1.0) * k_a)

    tri = tri_ref[...]
    l1 = lw.astype(BF16)
    rem = lw - l1.astype(F32)
    l2 = rem.astype(BF16)
    l3 = (rem - l2.astype(F32)).astype(BF16)
    cum = (jnp.dot(tri, l1, preferred_element_type=F32)
           + jnp.dot(tri, l2, preferred_element_type=F32)
           + jnp.dot(tri, l3, preferred_element_type=F32))
    cum_last = [cum[(bb + 1) * CHUNK - 1:(bb + 1) * CHUNK, :] for bb in range(nb)]
    cum_last_rows = jnp.concatenate(
        [jnp.broadcast_to(cl, (CHUNK, cl.shape[1])) for cl in cum_last], axis=0)
    e_pos = jnp.exp(cum)
    e_pos_x = jnp.exp(cum - lw)
    e_neg = jnp.exp(-cum)
    e_end = jnp.exp(cum_last_rows - cum)

    m_bd = mk_ref[0]
    c_strict = mc_ref[0]
    c_incl = mc_ref[1]
    c_eye = mc_ref[2]
    c_b8 = mc_ref[3].astype(BF16)
    c_levels = [mc_ref[lv].astype(BF16) for lv in (4, 5, 6)]
    ones_bd = e_ref[...]

    def head_sum(x):
        return jnp.dot(x.astype(BF16), ones_bd, preferred_element_type=F32)

    def head_sum_pair(x, y):
        s2 = head_sum(jnp.concatenate([x, y], axis=0))
        return s2[0:CHUNK], s2[CHUNK:]

    def head_sum2(x):
        hi, lo_ = _split2(x)
        s2 = jnp.dot(jnp.concatenate([hi, lo_], axis=0), ones_bd, preferred_element_type=F32)
        return s2[0:CHUNK] + s2[CHUNK:]

    def bd(x):
        return jnp.concatenate([x.astype(BF16)] * HEADS_PER_GROUP, axis=0) * ones_bd

    def fold4(x):
        return (x[0:CHUNK] + x[CHUNK:2 * CHUNK] + x[2 * CHUNK:3 * CHUNK] + x[3 * CHUNK:4 * CHUNK])

    chains = [(bb, gi) for bb in range(nb) for gi in range(N_GROUPS)]
    rsl = [(slice(bb * CHUNK, (bb + 1) * CHUNK), slice(gi * GROUP, (gi + 1) * GROUP))
           for bb, gi in chains]

    def each(fn, *lists):
        return [fn(*args) for args in zip(*lists)]

    def cut(x):
        return [x[rs, sl] for rs, sl in rsl]

    r_g, v_g, kp_g, a_g, kk_g = cut(r), cut(v), cut(kp), cut(a), cut(kk)
    sums = each(lambda x, rg, kg, rsl_: head_sum_pair(x * x, rg * kg * r_k[:, rsl_[1]]),
                kk_g, r_g, kp_g, rsl)
    kkn = each(lambda x, s2: x / jnp.maximum(jnp.sqrt(s2[0]), 1e-12), kk_g, sums)
    ka = each(lambda x, y: x * y, kkn, a_g)
    a_bar = each(lambda x, e: -x * e, kkn, cut(e_pos_x))
    r_bar = each(lambda x, e: x * e, r_g, cut(e_pos))
    b_til = each(lambda x, e: x * e, ka, cut(e_neg))
    k_til = each(lambda x, e: x * e, kp_g, cut(e_neg))
    b_hat = each(lambda x, e: x * e, ka, cut(e_end))
    k_hat = each(lambda x, e: x * e, kp_g, cut(e_end))

    def score_fn(ab, rb, bt, kt):
        lhs = jnp.concatenate([ab, rb], axis=0)
        rhs = jnp.concatenate([bd(bt), bd(kt)], axis=0)
        return lax.dot_general(lhs.astype(BF16), rhs,
                               (((1,), (1,)), ((), ())), preferred_element_type=F32)

    scores = each(score_fn, a_bar, r_bar, b_til, k_til)
    a_ab = each(lambda x: (x[0:CHUNK, 0:GROUP] * c_strict).astype(BF16), scores)
    a_ak = each(lambda x: (x[0:CHUNK, GROUP:] * c_strict).astype(BF16), scores)
    a_rb = each(lambda x: (x[CHUNK:, 0:GROUP] * c_incl).astype(BF16), scores)
    a_rk = each(lambda x: (x[CHUNK:, GROUP:] * c_incl).astype(BF16), scores)

    n1 = each(lambda x: x * c_b8, a_ab)
    n2 = each(lambda x: _bdot(x, bd(x)).astype(BF16), n1)
    sq = each(lambda x, y: _bdot(jnp.concatenate([y, c_eye + x], axis=0), bd(y)), n1, n2)
    inv = each(lambda x, s2: c_eye + x + s2[CHUNK:], n1, sq)
    inv = each(lambda x, s2: x + _bdot(x, bd(s2[0:CHUNK])), inv, sq)
    for c_off in c_levels:
        tmp = each(lambda x, a: _bdot(x, bd(a * c_off)).astype(BF16), inv, a_ab)
        inv = each(lambda x, t: x + _bdot(t, bd(x)), inv, tmp)

    v_bd = each(bd, v_g)
    av = each(lambda ak, rk, vb: _bdot(jnp.concatenate([ak, rk], axis=0), vb),
              a_ak, a_rk, v_bd)
    wu = each(lambda t, ab, x: _bdot(t, jnp.concatenate([bd(ab), bd(x[0:CHUNK])], axis=1)),
              inv, a_bar, av)
    ry = each(lambda a, x: _bdot(a, jnp.concatenate([bd(x[:, 0:GROUP]), bd(x[:, GROUP:])], axis=1)),
              a_rb, wu)
    r_hat = each(lambda x, y: x + y[:, 0:GROUP], r_bar, ry)
    y_hat = each(lambda y, x: y[:, GROUP:] + x[CHUNK:], ry, av)

    state = [st_ref[bb, gi] for bb, gi in chains]
    y = each(lambda rh, st, yh: lax.dot_general(
        rh.astype(BF16), bd(st), (((1,), (1,)), ((), ())),
        preferred_element_type=F32) + yh, r_hat, state, y_hat)

    def trans_fn(bh, kh, x, vg):
        vb = vg.astype(BF16)
        lhs_t = jnp.concatenate(
            [x.astype(BF16), jnp.concatenate([jnp.zeros_like(vb), vb], axis=1)], axis=0)
        rhs_t = jnp.concatenate([bh, kh], axis=0)
        return lax.dot_general(lhs_t, rhs_t.astype(BF16),
                               (((0,), (0,)), ((), ())), preferred_element_type=F32)

    mc = each(trans_fn, b_hat, k_hat, wu, v_g)
    for (bb, gi), (_, sl), mci, st in zip(chains, rsl, mc, state):
        st_ref[bb, gi] = (st * jnp.exp(cum_last[bb][:, sl])
                          + _bdot(st, mci[0:GROUP, :] * m_bd) + fold4(mci[GROUP:, :] * m_bd))

    inv_n = 1.0 / RWKV_HEAD_DIM
    mean = each(lambda x: head_sum2(x) * inv_n, y)
    dlt = each(lambda x, m: x - m, y, mean)
    var = each(lambda x: head_sum(x * x) * inv_n, dlt)
    bonus = each(lambda s2, vg: s2[1] * vg, sums, v_g)
    for (bb, gi), (rs, sl), d, vr, bo in zip(chains, rsl, dlt, var, bonus):
        yn = d * lax.rsqrt(vr + GN_EPS) * ln_g[:, sl] + ln_b[:, sl]
        o_ref[bb, :, sl] = ((yn + bo) * gate[rs, sl]).astype(o_ref.dtype)


def _rwkv_call(p3, vecs, wd, wa, wg, nb):
    b, s, _ = p3.shape
    w = RWKV_WIDTH
    mk, mc, tri = _rwkv_masks(nb)
    mk = jnp.asarray(mk)
    mc = jnp.asarray(mc)
    tri = jnp.asarray(tri, dtype=BF16)
    ones_bd = mk[0].astype(BF16)
    const2 = lambda bi, ci: (0, 0)
    const3 = lambda bi, ci: (0, 0, 0)
    return pl.pallas_call(
        functools.partial(_rwkv_kernel, nb=nb),
        grid=(b // nb, s // CHUNK),
        in_specs=[pl.BlockSpec((nb, CHUNK, w), lambda bi, ci: (bi, ci, 0)),
                  pl.BlockSpec((nb, CHUNK, w), lambda bi, ci: (bi, ci, 1)),
                  pl.BlockSpec((nb, CHUNK, w), lambda bi, ci: (bi, ci, 2)),
                  pl.BlockSpec((nb, CHUNK, 4 * LANES), lambda bi, ci: (bi, ci, 3 * w // (4 * LANES))),
                  pl.BlockSpec(vecs.shape, const2),
                  pl.BlockSpec(wd.shape, const2),
                  pl.BlockSpec(wa.shape, const2),
                  pl.BlockSpec(wg.shape, const2),
                  pl.BlockSpec(mk.shape, const3),
                  pl.BlockSpec(mc.shape, const3),
                  pl.BlockSpec(tri.shape, const2),
                  pl.BlockSpec(ones_bd.shape, const2)],
        out_specs=pl.BlockSpec((nb, CHUNK, w), lambda bi, ci: (bi, ci, 0)),
        out_shape=jax.ShapeDtypeStruct((b, s, w), BF16),
        scratch_shapes=[pltpu.VMEM((nb, N_GROUPS, CHUNK, GROUP), F32)],
        compiler_params=pltpu.CompilerParams(dimension_semantics=("arbitrary", "arbitrary"),
                                             vmem_limit_bytes=VMEM_LIMIT),
        name="rwkv",
    )(p3, p3, p3, p3, vecs, wd, wa, wg, mk, mc, tri, ones_bd)


def _mlaproj_kernel(pq_ref, pkv_ref, pos_ref, invf_ref, gq_ref, gkv_ref, wq_ref, wkv_ref,
                    q_ref, k_ref, v_ref):
    q = _bdot(_rms(pq_ref[...], gq_ref[...]), wq_ref[...])
    pkv = pkv_ref[...]
    kv = _bdot(_rms(pkv[:, 0:KV_LORA_RANK], gkv_ref[...]), wkv_ref[...])
    ang = pos_ref[...].astype(F32) * invf_ref[...]
    lane = lax.broadcasted_iota(jnp.int32, ang.shape, 1)
    half = QK_ROPE_DIM // 2
    cos_f = jnp.where(lane < QK_ROPE_DIM, jnp.cos(ang), 0.0)
    sin = jnp.sin(ang)
    sin_f = jnp.where(lane < half, -sin, jnp.where(lane < QK_ROPE_DIM, sin, 0.0))
    k_rot = pkv[:, KV_LORA_RANK:KV_LORA_RANK + LANES] * cos_f + pkv[:, KV_LORA_RANK + LANES:] * sin_f
    scale = (QK_NOPE_DIM + QK_ROPE_DIM) ** -0.5 * 1.4426950408889634
    swap0 = MLA_HEADS * QK_PAD
    ones_col = (lane == 0).astype(F32)
    for h in range(MLA_HEADS):
        q_nope = q[:, h * QK_PAD:h * QK_PAD + LANES]
        q_rot = (q[:, h * QK_PAD + LANES:(h + 1) * QK_PAD] * cos_f
                 + q[:, swap0 + h * LANES:swap0 + (h + 1) * LANES] * sin_f)
        q_ref[0, h] = (jnp.concatenate([q_nope, q_rot], axis=1) * scale).astype(q_ref.dtype)
        k_ref[0, h] = jnp.concatenate([kv[:, h * LANES:(h + 1) * LANES], k_rot], axis=1).astype(k_ref.dtype)
        v_ref[0, h] = jnp.concatenate(
            [kv[:, (MLA_HEADS + h) * LANES:(MLA_HEADS + h + 1) * LANES], ones_col],
            axis=1).astype(v_ref.dtype)


def _mlaproj_call(p2, pos2, invf, gq, gkv, wq, wkv, batch, seq, tm):
    t = p2.shape[0]
    tps = seq // tm
    blk = 4 * LANES
    q_blk = (IN_COLS_PAD - 2 * blk) // blk
    hd = MLA_HEADS
    out_map = lambda i: (i // tps, 0, i % tps, 0)
    const = lambda i: (0, 0)
    return pl.pallas_call(
        _mlaproj_kernel,
        grid=(t // tm,),
        in_specs=[pl.BlockSpec((tm, blk), lambda i: (i, q_blk)),
                  pl.BlockSpec((tm, blk), lambda i: (i, q_blk + 1)),
                  pl.BlockSpec((tm, 1), lambda i: (i, 0)),
                  pl.BlockSpec(invf.shape, const),
                  pl.BlockSpec(gq.shape, const),
                  pl.BlockSpec(gkv.shape, const),
                  pl.BlockSpec(wq.shape, const),
                  pl.BlockSpec(wkv.shape, const)],
        out_specs=[pl.BlockSpec((1, hd, tm, QK_PAD), out_map),
                   pl.BlockSpec((1, hd, tm, QK_PAD), out_map),
                   pl.BlockSpec((1, hd, tm, V_PAD), out_map)],
        out_shape=[jax.ShapeDtypeStruct((batch, hd, seq, QK_PAD), BF16),
                   jax.ShapeDtypeStruct((batch, hd, seq, QK_PAD), BF16),
                   jax.ShapeDtypeStruct((batch, hd, seq, V_PAD), BF16)],
        compiler_params=pltpu.CompilerParams(dimension_semantics=("arbitrary",),
                                             vmem_limit_bytes=VMEM_LIMIT),
        name="mlaproj",
    )(p2, p2, pos2, invf, gq, gkv, wq, wkv)


ATTN_ROW_PARTS = 4
ATTN_SM_ROWS = 32
ATTN_HEADS_PER_STEP = 2


def _attn_kernel(q_ref, k_ref, v_ref, o_ref, s_ref, p_ref, m_ref, a_ref, acc_ref, *, tq):
    i = pl.program_id(2)
    rp = tq // ATTN_ROW_PARTS
    heads = range(ATTN_HEADS_PER_STEP)
    m_ref[...] = jnp.full(m_ref.shape, -jnp.inf, F32)
    acc_ref[...] = jnp.zeros_like(acc_ref)

    def scores(hh, start, part, nk):
        rows = slice(part * rp, (part + 1) * rp)
        s_ref[hh, rows, 0:nk] = lax.dot_general(
            q_ref[0, hh, rows, :], k_ref[0, hh, pl.ds(start, nk), :],
            (((1,), (1,)), ((), ())), preferred_element_type=F32)

    def block(start, next_start, on_diagonal):
        def keys(part):
            return (part + 1) * rp if on_diagonal else tq

        def softmax(hh, part):
            nk = keys(part)
            for c in range(rp // ATTN_SM_ROWS):
                r0 = part * rp + c * ATTN_SM_ROWS
                rows = slice(r0, r0 + ATTN_SM_ROWS)
                sc = s_ref[hh, rows, 0:nk]
                if on_diagonal:
                    row = (r0 + lax.broadcasted_iota(jnp.int32, sc.shape, 0)) // CHUNK
                    col = lax.broadcasted_iota(jnp.int32, sc.shape, 1) // CHUNK
                    sc = jnp.where(row >= col, sc, -jnp.inf)
                m_old = m_ref[hh, rows, :]
                m_new = jnp.maximum(m_old, jnp.max(sc, axis=-1, keepdims=True))
                m_ref[hh, rows, :] = m_new
                a_ref[hh, rows, :] = jnp.exp2(m_old - m_new)
                p_ref[hh, rows, 0:nk] = jnp.exp2(sc - m_new).astype(BF16)

        def weighted_values(hh, part):
            rows = slice(part * rp, (part + 1) * rp)
            nk = keys(part)
            acc_ref[hh, rows, :] = a_ref[hh, rows, :] * acc_ref[hh, rows, :] + jnp.dot(
                p_ref[hh, rows, 0:nk], v_ref[0, hh, pl.ds(start, nk), :], preferred_element_type=F32)

        for part in range(ATTN_ROW_PARTS):
            for hh in heads:
                if part + 1 < ATTN_ROW_PARTS:
                    scores(hh, start, part + 1, keys(part + 1))
                elif not on_diagonal:
                    scores(hh, next_start, 0, tq)
            for hh in heads:
                softmax(hh, part)
                weighted_values(hh, part)

    def body(j, carry):
        block(pl.multiple_of(j * tq, tq), pl.multiple_of((j + 1) * tq, tq), False)
        return carry

    for hh in heads:
        scores(hh, 0, 0, tq)
    lax.fori_loop(0, i, body, 0)
    block(pl.multiple_of(i * tq, tq), None, True)
    for hh in heads:
        acc = acc_ref[hh]
        o_ref[0, :, hh * V_HEAD_DIM:(hh + 1) * V_HEAD_DIM] = (
            acc[:, 0:V_HEAD_DIM] / acc[:, V_HEAD_DIM:V_HEAD_DIM + 1]).astype(o_ref.dtype)


def _attn_call(q4, k4, v4, tq):
    b, h, s, _ = q4.shape
    hp = ATTN_HEADS_PER_STEP
    return pl.pallas_call(
        functools.partial(_attn_kernel, tq=tq),
        grid=(b, h // hp, s // tq),
        in_specs=[pl.BlockSpec((1, hp, tq, QK_PAD), lambda bi, hi, i: (bi, hi, i, 0)),
                  pl.BlockSpec((1, hp, s, QK_PAD), lambda bi, hi, i: (bi, hi, 0, 0)),
                  pl.BlockSpec((1, hp, s, V_PAD), lambda bi, hi, i: (bi, hi, 0, 0))],
        out_specs=pl.BlockSpec((1, tq, hp * V_HEAD_DIM), lambda bi, hi, i: (bi, i, hi)),
        out_shape=jax.ShapeDtypeStruct((b, s, h * V_HEAD_DIM), BF16),
        scratch_shapes=[pltpu.VMEM((hp, tq, tq), F32), pltpu.VMEM((hp, tq, tq), BF16),
                        pltpu.VMEM((hp, tq, 1), F32), pltpu.VMEM((hp, tq, 1), F32),
                        pltpu.VMEM((hp, tq, V_PAD), F32)],
        compiler_params=pltpu.CompilerParams(
            dimension_semantics=("arbitrary", "arbitrary", "arbitrary"),
            vmem_limit_bytes=VMEM_LIMIT),
        name="attn",
    )(q4, k4, v4)


NORM_ROWS = 32


def _outproj_kernel(yr_ref, ym_ref, x_ref, mod_ref, g_ref, gffn_ref, w_ref, o_ref, h_ref):
    half = RWKV_WIDTH
    o = (jnp.dot(yr_ref[...], w_ref[0:half, :], preferred_element_type=F32)
         + jnp.dot(ym_ref[...], w_ref[half:, :], preferred_element_type=F32))
    x_mid = x_ref[...] + mod_ref[0, 2:3, :] * _rms(o, g_ref[...])
    o_ref[...] = x_mid
    h = _rms(x_mid, gffn_ref[...]) * (1.0 + mod_ref[0, 4:5, :]) + mod_ref[0, 3:4, :]
    h_ref[...] = h.astype(h_ref.dtype)


def _outproj_call(yr, ym, x2, mod3, g_post, g_pre_ffn, w_out, seq, tm):
    t, d = x2.shape
    tps = seq // tm
    return pl.pallas_call(
        _outproj_kernel,
        grid=(t // tm,),
        in_specs=[pl.BlockSpec((tm, RWKV_WIDTH), lambda i: (i, 0)),
                  pl.BlockSpec((tm, MLA_WIDTH), lambda i: (i, 0)),
                  pl.BlockSpec((tm, d), lambda i: (i, 0)),
                  pl.BlockSpec((1, N_MOD, d), lambda i: (i // tps, 0, 0)),
                  pl.BlockSpec((1, d), lambda i: (0, 0)),
                  pl.BlockSpec((1, d), lambda i: (0, 0)),
                  pl.BlockSpec(w_out.shape, lambda i: (0, 0))],
        out_specs=[pl.BlockSpec((tm, d), lambda i: (i, 0)),
                   pl.BlockSpec((tm, d), lambda i: (i, 0))],
        out_shape=[jax.ShapeDtypeStruct((t, d), F32), jax.ShapeDtypeStruct((t, d), BF16)],
        compiler_params=pltpu.CompilerParams(dimension_semantics=("arbitrary",),
                                             vmem_limit_bytes=VMEM_LIMIT),
        name="outproj",
    )(yr, ym, x2, mod3, g_post, g_pre_ffn, w_out)


FFN_ROWS = 128


def _gelu_tanh(x):
    c = 0.7978845608028654
    return 0.5 * x * (1.0 + jnp.tanh(c * (x + 0.044715 * (x * x * x))))


def _ffn_kernel(h_ref, hh_ref, x_ref, mod_ref, wg_ref, wv_ref, cwg_ref, cwv_ref,
                cbg_ref, cbv_ref, wd_ref, gpost_ref, o_ref, hcat_ref, acc_ref, ug_ref, uv_ref, act_ref,
                *, tiles_per_seq):
    i = pl.program_id(0)
    j = pl.program_id(1)
    tm = act_ref.shape[0]

    @pl.when(j == 0)
    def _prologue():
        keep = (i % tiles_per_seq) != 0
        hcat_ref[0:HALO, :] = jnp.where(keep, hh_ref[...], jnp.zeros_like(hh_ref))
        hcat_ref[HALO:, :] = h_ref[...]
        acc_ref[...] = jnp.zeros_like(acc_ref)

    hb = hcat_ref[...]
    ug_ref[...] = jnp.dot(hb, wg_ref[...], preferred_element_type=F32)
    uv_ref[...] = jnp.dot(hb, wv_ref[...], preferred_element_type=F32)

    def conv(u_ref, r0, cw_ref, cb_ref):
        return (cb_ref[...] + cw_ref[2:3, :] * u_ref[pl.ds(r0, FFN_ROWS), :]
                + cw_ref[1:2, :] * u_ref[pl.ds(r0 - 1, FFN_ROWS), :]
                + cw_ref[0:1, :] * u_ref[pl.ds(r0 - 2, FFN_ROWS), :])

    for rc in range(tm // FFN_ROWS):
        r0 = HALO + rc * FFN_ROWS
        act = _gelu_tanh(conv(ug_ref, r0, cwg_ref, cbg_ref)) * conv(uv_ref, r0, cwv_ref, cbv_ref)
        act_ref[rc * FFN_ROWS:(rc + 1) * FFN_ROWS, :] = act.astype(BF16)
    acc_ref[...] += jnp.dot(act_ref[...], wd_ref[...], preferred_element_type=F32)

    @pl.when(j == pl.num_programs(1) - 1)
    def _epilogue():
        for c in range(tm // NORM_ROWS):
            rows = slice(c * NORM_ROWS, (c + 1) * NORM_ROWS)
            o_ref[rows, :] = x_ref[rows, :] + mod_ref[0, 5:6, :] * _rms(acc_ref[rows, :], gpost_ref[...])


def _ffn_call(h2, x2, mod3, w_up, conv_w, conv_b, w_down, g_post, seq, tm, tf):
    t, d = x2.shape
    tps = seq // tm
    hb = tm // HALO
    nf = D_FF // tf
    return pl.pallas_call(
        functools.partial(_ffn_kernel, tiles_per_seq=tps),
        grid=(t // tm, nf),
        in_specs=[pl.BlockSpec((tm, d), lambda i, j: (i, 0)),
                  pl.BlockSpec((HALO, d), lambda i, j: (jnp.maximum(i * hb - 1, 0), 0)),
                  pl.BlockSpec((tm, d), lambda i, j: (i, 0)),
                  pl.BlockSpec((1, N_MOD, d), lambda i, j: (i // tps, 0, 0)),
                  pl.BlockSpec((d, tf), lambda i, j: (0, j)),
                  pl.BlockSpec((d, tf), lambda i, j: (0, nf + j)),
                  pl.BlockSpec((3, tf), lambda i, j: (0, j)),
                  pl.BlockSpec((3, tf), lambda i, j: (0, nf + j)),
                  pl.BlockSpec((1, tf), lambda i, j: (0, j)),
                  pl.BlockSpec((1, tf), lambda i, j: (0, nf + j)),
                  pl.BlockSpec((tf, d), lambda i, j: (j, 0)),
                  pl.BlockSpec((1, d), lambda i, j: (0, 0))],
        out_specs=pl.BlockSpec((tm, d), lambda i, j: (i, 0)),
        out_shape=jax.ShapeDtypeStruct((t, d), F32),
        scratch_shapes=[pltpu.VMEM((tm + HALO, d), BF16), pltpu.VMEM((tm, d), F32),
                        pltpu.VMEM((tm + HALO, tf), F32), pltpu.VMEM((tm + HALO, tf), F32),
                        pltpu.VMEM((tm, tf), BF16)],
        compiler_params=pltpu.CompilerParams(dimension_semantics=("arbitrary", "arbitrary"),
                                             vmem_limit_bytes=VMEM_LIMIT),
        name="ffn",
    )(h2, h2, x2, mod3, w_up, w_up, conv_w, conv_w, conv_b, conv_b, w_down, g_post)


def _pad_cols(w, n):
    return jnp.pad(w, ((0, 0), (0, n - w.shape[1])))


def _layout_w_in(w_in, mu_shift):
    w3 = 3 * RWKV_WIDTH
    o_wd = w3
    o_ad = o_wd + DECAY_LORA
    o_gd = o_ad + ICLR_LORA
    o_q = o_gd + GATE_LORA
    o_kv = o_q + Q_LORA_RANK
    o_kr = o_kv + KV_LORA_RANK
    half = QK_ROPE_DIM // 2

    def lay(m):
        kr = m[:, o_kr:o_kr + QK_ROPE_DIM]
        return jnp.concatenate([
            m[:, 0:w3],
            _pad_cols(m[:, o_wd:o_ad], LANES),
            _pad_cols(m[:, o_ad:o_gd], LANES),
            m[:, o_gd:o_q],
            m[:, o_q:o_kv],
            m[:, o_kv:o_kr],
            _pad_cols(kr, LANES),
            _pad_cols(jnp.concatenate([kr[:, half:], kr[:, :half]], axis=1), LANES),
        ], axis=1)

    mu_full = jnp.concatenate([mu_shift, jnp.zeros((w_in.shape[1] - mu_shift.shape[0],), F32)])
    w_p = lay(w_in).astype(BF16)
    mu_p = lay(mu_full[None, :])
    return w_p, mu_p


def _layout_w_q(w_q_up):
    dn, dr = QK_NOPE_DIM, QK_ROPE_DIM
    half = dr // 2
    w = w_q_up.reshape(Q_LORA_RANK, MLA_HEADS, dn + dr)
    nope, u1, u2 = w[..., :dn], w[..., dn:dn + half], w[..., dn + half:]
    z = jnp.zeros(u1.shape[:-1] + (LANES - dr,), w.dtype)
    main = jnp.concatenate([nope, u1, u2, z], axis=-1).reshape(Q_LORA_RANK, MLA_HEADS * QK_PAD)
    swap = jnp.concatenate([u2, u1, z], axis=-1).reshape(Q_LORA_RANK, MLA_HEADS * LANES)
    return jnp.concatenate([main, swap], axis=1).astype(BF16)


def _layout_w_kv(w_kv_up):
    w = w_kv_up.reshape(KV_LORA_RANK, MLA_HEADS, QK_NOPE_DIM + V_HEAD_DIM)
    kn = w[..., :QK_NOPE_DIM].reshape(KV_LORA_RANK, MLA_HEADS * QK_NOPE_DIM)
    vv = w[..., QK_NOPE_DIM:].reshape(KV_LORA_RANK, MLA_HEADS * V_HEAD_DIM)
    return jnp.concatenate([kn, vv], axis=1).astype(BF16)


def _pad_rows(w, n):
    return jnp.pad(w, ((0, n - w.shape[0]), (0, 0)))


def _block(x, c, positions, w_mod, b_mod, g_pre_mix, w_in, mu_shift, w0, w_decay_up, a0,
           w_iclr_up, w_gate_up, k_k, k_a, r_k, ln_x_g, ln_x_b, q_norm_g, w_q_up, kv_norm_g,
           w_kv_up, w_out, g_post_mix, g_pre_ffn, w_ffn_up, conv_w, conv_b, w_ffn_down,
           g_post_ffn, *, tm_in, tn_in, nb_rwkv, tm_mla, tq, tm_out, tm_ffn, tf):
    b, s, d = x.shape
    t = b * s
    x2 = x.reshape(t, d)

    c_pad = jnp.pad(c, ((0, 8 - b % 8 if b % 8 else 0), (0, 0)))
    mod = _mod_call(c_pad, w_mod, b_mod[None, :])[:b]
    mod3 = mod.reshape(b, N_MOD, d)

    w_in_p, mu_p = _layout_w_in(w_in, mu_shift)
    p2 = _inproj_call(x2, mod3, g_pre_mix[None, :], w_in_p, mu_p, s, tm_in, tn_in)

    vecs = jnp.stack([w0, a0, k_k, k_a, r_k.reshape(-1), ln_x_g, ln_x_b, jnp.zeros_like(w0)])
    y_rwkv = _rwkv_call(p2.reshape(b, s, IN_COLS_PAD), vecs,
                        _pad_rows(w_decay_up, LANES).astype(BF16),
                        _pad_rows(w_iclr_up, LANES).astype(BF16),
                        w_gate_up.astype(BF16), nb_rwkv)

    half = QK_ROPE_DIM // 2
    inv_freq = ROPE_THETA ** (-jnp.arange(0, QK_ROPE_DIM, 2, dtype=F32) / QK_ROPE_DIM)
    invf = jnp.concatenate([inv_freq, inv_freq, jnp.zeros((LANES - 2 * half,), F32)])[None, :]
    q4, k4, v4 = _mlaproj_call(p2, positions.reshape(t, 1), invf, q_norm_g[None, :],
                               kv_norm_g[None, :], _layout_w_q(w_q_up), _layout_w_kv(w_kv_up),
                               b, s, tm_mla)
    y_mla = _attn_call(q4, k4, v4, tq)

    x_mid, h_ffn = _outproj_call(y_rwkv.reshape(t, RWKV_WIDTH), y_mla.reshape(t, MLA_WIDTH), x2,
                                 mod3, g_post_mix[None, :], g_pre_ffn[None, :],
                                 w_out.astype(BF16), s, tm_out)
    out = _ffn_call(h_ffn, x_mid, mod3, w_ffn_up.astype(BF16), conv_w, conv_b[None, :],
                    w_ffn_down.astype(BF16), g_post_ffn[None, :], s, tm_ffn, tf)
    return out.reshape(b, s, d)


def kernel(x, c, positions, w_mod, b_mod, g_pre_mix, w_in, mu_shift, w0, w_decay_up, a0, w_iclr_up, w_gate_up, k_k, k_a, r_k, ln_x_g, ln_x_b, q_norm_g, w_q_up, kv_norm_g, w_kv_up, w_out, g_post_mix, g_pre_ffn, w_ffn_up, conv_w, conv_b, w_ffn_down, g_post_ffn):
    args = (x, c, positions, w_mod, b_mod, g_pre_mix, w_in, mu_shift, w0, w_decay_up, a0,
            w_iclr_up, w_gate_up, k_k, k_a, r_k, ln_x_g, ln_x_b, q_norm_g, w_q_up, kv_norm_g,
            w_kv_up, w_out, g_post_mix, g_pre_ffn, w_ffn_up, conv_w, conv_b, w_ffn_down,
            g_post_ffn)
    for l in range(w_mod.shape[0]):
        layer = [a[l] for a in args[3:]]
        x = _block(x, c, positions, *layer, tm_in=1024, tn_in=1536, nb_rwkv=4, tm_mla=512, tq=1024,
                   tm_out=512, tm_ffn=512, tf=512)
    return x
```
